```python
import jax, jax.numpy as jnp
from jax import lax
import numpy as np

D_MODEL = 1024
BATCH = 2
SEQ = 8192
DEPTH = 1

CHUNK = 64
N_LEFT_CHUNKS = 8
BAND = (N_LEFT_CHUNKS + 1) * CHUNK
MIX_WIDTH = D_MODEL
CONV_WIDTH = MIX_WIDTH // 2
ATTN_WIDTH = MIX_WIDTH - CONV_WIDTH
HEAD_DIM = 64
N_HEADS = ATTN_WIDTH // HEAD_DIM
CONV_KERNEL = 31
MAX_REL = 128
D_FF = 2816
FFN_CONV_KERNEL = 3
IN_COLS = 2 * CONV_WIDTH + 3 * ATTN_WIDTH
EPS = 1e-6
NEG_INF = -1e30

kernel_name = "chunk_causal_conformer_hybrid_block"


def rms_norm(x, g):
    xf = x.astype(jnp.float32)
    y = xf * lax.rsqrt(jnp.mean(xf * xf, axis=-1, keepdims=True) + EPS)
    return (y * g.astype(jnp.float32)).astype(x.dtype)


def layer_norm(x, g, b):
    xf = x.astype(jnp.float32)
    mu = jnp.mean(xf, axis=-1, keepdims=True)
    xc = xf - mu
    var = jnp.mean(xc * xc, axis=-1, keepdims=True)
    y = xc * lax.rsqrt(var + EPS) * g.astype(jnp.float32) + b.astype(jnp.float32)
    return y.astype(x.dtype)


def causal_depthwise_conv(x, w, b):
    k = w.shape[0]
    c = x.shape[-1]
    y = lax.conv_general_dilated(
        x, w[:, None, :].astype(x.dtype), window_strides=(1,), padding=[(k - 1, 0)],
        dimension_numbers=("NWC", "WIO", "NWC"), feature_group_count=c)
    return y + b.astype(x.dtype)


def conformer_conv_group(a_val, a_gate, dw_w, dw_b, ln_g, ln_b):
    h = a_val * jax.nn.sigmoid(a_gate)
    h = causal_depthwise_conv(h, dw_w, dw_b)
    h = layer_norm(h, ln_g, ln_b)
    return jax.nn.silu(h)


def rel_bias_band(rel_table):
    i = np.arange(CHUNK)[:, None]
    j = np.arange(BAND)[None, :]
    rel = N_LEFT_CHUNKS * CHUNK + i - j
    idx = np.clip(rel, -MAX_REL, MAX_REL) + MAX_REL
    return rel_table[:, idx]


def chunked_band_attention(q, k, v, rel_table):
    b, t, _ = q.shape
    nc = t // CHUNK
    pad = N_LEFT_CHUNKS * CHUNK
    q = q.reshape(b, nc, CHUNK, N_HEADS, HEAD_DIM)

    def band(z):
        z = z.reshape(b, t, N_HEADS, HEAD_DIM)
        z = jnp.pad(z, ((0, 0), (pad, 0), (0, 0), (0, 0)))
        zc = z.reshape(b, nc + N_LEFT_CHUNKS, CHUNK, N_HEADS, HEAD_DIM)
        return jnp.concatenate([zc[:, s:s + nc] for s in range(N_LEFT_CHUNKS + 1)], axis=2)

    kb = band(k)
    vb = band(v)
    scale = HEAD_DIM ** -0.5
    scores = jnp.einsum("bnqhd,bnkhd->bhnqk", q, kb,
                        preferred_element_type=jnp.float32) * scale
    scores = scores + rel_bias_band(rel_table).astype(jnp.float32)[None, :, None]
    key_pos = (jnp.arange(nc)[:, None] - N_LEFT_CHUNKS) * CHUNK + jnp.arange(BAND)[None, :]
    valid = (key_pos >= 0)[None, None, :, None, :]
    scores = jnp.where(valid, scores, NEG_INF)
    p = jax.nn.softmax(scores, axis=-1)
    out = jnp.einsum("bhnqk,bnkhd->bnqhd", p.astype(vb.dtype), vb)
    return out.reshape(b, t, ATTN_WIDTH)


def conv_gated_ffn(u, w_up, dw_w, dw_b, w_down):
    h = u @ w_up
    h = causal_depthwise_conv(h, dw_w, dw_b)
    gate, val = jnp.split(h, 2, axis=-1)
    return (jax.nn.gelu(gate) * val) @ w_down


def setup_inputs(seed: int = 0) -> dict:
    key = jax.random.key(seed)
    ks = jax.random.split(key, 20)
    L = DEPTH
    nrm = jax.random.normal
    f32 = jnp.float32
    return {
        "x": nrm(ks[0], (BATCH, SEQ, D_MODEL), f32),
        "norm_mix_pre": 1.0 + 0.05 * nrm(ks[1], (L, D_MODEL), f32),
        "w_in": nrm(ks[2], (L, D_MODEL, IN_COLS), f32) * D_MODEL ** -0.5,
        "conv_dw_w": nrm(ks[3], (L, CONV_KERNEL, CONV_WIDTH), f32) * CONV_KERNEL ** -0.5,
        "conv_dw_b": 0.01 * nrm(ks[4], (L, CONV_WIDTH), f32),
        "conv_ln_g": 1.0 + 0.05 * nrm(ks[5], (L, CONV_WIDTH), f32),
        "conv_ln_b": 0.01 * nrm(ks[6], (L, CONV_WIDTH), f32),
        "rel_bias": 0.5 * nrm(ks[7], (L, N_HEADS, 2 * MAX_REL + 1), f32),
        "w_out": nrm(ks[8], (L, MIX_WIDTH, D_MODEL), f32) * MIX_WIDTH ** -0.5,
        "norm_mix_post": 1.0 + 0.05 * nrm(ks[9], (L, D_MODEL), f32),
        "norm_ffn_pre": 1.0 + 0.05 * nrm(ks[10], (L, D_MODEL), f32),
        "w_up": nrm(ks[11], (L, D_MODEL, 2 * D_FF), f32) * D_MODEL ** -0.5,
        "ffn_dw_w": nrm(ks[12], (L, FFN_CONV_KERNEL, 2 * D_FF), f32) * FFN_CONV_KERNEL ** -0.5,
        "ffn_dw_b": 0.01 * nrm(ks[13], (L, 2 * D_FF), f32),
        "w_down": nrm(ks[14], (L, D_FF, D_MODEL), f32) * D_FF ** -0.5,
        "norm_ffn_post": 1.0 + 0.05 * nrm(ks[15], (L, D_MODEL), f32),
    }


def reference(x, norm_mix_pre, w_in, conv_dw_w, conv_dw_b, conv_ln_g, conv_ln_b,
              rel_bias, w_out, norm_mix_post, norm_ffn_pre, w_up, ffn_dw_w, ffn_dw_b,
              w_down, norm_ffn_post):
    h = x
    for l in range(DEPTH):
        u = rms_norm(h, norm_mix_pre[l])
        proj = u @ w_in[l]
        a_val, a_gate, q, k, v = jnp.split(
            proj, np.cumsum([CONV_WIDTH, CONV_WIDTH, ATTN_WIDTH, ATTN_WIDTH]).tolist(), axis=-1)
        conv_out = conformer_conv_group(a_val, a_gate, conv_dw_w[l], conv_dw_b[l],
                                        conv_ln_g[l], conv_ln_b[l])
        attn_out = chunked_band_attention(q, k, v, rel_bias[l])
        mixed = jnp.concatenate([conv_out, attn_out], axis=-1) @ w_out[l]
        h = h + rms_norm(mixed, norm_mix_post[l])
        u = rms_norm(h, norm_ffn_pre[l])
        f = conv_gated_ffn(u, w_up[l], ffn_dw_w[l], ffn_dw_b[l], w_down[l])
        h = h + rms_norm(f, norm_ffn_post[l])
    return h
```

```python
import functools

import jax
import jax.numpy as jnp
import numpy as np
from jax import lax
from jax.experimental import pallas as pl
from jax.experimental.pallas import tpu as pltpu

D_MODEL = 1024
CHUNK = 64
N_LEFT_CHUNKS = 8
BAND = (N_LEFT_CHUNKS + 1) * CHUNK
CONV_WIDTH = 512
ATTN_WIDTH = 512
HEAD_DIM = 64
N_HEADS = ATTN_WIDTH // HEAD_DIM
N_PAIRS = N_HEADS // 2
PAIR_W = 2 * HEAD_DIM
CONV_KERNEL = 31
MAX_REL = 128
D_FF = 2816
FFN_CONV_KERNEL = 3
EPS = 1e-6
NEG_INF = -1e30

TM_MIX = 512
HALO_H = 32
HALO_KV = N_LEFT_CHUNKS * CHUNK
CONV_RB = 32
TM_FFN = 512
FFN_CW = 256
FFN_HALO = 8
VMEM_LIMIT = 56 * 1024 * 1024


def _rms(xf, g):
    return xf * lax.rsqrt(jnp.mean(xf * xf, axis=-1, keepdims=True) + EPS) * g


def _dot(a, b):
    return jnp.dot(a, b, preferred_element_type=jnp.float32)


def _mixer_kernel(x_ref, g_pre_ref, w_in_ref, dw_w_ref, dw_b_ref, ln_g_ref, ln_b_ref,
                  bias_ref, w_out_ref, g_post_ref, o_ref,
                  hbuf, qbuf, kbuf, vbuf, mixbuf):
    i = pl.program_id(1)

    @pl.when(i == 0)
    def _():
        hbuf[0:HALO_H, :] = jnp.zeros((HALO_H, CONV_WIDTH), jnp.float32)
        kbuf[0:HALO_KV, :] = jnp.zeros((HALO_KV, ATTN_WIDTH), jnp.bfloat16)
        vbuf[0:HALO_KV, :] = jnp.zeros((HALO_KV, ATTN_WIDTH), jnp.bfloat16)

    xt = x_ref[...]
    u = _rms(xt, g_pre_ref[...]).astype(jnp.bfloat16)

    a_val = _dot(u, w_in_ref[:, 0:CONV_WIDTH])
    a_gate = _dot(u, w_in_ref[:, CONV_WIDTH:2 * CONV_WIDTH])
    hbuf[HALO_H:HALO_H + TM_MIX, :] = a_val * jax.nn.sigmoid(a_gate)
    c0 = 2 * CONV_WIDTH
    q = _dot(u, w_in_ref[:, c0:c0 + ATTN_WIDTH])
    qbuf[...] = (q * (HEAD_DIM ** -0.5)).astype(jnp.bfloat16)
    k = _dot(u, w_in_ref[:, c0 + ATTN_WIDTH:c0 + 2 * ATTN_WIDTH])
    kbuf[HALO_KV:HALO_KV + TM_MIX, :] = k.astype(jnp.bfloat16)
    v = _dot(u, w_in_ref[:, c0 + 2 * ATTN_WIDTH:c0 + 3 * ATTN_WIDTH])
    vbuf[HALO_KV:HALO_KV + TM_MIX, :] = v.astype(jnp.bfloat16)

    for rb in range(TM_MIX // CONV_RB):
        base = rb * CONV_RB
        acc = jnp.broadcast_to(dw_b_ref[...], (CONV_RB, CONV_WIDTH))
        for j in range(CONV_KERNEL):
            off = base + HALO_H - (CONV_KERNEL - 1) + j
            acc = acc + dw_w_ref[j:j + 1, :] * hbuf[off:off + CONV_RB, :]
        mu = jnp.mean(acc, axis=-1, keepdims=True)
        xc = acc - mu
        var = jnp.mean(xc * xc, axis=-1, keepdims=True)
        y = xc * lax.rsqrt(var + EPS) * ln_g_ref[...] + ln_b_ref[...]
        y = y * jax.nn.sigmoid(y)
        mixbuf[base:base + CONV_RB, 0:CONV_WIDTH] = y.astype(jnp.bfloat16)

    lane = lax.broadcasted_iota(jnp.int32, (1, PAIR_W), 1)
    first_head = lane < HEAD_DIM
    col = lax.broadcasted_iota(jnp.int32, (CHUNK, BAND), 1)

    def chunk_body(c, carry):
        r0 = pl.multiple_of(c * CHUNK, CHUNK)
        key_start = i * TM_MIX + c * CHUNK - HALO_KV
        valid = (col + key_start) >= 0
        for p in range(N_PAIRS):
            ls = slice(p * PAIR_W, (p + 1) * PAIR_W)
            qp = qbuf[pl.ds(r0, CHUNK), ls]
            kw = kbuf[pl.ds(r0, BAND), ls]
            vw = vbuf[pl.ds(r0, BAND), ls]
            outs = []
            for hh in range(2):
                sel = first_head if hh == 0 else jnp.logical_not(first_head)
                qm = jnp.where(sel, qp, jnp.zeros_like(qp))
                s = lax.dot_general(qm, kw, (((1,), (1,)), ((), ())),
                                    preferred_element_type=jnp.float32)
                s = s + bias_ref[2 * p + hh]
                s = jnp.where(valid, s, NEG_INF)
                mx = jnp.max(s, axis=-1, keepdims=True)
                e = jnp.exp(s - mx)
                l = jnp.sum(e, axis=-1, keepdims=True)
                o = _dot(e.astype(jnp.bfloat16), vw)
                outs.append(o * (1.0 / l))
            out = jnp.where(first_head, outs[0], outs[1])
            mixbuf[pl.ds(r0, CHUNK), CONV_WIDTH + p * PAIR_W:CONV_WIDTH + (p + 1) * PAIR_W] = (
                out.astype(jnp.bfloat16))
        return carry

    lax.fori_loop(0, TM_MIX // CHUNK, chunk_body, 0)

    mixed = _dot(mixbuf[...], w_out_ref[...])
    o_ref[...] = xt + _rms(mixed, g_post_ref[...])

    hbuf[0:HALO_H, :] = hbuf[TM_MIX:TM_MIX + HALO_H, :]
    kbuf[0:HALO_KV, :] = kbuf[TM_MIX:TM_MIX + HALO_KV, :]
    vbuf[0:HALO_KV, :] = vbuf[TM_MIX:TM_MIX + HALO_KV, :]


def _ffn_kernel(h_ref, g_pre_ref, w_up_ref, dw_w_ref, dw_b_ref, w_down_ref, g_post_ref, o_ref,
                carry, gbuf, vbuf, actbuf):
    i = pl.program_id(1)

    @pl.when(i == 0)
    def _():
        carry[...] = jnp.zeros(carry.shape, jnp.float32)

    xt = h_ref[...]
    u = _rms(xt, g_pre_ref[...]).astype(jnp.bfloat16)

    def up_conv(buf, cols):
        hc = _dot(u, w_up_ref[:, cols])
        buf[0:FFN_HALO, :] = carry[:, cols]
        buf[FFN_HALO:FFN_HALO + TM_FFN, :] = hc
        carry[:, cols] = hc[TM_FFN - FFN_HALO:TM_FFN, :]
        y = dw_b_ref[:, cols] + dw_w_ref[2:3, cols] * hc
        y = y + dw_w_ref[1:2, cols] * buf[FFN_HALO - 1:FFN_HALO - 1 + TM_FFN, :]
        y = y + dw_w_ref[0:1, cols] * buf[FFN_HALO - 2:FFN_HALO - 2 + TM_FFN, :]
        return y

    for c in range(D_FF // FFN_CW):
        gate = up_conv(gbuf, slice(c * FFN_CW, (c + 1) * FFN_CW))
        val = up_conv(vbuf, slice(D_FF + c * FFN_CW, D_FF + (c + 1) * FFN_CW))
        actbuf[:, c * FFN_CW:(c + 1) * FFN_CW] = (jax.nn.gelu(gate) * val).astype(jnp.bfloat16)

    f = _dot(actbuf[...], w_down_ref[...])
    o_ref[...] = xt + _rms(f, g_post_ref[...])


def _rel_bias_band(rel_table):
    i = np.arange(CHUNK)[:, None]
    j = np.arange(BAND)[None, :]
    rel = N_LEFT_CHUNKS * CHUNK + i - j
    idx = np.clip(rel, -MAX_REL, MAX_REL) + MAX_REL
    return rel_table[:, idx]


def _const_spec(shape):
    return pl.BlockSpec(shape, lambda b, i: (0,) * len(shape), pipeline_mode=pl.Buffered(1))


def _mixer(x, g_pre, w_in, dw_w, dw_b, ln_g, ln_b, bias, w_out, g_post):
    B, T, D = x.shape
    row_spec = pl.BlockSpec((None, TM_MIX, D), lambda b, i: (b, i, 0))
    return pl.pallas_call(
        _mixer_kernel,
        grid=(B, T // TM_MIX),
        in_specs=[row_spec,
                  _const_spec((1, D)), _const_spec(w_in.shape), _const_spec(dw_w.shape),
                  _const_spec((1, CONV_WIDTH)), _const_spec((1, CONV_WIDTH)),
                  _const_spec((1, CONV_WIDTH)), _const_spec(bias.shape),
                  _const_spec(w_out.shape), _const_spec((1, D))],
        out_specs=row_spec,
        out_shape=jax.ShapeDtypeStruct(x.shape, x.dtype),
        scratch_shapes=[
            pltpu.VMEM((HALO_H + TM_MIX, CONV_WIDTH), jnp.float32),
            pltpu.VMEM((TM_MIX, ATTN_WIDTH), jnp.bfloat16),
            pltpu.VMEM((HALO_KV + TM_MIX, ATTN_WIDTH), jnp.bfloat16),
            pltpu.VMEM((HALO_KV + TM_MIX, ATTN_WIDTH), jnp.bfloat16),
            pltpu.VMEM((TM_MIX, CONV_WIDTH + ATTN_WIDTH), jnp.bfloat16),
        ],
        compiler_params=pltpu.CompilerParams(
            dimension_semantics=("arbitrary", "arbitrary"), vmem_limit_bytes=VMEM_LIMIT),
        name="mixer",
    )(x, g_pre, w_in, dw_w, dw_b, ln_g, ln_b, bias, w_out, g_post)


def _ffn(h, g_pre, w_up, dw_w, dw_b, w_down, g_post):
    B, T, D = h.shape
    row_spec = pl.BlockSpec((None, TM_FFN, D), lambda b, i: (b, i, 0))
    return pl.pallas_call(
        _ffn_kernel,
        grid=(B, T // TM_FFN),
        in_specs=[row_spec,
                  _const_spec((1, D)), _const_spec(w_up.shape), _const_spec(dw_w.shape),
                  _const_spec((1, 2 * D_FF)), _const_spec(w_down.shape), _const_spec((1, D))],
        out_specs=row_spec,
        out_shape=jax.ShapeDtypeStruct(h.shape, h.dtype),
        scratch_shapes=[
            pltpu.VMEM((FFN_HALO, 2 * D_FF), jnp.float32),
            pltpu.VMEM((FFN_HALO + TM_FFN, FFN_CW), jnp.float32),
            pltpu.VMEM((FFN_HALO + TM_FFN, FFN_CW), jnp.float32),
            pltpu.VMEM((TM_FFN, D_FF), jnp.bfloat16),
        ],
        compiler_params=pltpu.CompilerParams(
            dimension_semantics=("arbitrary", "arbitrary"), vmem_limit_bytes=VMEM_LIMIT),
        name="ffn",
    )(h, g_pre, w_up, dw_w, dw_b, w_down, g_post)


def kernel(x, norm_mix_pre, w_in, conv_dw_w, conv_dw_b, conv_ln_g, conv_ln_b, rel_bias, w_out,
           norm_mix_post, norm_ffn_pre, w_up, ffn_dw_w, ffn_dw_b, w_down, norm_ffn_post):
    bf16 = jnp.bfloat16
    h = x
    for l in range(norm_mix_pre.shape[0]):
        bias = _rel_bias_band(rel_bias[l])
        h = _mixer(h, norm_mix_pre[l][None], w_in[l].astype(bf16), conv_dw_w[l],
                   conv_dw_b[l][None], conv_ln_g[l][None], conv_ln_b[l][None], bias,
                   w_out[l].astype(bf16), norm_mix_post[l][None])
        h = _ffn(h, norm_ffn_pre[l][None], w_up[l].astype(bf16), ffn_dw_w[l],
                 ffn_dw_b[l][None], w_down[l].astype(bf16), norm_ffn_post[l][None])
    return h
```

```python
import jax
import jax.numpy as jnp
from jax import lax
from jax.experimental import pallas as pl
from jax.experimental.pallas import tpu as pltpu

D_MODEL = 1024
CHUNK = 64
N_LEFT_CHUNKS = 8
CONV_WIDTH = 512
ATTN_WIDTH = 512
HEAD_DIM = 64
N_HEADS = ATTN_WIDTH // HEAD_DIM
N_PAIRS = N_HEADS // 2
PAIR_W = 2 * HEAD_DIM
CONV_KERNEL = 31
MAX_REL = 128
D_FF = 2816
FFN_CONV_KERNEL = 3
EPS = 1e-6
NEG_INF = -1e30

TM_MIX = 512
HALO_H = 32
HALO_KV = N_LEFT_CHUNKS * CHUNK
CONV_RB = 32
LN_RB = 128
SUBLANES = 8
HSHIFT_ROWS = HALO_H + TM_MIX - SUBLANES
QUAD = 4
QW = QUAD * CHUNK
KW = HALO_KV + QW
N_QUADS = TM_MIX // QW
PIECE = 128
ROLL_W = 1024
TM_FFN = 512
FFN_CW = 256
FFN_HALO = 8
VMEM_LIMIT = 56 * 1024 * 1024


def _rms(xf, g):
    return xf * lax.rsqrt(jnp.mean(xf * xf, axis=-1, keepdims=True) + EPS) * g


def _dot(a, b):
    return jnp.dot(a, b, preferred_element_type=jnp.float32)


def _dot_nt(a, b):
    return lax.dot_general(a, b, (((1,), (1,)), ((), ())), preferred_element_type=jnp.float32)


def _build_bias(rel_ref, bias_t):
    r_idx = lax.broadcasted_iota(jnp.int32, (KW, QW), 0) // CHUNK
    c_idx = lax.broadcasted_iota(jnp.int32, (KW, QW), 1) // CHUNK
    visible = (r_idx >= c_idx) & (r_idx <= c_idx + N_LEFT_CHUNKS)
    for h in range(N_HEADS):
        row = jnp.broadcast_to(rel_ref[h:h + 1, :], (QW, ROLL_W))
        nat = pltpu.roll(row, 0, 1, stride=1, stride_axis=0)[:, 0:KW]
        bias_t[h] = jnp.where(visible, nat.T, NEG_INF)


def _score_steps(h, quad, lo, qbuf, kbuf, bias_t, s_all, mx_all):
    s_ref, mx_ref = s_all.at[quad], mx_all.at[quad]
    pair = h // 2
    k0 = quad * QW
    lane = lax.broadcasted_iota(jnp.int32, (1, PAIR_W), 1)
    own = (lane // HEAD_DIM) == (h % 2)
    n_steps = (KW - lo) // PIECE
    qq = qbuf[pair, k0:k0 + QW, :]
    qm = jnp.where(own, qq, jnp.zeros_like(qq))

    def step(t):
        r = lo + t * PIECE
        s = _dot_nt(kbuf[pair, k0 + r:k0 + r + PIECE, :], qm) + bias_t[h, r:r + PIECE, :]
        s_ref[r:r + PIECE, :] = s
        m = jnp.max(s.reshape(PIECE // CHUNK, CHUNK, QW), axis=0)
        mx_ref[...] = m if t == 0 else jnp.maximum(mx_ref[...], m)

    return [lambda t=t: step(t) for t in range(n_steps)]


def _softmax_steps(quad, lo, s_all, mx_all, p_all, sum_all):
    s_ref, mx_ref, p_ref, sum_ref = s_all.at[quad], mx_all.at[quad], p_all.at[quad], sum_all.at[quad]
    n_steps = (KW - lo) // PIECE
    mx = jnp.max(mx_ref[...], axis=0, keepdims=True)

    def step(t):
        r = lo + t * PIECE
        e = jnp.exp(s_ref[r:r + PIECE, :] - mx)
        p_ref[r:r + PIECE, :] = e.astype(jnp.bfloat16)
        part = jnp.sum(e.reshape(PIECE // CHUNK, CHUNK, QW), axis=0)
        sum_ref[...] = part if t == 0 else sum_ref[...] + part

    return [lambda t=t: step(t) for t in range(n_steps)]


def _weighted_values(h, quad, lo, vtbuf, p_all, sum_all, out_t):
    row0 = pl.multiple_of(h * HEAD_DIM, HEAD_DIM)
    k0 = quad * QW
    denom = jnp.sum(sum_all[quad], axis=0, keepdims=True)
    o_t = _dot(vtbuf[pl.ds(row0, HEAD_DIM), k0 + lo:k0 + KW], p_all[quad, lo:KW, :])
    out_t[pl.ds(row0, HEAD_DIM), k0:k0 + QW] = o_t * (1.0 / denom)


def _interleave(a_steps, b_steps):
    for t in range(max(len(a_steps), len(b_steps))):
        if t < len(a_steps):
            a_steps[t]()
        if t < len(b_steps):
            b_steps[t]()


def _mixer_kernel(x_ref, g_pre_ref, w_a_ref, w_q_ref, w_k_ref, w_vt_ref, dw_w_ref, dw_b_ref,
                  ln_g_ref, ln_b_ref, rel_ref, w_out_ref, g_post_ref, o_ref,
                  hbuf, hshift, cbuf, qbuf, kbuf, vtbuf, bias_t, s_ref, mx_ref, p_ref, sum_ref,
                  out_t, mixbuf):
    b = pl.program_id(0)
    i = pl.program_id(1)

    @pl.when((b == 0) & (i == 0))
    def _():
        _build_bias(rel_ref, bias_t)

    @pl.when(i == 0)
    def _():
        hbuf[0:HALO_H, :] = jnp.zeros((HALO_H, CONV_WIDTH), jnp.float32)
        kbuf[:, 0:HALO_KV, :] = jnp.zeros((N_PAIRS, HALO_KV, PAIR_W), jnp.bfloat16)
        vtbuf[:, 0:HALO_KV] = jnp.zeros((ATTN_WIDTH, HALO_KV), jnp.bfloat16)

    xt = x_ref[...]
    u = _rms(xt, g_pre_ref[...]).astype(jnp.bfloat16)

    a_val = _dot(u, w_a_ref[:, 0:CONV_WIDTH])
    a_gate = _dot(u, w_a_ref[:, CONV_WIDTH:2 * CONV_WIDTH])
    hbuf[HALO_H:HALO_H + TM_MIX, :] = a_val * jax.nn.sigmoid(a_gate)
    q = (_dot(u, w_q_ref[...]) * (HEAD_DIM ** -0.5)).astype(jnp.bfloat16)
    k = _dot(u, w_k_ref[...]).astype(jnp.bfloat16)
    for p in range(N_PAIRS):
        qbuf[p] = q[:, p * PAIR_W:(p + 1) * PAIR_W]
        kbuf[p, HALO_KV:HALO_KV + TM_MIX, :] = k[:, p * PAIR_W:(p + 1) * PAIR_W]
    vtbuf[:, HALO_KV:HALO_KV + TM_MIX] = _dot_nt(w_vt_ref[...], u).astype(jnp.bfloat16)

    for r in range(1, SUBLANES):
        hshift[r - 1] = hbuf[r:r + HSHIFT_ROWS, :]
    def conv_body(rb, carry):
        base = pl.multiple_of(rb * CONV_RB, CONV_RB)
        acc = jnp.broadcast_to(dw_b_ref[...][None], (CONV_RB // SUBLANES, SUBLANES, CONV_WIDTH))
        for j in range(CONV_KERNEL):
            off = HALO_H - (CONV_KERNEL - 1) + j
            r = off % SUBLANES
            rows = pl.ds(base + (off - r), CONV_RB)
            tap = hbuf[rows, :] if r == 0 else hshift[r - 1, rows, :]
            acc = acc + dw_w_ref[j][None] * tap.reshape(acc.shape)
        cbuf[pl.ds(base, CONV_RB), :] = acc.reshape(CONV_RB, CONV_WIDTH)
        return carry

    lax.fori_loop(0, TM_MIX // CONV_RB, conv_body, 0)
    for blk in range(TM_MIX // LN_RB):
        rows = slice(blk * LN_RB, (blk + 1) * LN_RB)
        c = cbuf[rows, :]
        mu = jnp.mean(c, axis=-1, keepdims=True)
        xc = c - mu
        var = jnp.mean(xc * xc, axis=-1, keepdims=True)
        y = xc * lax.rsqrt(var + EPS) * ln_g_ref[...] + ln_b_ref[...]
        y = y * jax.nn.sigmoid(y)
        mixbuf[rows, 0:CONV_WIDTH] = y.astype(jnp.bfloat16)

    def attn_loop(first_tile):
        lo = [max(0, HALO_KV - quad * QW) if first_tile else 0 for quad in range(N_QUADS)]

        def scores(h, quad):
            return _score_steps(h, quad, lo[quad], qbuf, kbuf, bias_t, s_ref, mx_ref)

        for step in scores(0, 0):
            step()

        def body(h, carry):
            for quad in range(N_QUADS):
                if quad + 1 < N_QUADS:
                    nxt = scores(h, quad + 1)
                else:
                    nxt = scores(jnp.minimum(h + 1, N_HEADS - 1), 0)
                _interleave(nxt, _softmax_steps(quad, lo[quad], s_ref, mx_ref, p_ref, sum_ref))
                _weighted_values(h, quad, lo[quad], vtbuf, p_ref, sum_ref, out_t)
            return carry
        lax.fori_loop(0, N_HEADS, body, 0)

    @pl.when(i == 0)
    def _():
        attn_loop(True)

    @pl.when(i > 0)
    def _():
        attn_loop(False)

    mixbuf[:, CONV_WIDTH:CONV_WIDTH + ATTN_WIDTH] = out_t[...].T.astype(jnp.bfloat16)
    mixed = _dot(mixbuf[...], w_out_ref[...])
    o_ref[...] = xt + _rms(mixed, g_post_ref[...])

    hbuf[0:HALO_H, :] = hbuf[TM_MIX:TM_MIX + HALO_H, :]
    kbuf[:, 0:HALO_KV, :] = kbuf[:, TM_MIX:TM_MIX + HALO_KV, :]
    vtbuf[:, 0:HALO_KV] = vtbuf[:, TM_MIX:TM_MIX + HALO_KV]


def _ffn_kernel(h_ref, g_pre_ref, w_up_ref, dw_w_ref, dw_b_ref, w_down_ref, g_post_ref, o_ref,
                carry, gbuf, vbuf, actbuf):
    i = pl.program_id(1)

    @pl.when(i == 0)
    def _():
        carry[...] = jnp.zeros(carry.shape, jnp.float32)

    xt = h_ref[...]
    u = _rms(xt, g_pre_ref[...]).astype(jnp.bfloat16)

    def up_conv(buf, cols):
        hc = _dot(u, w_up_ref[:, cols])
        buf[0:FFN_HALO, :] = carry[:, cols]
        buf[FFN_HALO:FFN_HALO + TM_FFN, :] = hc
        carry[:, cols] = hc[TM_FFN - FFN_HALO:TM_FFN, :]
        y = dw_b_ref[:, cols] + dw_w_ref[2:3, cols] * hc
        y = y + dw_w_ref[1:2, cols] * buf[FFN_HALO - 1:FFN_HALO - 1 + TM_FFN, :]
        y = y + dw_w_ref[0:1, cols] * buf[FFN_HALO - 2:FFN_HALO - 2 + TM_FFN, :]
        return y

    for c in range(D_FF // FFN_CW):
        gate = up_conv(gbuf, slice(c * FFN_CW, (c + 1) * FFN_CW))
        val = up_conv(vbuf, slice(D_FF + c * FFN_CW, D_FF + (c + 1) * FFN_CW))
        actbuf[:, c * FFN_CW:(c + 1) * FFN_CW] = (jax.nn.gelu(gate) * val).astype(jnp.bfloat16)

    f = _dot(actbuf[...], w_down_ref[...])
    o_ref[...] = xt + _rms(f, g_post_ref[...])


def _rel_distance_row(rel_table):
    h = rel_table.shape[0]
    far = rel_table[:, 2 * MAX_REL:2 * MAX_REL + 1]
    n_far = HALO_KV - MAX_REL + 1
    near = jnp.flip(rel_table, axis=1)[:, 1:]
    n_tail = ROLL_W - n_far - near.shape[1]
    return jnp.concatenate([jnp.broadcast_to(far, (h, n_far)), near,
                            jnp.broadcast_to(far, (h, n_tail))], axis=1)


def _const_spec(shape):
    return pl.BlockSpec(shape, lambda b, i: (0,) * len(shape), pipeline_mode=pl.Buffered(1))


def _mixer(x, g_pre, w_a, w_q, w_k, w_vt, dw_w, dw_b, ln_g, ln_b, rel_row, w_out, g_post):
    B, T, D = x.shape
    row_spec = pl.BlockSpec((None, TM_MIX, D), lambda b, i: (b, i, 0))
    consts = (g_pre, w_a, w_q, w_k, w_vt, dw_w, dw_b, ln_g, ln_b, rel_row, w_out, g_post)
    return pl.pallas_call(
        _mixer_kernel,
        grid=(B, T // TM_MIX),
        in_specs=[row_spec] + [_const_spec(c.shape) for c in consts],
        out_specs=row_spec,
        out_shape=jax.ShapeDtypeStruct(x.shape, x.dtype),
        scratch_shapes=[
            pltpu.VMEM((HALO_H + TM_MIX, CONV_WIDTH), jnp.float32),
            pltpu.VMEM((SUBLANES - 1, HSHIFT_ROWS, CONV_WIDTH), jnp.float32),
            pltpu.VMEM((TM_MIX, CONV_WIDTH), jnp.float32),
            pltpu.VMEM((N_PAIRS, TM_MIX, PAIR_W), jnp.bfloat16),
            pltpu.VMEM((N_PAIRS, HALO_KV + TM_MIX, PAIR_W), jnp.bfloat16),
            pltpu.VMEM((ATTN_WIDTH, HALO_KV + TM_MIX), jnp.bfloat16),
            pltpu.VMEM((N_HEADS, KW, QW), jnp.float32),
            pltpu.VMEM((N_QUADS, KW, QW), jnp.float32),
            pltpu.VMEM((N_QUADS, CHUNK, QW), jnp.float32),
            pltpu.VMEM((N_QUADS, KW, QW), jnp.bfloat16),
            pltpu.VMEM((N_QUADS, CHUNK, QW), jnp.float32),
            pltpu.VMEM((ATTN_WIDTH, TM_MIX), jnp.float32),
            pltpu.VMEM((TM_MIX, CONV_WIDTH + ATTN_WIDTH), jnp.bfloat16),
        ],
        compiler_params=pltpu.CompilerParams(
            dimension_semantics=("arbitrary", "arbitrary"), vmem_limit_bytes=VMEM_LIMIT),
        name="mixer",
    )(x, *consts)


def _ffn(h, g_pre, w_up, dw_w, dw_b, w_down, g_post):
    B, T, D = h.shape
    row_spec = pl.BlockSpec((None, TM_FFN, D), lambda b, i: (b, i, 0))
    return pl.pallas_call(
        _ffn_kernel,
        grid=(B, T // TM_FFN),
        in_specs=[row_spec,
                  _const_spec((1, D)), _const_spec(w_up.shape), _const_spec(dw_w.shape),
                  _const_spec((1, 2 * D_FF)), _const_spec(w_down.shape), _const_spec((1, D))],
        out_specs=row_spec,
        out_shape=jax.ShapeDtypeStruct(h.shape, h.dtype),
        scratch_shapes=[
            pltpu.VMEM((FFN_HALO, 2 * D_FF), jnp.float32),
            pltpu.VMEM((FFN_HALO + TM_FFN, FFN_CW), jnp.float32),
            pltpu.VMEM((FFN_HALO + TM_FFN, FFN_CW), jnp.float32),
            pltpu.VMEM((TM_FFN, D_FF), jnp.bfloat16),
        ],
        compiler_params=pltpu.CompilerParams(
            dimension_semantics=("arbitrary", "arbitrary"), vmem_limit_bytes=VMEM_LIMIT),
        name="ffn",
    )(h, g_pre, w_up, dw_w, dw_b, w_down, g_post)


def kernel(x, norm_mix_pre, w_in, conv_dw_w, conv_dw_b, conv_ln_g, conv_ln_b, rel_bias, w_out,
           norm_mix_post, norm_ffn_pre, w_up, ffn_dw_w, ffn_dw_b, w_down, norm_ffn_post):
    bf16 = jnp.bfloat16
    c0 = 2 * CONV_WIDTH
    h = x
    for l in range(norm_mix_pre.shape[0]):
        w = w_in[l].astype(bf16)
        h = _mixer(h, norm_mix_pre[l][None], w[:, 0:c0], w[:, c0:c0 + ATTN_WIDTH],
                   w[:, c0 + ATTN_WIDTH:c0 + 2 * ATTN_WIDTH], w[:, c0 + 2 * ATTN_WIDTH:].T,
                   jnp.broadcast_to(conv_dw_w[l][:, None, :], (CONV_KERNEL, SUBLANES, CONV_WIDTH)),
                   conv_dw_b[l][None], conv_ln_g[l][None], conv_ln_b[l][None],
                   _rel_distance_row(rel_bias[l]), w_out[l].astype(bf16), norm_mix_post[l][None])
        h = _ffn(h, norm_ffn_pre[l][None], w_up[l].astype(bf16), ffn_dw_w[l],
                 ffn_dw_b[l][None], w_down[l].astype(bf16), norm_ffn_post[l][None])
    return h
```

```python
import jax
import jax.numpy as jnp
from jax import lax
from jax.experimental import pallas as pl
from jax.experimental.pallas import tpu as pltpu

D_MODEL = 1024
CHUNK = 64
N_LEFT_CHUNKS = 8
CONV_WIDTH = 512
ATTN_WIDTH = 512
HEAD_DIM = 64
N_HEADS = ATTN_WIDTH // HEAD_DIM
N_PAIRS = N_HEADS // 2
PAIR_W = 2 * HEAD_DIM
CONV_KERNEL = 31
MAX_REL = 128
D_FF = 2816
FFN_CONV_KERNEL = 3
EPS = 1e-6
NEG_INF = -1e30

TM_MIX = 512
HALO_H = 32
HALO_KV = N_LEFT_CHUNKS * CHUNK
CONV_RB = 32
LN_RB = 128
PROJ_RB = 128
SUBLANES = 8
HSHIFT_ROWS = HALO_H + TM_MIX - SUBLANES
QUAD = 4
QW = QUAD * CHUNK
KW = HALO_KV + QW
N_QUADS = TM_MIX // QW
PIECE = 128
ROLL_W = 1024
TM_FFN = 512
FFN_CW = 256
FFN_HALO = 8
VMEM_LIMIT = 56 * 1024 * 1024


def _rms(xf, g):
    return xf * lax.rsqrt(jnp.mean(xf * xf, axis=-1, keepdims=True) + EPS) * g


def _dot(a, b):
    return jnp.dot(a, b, preferred_element_type=jnp.float32)


def _dot_nt(a, b):
    return lax.dot_general(a, b, (((1,), (1,)), ((), ())), preferred_element_type=jnp.float32)


def _build_bias(rel_ref, bias_t):
    r_idx = lax.broadcasted_iota(jnp.int32, (KW, QW), 0) // CHUNK
    c_idx = lax.broadcasted_iota(jnp.int32, (KW, QW), 1) // CHUNK
    visible = (r_idx >= c_idx) & (r_idx <= c_idx + N_LEFT_CHUNKS)
    for h in range(N_HEADS):
        row = jnp.broadcast_to(rel_ref[h:h + 1, :], (QW, ROLL_W))
        nat = pltpu.roll(row, 0, 1, stride=1, stride_axis=0)[:, 0:KW]
        bias_t[h] = jnp.where(visible, nat.T, NEG_INF)


def _score_steps(h, quad, lo, qbuf, kbuf, bias_t, s_all, mx_all):
    s_ref, mx_ref = s_all.at[quad], mx_all.at[quad]
    pair = h // 2
    k0 = quad * QW
    lane = lax.broadcasted_iota(jnp.int32, (1, PAIR_W), 1)
    own = (lane // HEAD_DIM) == (h % 2)
    n_steps = (KW - lo) // PIECE
    qq = qbuf[pair, k0:k0 + QW, :]
    qm = jnp.where(own, qq, jnp.zeros_like(qq))

    def step(t):
        r = lo + t * PIECE
        s = _dot_nt(kbuf[pair, k0 + r:k0 + r + PIECE, :], qm) + bias_t[h, r:r + PIECE, :]
        s_ref[r:r + PIECE, :] = s
        m = jnp.max(s.reshape(PIECE // CHUNK, CHUNK, QW), axis=0)
        mx_ref[...] = m if t == 0 else jnp.maximum(mx_ref[...], m)

    return [lambda t=t: step(t) for t in range(n_steps)]


def _softmax_steps(quad, lo, s_all, mx_all, p_all, sum_all):
    s_ref, mx_ref, p_ref, sum_ref = s_all.at[quad], mx_all.at[quad], p_all.at[quad], sum_all.at[quad]
    n_steps = (KW - lo) // PIECE
    mx = jnp.max(mx_ref[...], axis=0, keepdims=True)

    def step(t):
        r = lo + t * PIECE
        e = jnp.exp(s_ref[r:r + PIECE, :] - mx)
        p_ref[r:r + PIECE, :] = e.astype(jnp.bfloat16)
        part = jnp.sum(e.reshape(PIECE // CHUNK, CHUNK, QW), axis=0)
        sum_ref[...] = part if t == 0 else sum_ref[...] + part

    return [lambda t=t: step(t) for t in range(n_steps)]


def _weighted_values(h, quad, lo, vtbuf, p_all, sum_all, out_t):
    row0 = pl.multiple_of(h * HEAD_DIM, HEAD_DIM)
    k0 = quad * QW
    denom = jnp.sum(sum_all[quad], axis=0, keepdims=True)
    o_t = _dot(vtbuf[pl.ds(row0, HEAD_DIM), k0 + lo:k0 + KW], p_all[quad, lo:KW, :])
    out_t[pl.ds(row0, HEAD_DIM), k0:k0 + QW] = o_t * (1.0 / denom)


def _interleave(a_steps, b_steps):
    for t in range(max(len(a_steps), len(b_steps))):
        if t < len(a_steps):
            a_steps[t]()
        if t < len(b_steps):
            b_steps[t]()


def _mixer_kernel(x_ref, g_pre_ref, w_in_ref, w_vt_ref, dw_w_ref, dw_b_ref,
                  ln_g_ref, ln_b_ref, rel_ref, w_out_ref, g_post_ref, o_ref,
                  hbuf, hshift, cbuf, qbuf, kbuf, vtbuf, bias_t, s_ref, mx_ref, p_ref, sum_ref,
                  out_t, mixbuf):
    b = pl.program_id(0)
    i = pl.program_id(1)

    @pl.when((b == 0) & (i == 0))
    def _():
        _build_bias(rel_ref, bias_t)

    @pl.when(i == 0)
    def _():
        hbuf[0:HALO_H, :] = jnp.zeros((HALO_H, CONV_WIDTH), jnp.float32)
        kbuf[:, 0:HALO_KV, :] = jnp.zeros((N_PAIRS, HALO_KV, PAIR_W), jnp.bfloat16)
        vtbuf[:, 0:HALO_KV] = jnp.zeros((ATTN_WIDTH, HALO_KV), jnp.bfloat16)

    xt = x_ref[...]
    u = _rms(xt, g_pre_ref[...]).astype(jnp.bfloat16)

    c_q = 2 * CONV_WIDTH
    c_k = c_q + ATTN_WIDTH
    a_val = _dot(u, w_in_ref[:, 0:CONV_WIDTH])
    a_gate = _dot(u, w_in_ref[:, CONV_WIDTH:c_q])
    hbuf[HALO_H:HALO_H + TM_MIX, :] = a_val * jax.nn.sigmoid(a_gate)

    for r in range(1, SUBLANES):
        hshift[r - 1] = hbuf[r:r + HSHIFT_ROWS, :]

    def conv_block(base):
        acc = jnp.broadcast_to(dw_b_ref[...][None], (CONV_RB // SUBLANES, SUBLANES, CONV_WIDTH))
        for j in range(CONV_KERNEL):
            off = HALO_H - (CONV_KERNEL - 1) + j
            r = off % SUBLANES
            rows = slice(base + off - r, base + off - r + CONV_RB)
            tap = hbuf[rows, :] if r == 0 else hshift[r - 1, rows, :]
            acc = acc + dw_w_ref[j][None] * tap.reshape(acc.shape)
        cbuf[base:base + CONV_RB, :] = acc.reshape(CONV_RB, CONV_WIDTH)

    for blk in range(TM_MIX // PROJ_RB):
        rows = slice(blk * PROJ_RB, (blk + 1) * PROJ_RB)
        krows = slice(HALO_KV + blk * PROJ_RB, HALO_KV + (blk + 1) * PROJ_RB)
        ub = u[rows, :]
        q = (_dot(ub, w_in_ref[:, c_q:c_k]) * (HEAD_DIM ** -0.5)).astype(jnp.bfloat16)
        k = _dot(ub, w_in_ref[:, c_k:c_k + ATTN_WIDTH]).astype(jnp.bfloat16)
        for p in range(N_PAIRS):
            qbuf[p, rows, :] = q[:, p * PAIR_W:(p + 1) * PAIR_W]
            kbuf[p, krows, :] = k[:, p * PAIR_W:(p + 1) * PAIR_W]
        if blk % 2 == 0:
            wide = slice(blk * PROJ_RB, (blk + 2) * PROJ_RB)
            vtbuf[:, HALO_KV + wide.start:HALO_KV + wide.stop] = (
                _dot_nt(w_vt_ref[...], u[wide, :]).astype(jnp.bfloat16))
        for rb in range(PROJ_RB // CONV_RB):
            conv_block(blk * PROJ_RB + rb * CONV_RB)

    for blk in range(TM_MIX // LN_RB):
        rows = slice(blk * LN_RB, (blk + 1) * LN_RB)
        c = cbuf[rows, :]
        mu = jnp.mean(c, axis=-1, keepdims=True)
        xc = c - mu
        var = jnp.mean(xc * xc, axis=-1, keepdims=True)
        y = xc * lax.rsqrt(var + EPS) * ln_g_ref[...] + ln_b_ref[...]
        y = y * jax.nn.sigmoid(y)
        mixbuf[rows, 0:CONV_WIDTH] = y.astype(jnp.bfloat16)

    def attn_loop(first_tile):
        lo = [max(0, HALO_KV - quad * QW) if first_tile else 0 for quad in range(N_QUADS)]

        def scores(h, quad):
            return _score_steps(h, quad, lo[quad], qbuf, kbuf, bias_t, s_ref, mx_ref)

        for step in scores(0, 0):
            step()

        def body(h, carry):
            for quad in range(N_QUADS):
                if quad + 1 < N_QUADS:
                    nxt = scores(h, quad + 1)
                else:
                    nxt = scores(jnp.minimum(h + 1, N_HEADS - 1), 0)
                _interleave(nxt, _softmax_steps(quad, lo[quad], s_ref, mx_ref, p_ref, sum_ref))
                _weighted_values(h, quad, lo[quad], vtbuf, p_ref, sum_ref, out_t)
            return carry
        lax.fori_loop(0, N_HEADS, body, 0)

    @pl.when(i == 0)
    def _():
        attn_loop(True)

    @pl.when(i > 0)
    def _():
        attn_loop(False)

    mixbuf[:, CONV_WIDTH:CONV_WIDTH + ATTN_WIDTH] = out_t[...].T.astype(jnp.bfloat16)
    mixed = _dot(mixbuf[...], w_out_ref[...])
    o_ref[...] = xt + _rms(mixed, g_post_ref[...])

    hbuf[0:HALO_H, :] = hbuf[TM_MIX:TM_MIX + HALO_H, :]
    kbuf[:, 0:HALO_KV, :] = kbuf[:, TM_MIX:TM_MIX + HALO_KV, :]
    vtbuf[:, 0:HALO_KV] = vtbuf[:, TM_MIX:TM_MIX + HALO_KV]


def _ffn_kernel(h_ref, g_pre_ref, w_up_ref, dw_w_ref, dw_b_ref, w_down_ref, g_post_ref, o_ref,
                carry, gbuf, vbuf, actbuf):
    i = pl.program_id(1)

    @pl.when(i == 0)
    def _():
        carry[...] = jnp.zeros(carry.shape, jnp.float32)

    xt = h_ref[...]
    u = _rms(xt, g_pre_ref[...]).astype(jnp.bfloat16)

    def up_conv(buf, cols):
        hc = _dot(u, w_up_ref[:, cols])
        buf[0:FFN_HALO, :] = carry[:, cols]
        buf[FFN_HALO:FFN_HALO + TM_FFN, :] = hc
        carry[:, cols] = hc[TM_FFN - FFN_HALO:TM_FFN, :]
        y = dw_b_ref[:, cols] + dw_w_ref[2:3, cols] * hc
        y = y + dw_w_ref[1:2, cols] * buf[FFN_HALO - 1:FFN_HALO - 1 + TM_FFN, :]
        y = y + dw_w_ref[0:1, cols] * buf[FFN_HALO - 2:FFN_HALO - 2 + TM_FFN, :]
        return y

    for c in range(D_FF // FFN_CW):
        gate = up_conv(gbuf.at[c % 2], slice(c * FFN_CW, (c + 1) * FFN_CW))
        val = up_conv(vbuf.at[c % 2], slice(D_FF + c * FFN_CW, D_FF + (c + 1) * FFN_CW))
        actbuf[:, c * FFN_CW:(c + 1) * FFN_CW] = (jax.nn.gelu(gate) * val).astype(jnp.bfloat16)
    f = _dot(actbuf[...], w_down_ref[...])
    o_ref[...] = xt + _rms(f, g_post_ref[...])


def _rel_distance_row(rel_table):
    h = rel_table.shape[0]
    far = rel_table[:, 2 * MAX_REL:2 * MAX_REL + 1]
    n_far = HALO_KV - MAX_REL + 1
    near = jnp.flip(rel_table, axis=1)[:, 1:]
    n_tail = ROLL_W - n_far - near.shape[1]
    return jnp.concatenate([jnp.broadcast_to(far, (h, n_far)), near,
                            jnp.broadcast_to(far, (h, n_tail))], axis=1)


def _const_spec(shape):
    return pl.BlockSpec(shape, lambda b, i: (0,) * len(shape), pipeline_mode=pl.Buffered(1))


def _mixer(x, g_pre, w_in, w_vt, dw_w, dw_b, ln_g, ln_b, rel_row, w_out, g_post):
    B, T, D = x.shape
    row_spec = pl.BlockSpec((None, TM_MIX, D), lambda b, i: (b, i, 0))
    consts = (g_pre, w_in, w_vt, dw_w, dw_b, ln_g, ln_b, rel_row, w_out, g_post)
    return pl.pallas_call(
        _mixer_kernel,
        grid=(B, T // TM_MIX),
        in_specs=[row_spec] + [_const_spec(c.shape) for c in consts],
        out_specs=row_spec,
        out_shape=jax.ShapeDtypeStruct(x.shape, x.dtype),
        scratch_shapes=[
            pltpu.VMEM((HALO_H + TM_MIX, CONV_WIDTH), jnp.float32),
            pltpu.VMEM((SUBLANES - 1, HSHIFT_ROWS, CONV_WIDTH), jnp.float32),
            pltpu.VMEM((TM_MIX, CONV_WIDTH), jnp.float32),
            pltpu.VMEM((N_PAIRS, TM_MIX, PAIR_W), jnp.bfloat16),
            pltpu.VMEM((N_PAIRS, HALO_KV + TM_MIX, PAIR_W), jnp.bfloat16),
            pltpu.VMEM((ATTN_WIDTH, HALO_KV + TM_MIX), jnp.bfloat16),
            pltpu.VMEM((N_HEADS, KW, QW), jnp.float32),
            pltpu.VMEM((N_QUADS, KW, QW), jnp.float32),
            pltpu.VMEM((N_QUADS, CHUNK, QW), jnp.float32),
            pltpu.VMEM((N_QUADS, KW, QW), jnp.bfloat16),
            pltpu.VMEM((N_QUADS, CHUNK, QW), jnp.float32),
            pltpu.VMEM((ATTN_WIDTH, TM_MIX), jnp.float32),
            pltpu.VMEM((TM_MIX, CONV_WIDTH + ATTN_WIDTH), jnp.bfloat16),
        ],
        compiler_params=pltpu.CompilerParams(
            dimension_semantics=("arbitrary", "arbitrary"), vmem_limit_bytes=VMEM_LIMIT),
        name="mixer",
    )(x, *consts)


def _ffn(h, g_pre, w_up, dw_w, dw_b, w_down, g_post):
    B, T, D = h.shape
    row_spec = pl.BlockSpec((None, TM_FFN, D), lambda b, i: (b, i, 0))
    return pl.pallas_call(
        _ffn_kernel,
        grid=(B, T // TM_FFN),
        in_specs=[row_spec,
                  _const_spec((1, D)), _const_spec(w_up.shape), _const_spec(dw_w.shape),
                  _const_spec((1, 2 * D_FF)), _const_spec(w_down.shape), _const_spec((1, D))],
        out_specs=row_spec,
        out_shape=jax.ShapeDtypeStruct(h.shape, h.dtype),
        scratch_shapes=[
            pltpu.VMEM((FFN_HALO, 2 * D_FF), jnp.float32),
            pltpu.VMEM((2, FFN_HALO + TM_FFN, FFN_CW), jnp.float32),
            pltpu.VMEM((2, FFN_HALO + TM_FFN, FFN_CW), jnp.float32),
            pltpu.VMEM((TM_FFN, D_FF), jnp.bfloat16),
        ],
        compiler_params=pltpu.CompilerParams(
            dimension_semantics=("arbitrary", "arbitrary"), vmem_limit_bytes=VMEM_LIMIT),
        name="ffn",
    )(h, g_pre, w_up, dw_w, dw_b, w_down, g_post)


def kernel(x, norm_mix_pre, w_in, conv_dw_w, conv_dw_b, conv_ln_g, conv_ln_b, rel_bias, w_out,
           norm_mix_post, norm_ffn_pre, w_up, ffn_dw_w, ffn_dw_b, w_down, norm_ffn_post):
    bf16 = jnp.bfloat16
    c0 = 2 * CONV_WIDTH
    h = x
    for l in range(norm_mix_pre.shape[0]):
        w = w_in[l].astype(bf16)
        h = _mixer(h, norm_mix_pre[l][None], w, w[:, c0 + 2 * ATTN_WIDTH:].T,
                   jnp.broadcast_to(conv_dw_w[l][:, None, :], (CONV_KERNEL, SUBLANES, CONV_WIDTH)),
                   conv_dw_b[l][None], conv_ln_g[l][None], conv_ln_b[l][None],
                   _rel_distance_row(rel_bias[l]), w_out[l].astype(bf16), norm_mix_post[l][None])
        h = _ffn(h, norm_ffn_pre[l][None], w_up[l].astype(bf16), ffn_dw_w[l],
                 ffn_dw_b[l][None], w_down[l].astype(bf16), norm_ffn_post[l][None])
    return h
```

```python
import jax
import jax.numpy as jnp
from jax import lax
from jax.experimental import pallas as pl
from jax.experimental.pallas import tpu as pltpu

D_MODEL = 1024
CHUNK = 64
N_LEFT_CHUNKS = 8
CONV_WIDTH = 512
ATTN_WIDTH = 512
HEAD_DIM = 64
N_HEADS = ATTN_WIDTH // HEAD_DIM
N_PAIRS = N_HEADS // 2
PAIR_W = 2 * HEAD_DIM
CONV_KERNEL = 31
MAX_REL = 128
D_FF = 2816
FFN_CONV_KERNEL = 3
EPS = 1e-6
NEG_INF = -1e30

TM_MIX = 512
HALO_H = 32
HALO_KV = N_LEFT_CHUNKS * CHUNK
CONV_RB = 32
LN_RB = 128
PROJ_RB = 128
SUBLANES = 8
HSHIFT_ROWS = HALO_H + TM_MIX - SUBLANES
QUAD = 4
QW = QUAD * CHUNK
KW = HALO_KV + QW
N_QUADS = TM_MIX // QW
PIECE = 128
SCORE_ROWS = 384
SOFTMAX_ROWS = 256
HALF_W = 128
FAR_LAG = (HALO_KV - MAX_REL) // CHUNK - 1
LOG2E = 1.4426950408889634
ROLL_W = 1024
TM_FFN = 512
FFN_CW = 256
FFN_HALO = 8
VMEM_LIMIT = 56 * 1024 * 1024


def _rms(xf, g):
    return xf * lax.rsqrt(jnp.mean(xf * xf, axis=-1, keepdims=True) + EPS) * g


def _dot(a, b):
    return jnp.dot(a, b, preferred_element_type=jnp.float32)


def _dot_nt(a, b):
    return lax.dot_general(a, b, (((1,), (1,)), ((), ())), preferred_element_type=jnp.float32)


def _build_bias(rel_ref, bias_t):
    r_idx = lax.broadcasted_iota(jnp.int32, (KW, QW), 0) // CHUNK
    c_idx = lax.broadcasted_iota(jnp.int32, (KW, QW), 1) // CHUNK
    visible = (r_idx >= c_idx) & (r_idx <= c_idx + N_LEFT_CHUNKS)
    for h in range(N_HEADS):
        row = jnp.broadcast_to(rel_ref[h:h + 1, :], (QW, ROLL_W))
        nat = pltpu.roll(row, 0, 1, stride=1, stride_axis=0)[:, 0:KW]
        rel_to_far = (nat.T - rel_ref[h:h + 1, 0:1]) * LOG2E
        bias_t[h] = jnp.where(visible, rel_to_far, NEG_INF)


def _block_kind(piece, half):
    lags = [kc - qc
            for kc in range(piece * PIECE // CHUNK, (piece + 1) * PIECE // CHUNK)
            for qc in range(half * HALF_W // CHUNK, (half + 1) * HALF_W // CHUNK)]
    live = any(0 <= lag <= N_LEFT_CHUNKS for lag in lags)
    plain = all(0 <= lag <= FAR_LAG for lag in lags)
    return live, plain


def _first_live(piece, half, lo):
    return not any(_block_kind(t, half)[0] for t in range(lo // PIECE, piece))


def _unit_blocks(piece):
    for head in range(2):
        for half in range(QW // HALF_W):
            live, plain = _block_kind(piece, half)
            if live:
                start = head * QW + half * HALF_W
                yield head, half, plain, slice(start, start + HALF_W)


def _score_steps(pair, quad, lo, qbuf, kbuf, bias_t, s_all, mx_all):
    s_ref, mx_ref = s_all.at[quad], mx_all.at[quad]
    k0 = quad * QW
    lane = lax.broadcasted_iota(jnp.int32, (1, PAIR_W), 1)
    qq = qbuf[pair, k0:k0 + QW, :]
    zero = jnp.zeros_like(qq)
    qm = jnp.concatenate([jnp.where(lane < HEAD_DIM, qq, zero),
                          jnp.where(lane >= HEAD_DIM, qq, zero)], axis=0)

    def step(r0, r1):
        s = _dot_nt(kbuf[pair, k0 + r0:k0 + r1, :], qm)
        running = {}
        for piece in range(r0 // PIECE, r1 // PIECE):
            r = piece * PIECE
            for head, half, plain, lanes in _unit_blocks(piece):
                sh = s[r - r0:r - r0 + PIECE, lanes]
                if not plain:
                    sh = sh + bias_t[2 * pair + head, r:r + PIECE,
                                     half * HALF_W:(half + 1) * HALF_W]
                s_ref[r:r + PIECE, lanes] = sh
                m = jnp.max(sh.reshape(PIECE // SUBLANES, SUBLANES, HALF_W), axis=0)
                key = (lanes.start, half)
                if key in running:
                    running[key] = jnp.maximum(running[key], m)
                elif _first_live(piece, half, lo):
                    running[key] = m
                else:
                    running[key] = jnp.maximum(mx_ref[:, lanes], m)
        for (start, _), m in running.items():
            mx_ref[:, start:start + HALF_W] = m

    starts = range(lo, KW, SCORE_ROWS)
    return [lambda r0=r0: step(r0, min(r0 + SCORE_ROWS, KW)) for r0 in starts]


def _softmax_steps(quad, lo, s_all, mx_all, p_all, sum_all):
    s_ref, mx_ref, p_ref, sum_ref = s_all.at[quad], mx_all.at[quad], p_all.at[quad], sum_all.at[quad]
    mx = jnp.max(mx_ref[...], axis=0, keepdims=True)

    def step(r0, r1):
        running = {}
        for piece in range(r0 // PIECE, r1 // PIECE):
            r = piece * PIECE
            for _, half, _, lanes in _unit_blocks(piece):
                e = jnp.exp2(s_ref[r:r + PIECE, lanes] - mx[:, lanes])
                p_ref[r:r + PIECE, lanes] = e.astype(jnp.bfloat16)
                part = jnp.sum(e.reshape(PIECE // SUBLANES, SUBLANES, HALF_W), axis=0)
                key = (lanes.start, half)
                if key in running:
                    running[key] = running[key] + part
                elif _first_live(piece, half, lo):
                    running[key] = part
                else:
                    running[key] = sum_ref[:, lanes] + part
        for (start, _), part in running.items():
            sum_ref[:, start:start + HALF_W] = part

    starts = range(lo, KW, SOFTMAX_ROWS)
    return [lambda r0=r0: step(r0, min(r0 + SOFTMAX_ROWS, KW)) for r0 in starts]


def _zero_dead_blocks(p_all):
    p_all[...] = jnp.zeros(p_all.shape, jnp.bfloat16)


def _weighted_values(pair, quad, lo, vtbuf, p_all, sum_all, out_t):
    row0 = pl.multiple_of(pair * PAIR_W, PAIR_W)
    k0 = quad * QW
    denom = jnp.sum(sum_all[quad], axis=0, keepdims=True)
    o_t = _dot(vtbuf[pl.ds(row0, PAIR_W), k0 + lo:k0 + KW], p_all[quad, lo:KW, :])
    inv = 1.0 / denom
    for head in range(2):
        rows = slice(head * HEAD_DIM, (head + 1) * HEAD_DIM)
        lanes = slice(head * QW, (head + 1) * QW)
        out_t[pl.ds(row0 + head * HEAD_DIM, HEAD_DIM), k0:k0 + QW] = o_t[rows, lanes] * inv[:, lanes]


def _interleave(a_steps, b_steps):
    done = 0
    for t, a_step in enumerate(a_steps):
        a_step()
        upto = (t + 1) * len(b_steps) // len(a_steps)
        for b_step in b_steps[done:upto]:
            b_step()
        done = upto


def _mixer_kernel(x_ref, g_pre_ref, w_in_ref, w_vt_ref, dw_w_ref, dw_b_ref,
                  ln_g_ref, ln_b_ref, rel_ref, w_out_ref, g_post_ref, o_ref,
                  hbuf, hshift, cbuf, qbuf, kbuf, vtbuf, bias_t, s_ref, mx_ref, p_ref, sum_ref,
                  out_t, mixbuf):
    b = pl.program_id(0)
    i = pl.program_id(1)

    @pl.when((b == 0) & (i == 0))
    def _():
        _build_bias(rel_ref, bias_t)
        _zero_dead_blocks(p_ref)

    @pl.when(i == 0)
    def _():
        hbuf[0:HALO_H, :] = jnp.zeros((HALO_H, CONV_WIDTH), jnp.float32)
        kbuf[:, 0:HALO_KV, :] = jnp.zeros((N_PAIRS, HALO_KV, PAIR_W), jnp.bfloat16)
        vtbuf[:, 0:HALO_KV] = jnp.zeros((ATTN_WIDTH, HALO_KV), jnp.bfloat16)

    xt = x_ref[...]
    u = _rms(xt, g_pre_ref[...]).astype(jnp.bfloat16)

    c_q = 2 * CONV_WIDTH
    c_k = c_q + ATTN_WIDTH
    a_val = _dot(u, w_in_ref[:, 0:CONV_WIDTH])
    a_gate = _dot(u, w_in_ref[:, CONV_WIDTH:c_q])
    hbuf[HALO_H:HALO_H + TM_MIX, :] = a_val * jax.nn.sigmoid(a_gate)

    for r in range(1, SUBLANES):
        hshift[r - 1] = hbuf[r:r + HSHIFT_ROWS, :]

    def conv_block(base):
        acc = jnp.broadcast_to(dw_b_ref[...][None], (CONV_RB // SUBLANES, SUBLANES, CONV_WIDTH))
        for j in range(CONV_KERNEL):
            off = HALO_H - (CONV_KERNEL - 1) + j
            r = off % SUBLANES
            rows = slice(base + off - r, base + off - r + CONV_RB)
            tap = hbuf[rows, :] if r == 0 else hshift[r - 1, rows, :]
            acc = acc + dw_w_ref[j][None] * tap.reshape(acc.shape)
        cbuf[base:base + CONV_RB, :] = acc.reshape(CONV_RB, CONV_WIDTH)

    for blk in range(TM_MIX // PROJ_RB):
        rows = slice(blk * PROJ_RB, (blk + 1) * PROJ_RB)
        krows = slice(HALO_KV + blk * PROJ_RB, HALO_KV + (blk + 1) * PROJ_RB)
        ub = u[rows, :]
        q = (_dot(ub, w_in_ref[:, c_q:c_k]) * (HEAD_DIM ** -0.5 * LOG2E)).astype(jnp.bfloat16)
        k = _dot(ub, w_in_ref[:, c_k:c_k + ATTN_WIDTH]).astype(jnp.bfloat16)
        for p in range(N_PAIRS):
            qbuf[p, rows, :] = q[:, p * PAIR_W:(p + 1) * PAIR_W]
            kbuf[p, krows, :] = k[:, p * PAIR_W:(p + 1) * PAIR_W]
        if blk % 2 == 0:
            wide = slice(blk * PROJ_RB, (blk + 2) * PROJ_RB)
            vtbuf[:, HALO_KV + wide.start:HALO_KV + wide.stop] = (
                _dot_nt(w_vt_ref[...], u[wide, :]).astype(jnp.bfloat16))
        for rb in range(PROJ_RB // CONV_RB):
            conv_block(blk * PROJ_RB + rb * CONV_RB)

    for blk in range(TM_MIX // LN_RB):
        rows = slice(blk * LN_RB, (blk + 1) * LN_RB)
        c = cbuf[rows, :]
        mu = jnp.mean(c, axis=-1, keepdims=True)
        xc = c - mu
        var = jnp.mean(xc * xc, axis=-1, keepdims=True)
        y = xc * lax.rsqrt(var + EPS) * ln_g_ref[...] + ln_b_ref[...]
        y = y * jax.nn.sigmoid(y)
        mixbuf[rows, 0:CONV_WIDTH] = y.astype(jnp.bfloat16)

    def attn_loop(first_tile):
        lo = [max(0, HALO_KV - quad * QW) if first_tile else 0 for quad in range(N_QUADS)]

        def scores(pair, quad):
            return _score_steps(pair, quad, lo[quad], qbuf, kbuf, bias_t, s_ref, mx_ref)

        for step in scores(0, 0):
            step()

        def body(pair, carry):
            for quad in range(N_QUADS):
                if quad + 1 < N_QUADS:
                    nxt = scores(pair, quad + 1)
                else:
                    nxt = scores(jnp.minimum(pair + 1, N_PAIRS - 1), 0)
                _interleave(nxt, _softmax_steps(quad, lo[quad], s_ref, mx_ref, p_ref, sum_ref))
                _weighted_values(pair, quad, lo[quad], vtbuf, p_ref, sum_ref, out_t)
            return carry
        lax.fori_loop(0, N_PAIRS, body, 0)

    @pl.when(i == 0)
    def _():
        attn_loop(True)

    @pl.when(i > 0)
    def _():
        attn_loop(False)

    mixbuf[:, CONV_WIDTH:CONV_WIDTH + ATTN_WIDTH] = out_t[...].T.astype(jnp.bfloat16)
    mixed = _dot(mixbuf[...], w_out_ref[...])
    o_ref[...] = xt + _rms(mixed, g_post_ref[...])

    hbuf[0:HALO_H, :] = hbuf[TM_MIX:TM_MIX + HALO_H, :]
    kbuf[:, 0:HALO_KV, :] = kbuf[:, TM_MIX:TM_MIX + HALO_KV, :]
    vtbuf[:, 0:HALO_KV] = vtbuf[:, TM_MIX:TM_MIX + HALO_KV]


def _ffn_kernel(h_ref, g_pre_ref, w_up_ref, dw_w_ref, dw_b_ref, w_down_ref, g_post_ref, o_ref,
                carry, gbuf, vbuf, actbuf):
    i = pl.program_id(1)

    @pl.when(i == 0)
    def _():
        carry[...] = jnp.zeros(carry.shape, jnp.float32)

    xt = h_ref[...]
    u = _rms(xt, g_pre_ref[...]).astype(jnp.bfloat16)

    def up_conv(buf, cols):
        hc = _dot(u, w_up_ref[:, cols])
        buf[0:FFN_HALO, :] = carry[:, cols]
        buf[FFN_HALO:FFN_HALO + TM_FFN, :] = hc
        carry[:, cols] = hc[TM_FFN - FFN_HALO:TM_FFN, :]
        y = dw_b_ref[:, cols] + dw_w_ref[2:3, cols] * hc
        y = y + dw_w_ref[1:2, cols] * buf[FFN_HALO - 1:FFN_HALO - 1 + TM_FFN, :]
        y = y + dw_w_ref[0:1, cols] * buf[FFN_HALO - 2:FFN_HALO - 2 + TM_FFN, :]
        return y

    for c in range(D_FF // FFN_CW):
        gate = up_conv(gbuf.at[c % 2], slice(c * FFN_CW, (c + 1) * FFN_CW))
        val = up_conv(vbuf.at[c % 2], slice(D_FF + c * FFN_CW, D_FF + (c + 1) * FFN_CW))
        actbuf[:, c * FFN_CW:(c + 1) * FFN_CW] = (jax.nn.gelu(gate) * val).astype(jnp.bfloat16)
    f = _dot(actbuf[...], w_down_ref[...])
    o_ref[...] = xt + _rms(f, g_post_ref[...])


def _rel_distance_row(rel_table):
    h = rel_table.shape[0]
    far = rel_table[:, 2 * MAX_REL:2 * MAX_REL + 1]
    n_far = HALO_KV - MAX_REL + 1
    near = jnp.flip(rel_table, axis=1)[:, 1:]
    n_tail = ROLL_W - n_far - near.shape[1]
    return jnp.concatenate([jnp.broadcast_to(far, (h, n_far)), near,
                            jnp.broadcast_to(far, (h, n_tail))], axis=1)


def _const_spec(shape):
    return pl.BlockSpec(shape, lambda b, i: (0,) * len(shape), pipeline_mode=pl.Buffered(1))


def _mixer(x, g_pre, w_in, w_vt, dw_w, dw_b, ln_g, ln_b, rel_row, w_out, g_post):
    B, T, D = x.shape
    row_spec = pl.BlockSpec((None, TM_MIX, D), lambda b, i: (b, i, 0))
    consts = (g_pre, w_in, w_vt, dw_w, dw_b, ln_g, ln_b, rel_row, w_out, g_post)
    return pl.pallas_call(
        _mixer_kernel,
        grid=(B, T // TM_MIX),
        in_specs=[row_spec] + [_const_spec(c.shape) for c in consts],
        out_specs=row_spec,
        out_shape=jax.ShapeDtypeStruct(x.shape, x.dtype),
        scratch_shapes=[
            pltpu.VMEM((HALO_H + TM_MIX, CONV_WIDTH), jnp.float32),
            pltpu.VMEM((SUBLANES - 1, HSHIFT_ROWS, CONV_WIDTH), jnp.float32),
            pltpu.VMEM((TM_MIX, CONV_WIDTH), jnp.float32),
            pltpu.VMEM((N_PAIRS, TM_MIX, PAIR_W), jnp.bfloat16),
            pltpu.VMEM((N_PAIRS, HALO_KV + TM_MIX, PAIR_W), jnp.bfloat16),
            pltpu.VMEM((ATTN_WIDTH, HALO_KV + TM_MIX), jnp.bfloat16),
            pltpu.VMEM((N_HEADS, KW, QW), jnp.float32),
            pltpu.VMEM((N_QUADS, KW, 2 * QW), jnp.float32),
            pltpu.VMEM((N_QUADS, SUBLANES, 2 * QW), jnp.float32),
            pltpu.VMEM((N_QUADS, KW, 2 * QW), jnp.bfloat16),
            pltpu.VMEM((N_QUADS, SUBLANES, 2 * QW), jnp.float32),
            pltpu.VMEM((ATTN_WIDTH, TM_MIX), jnp.float32),
            pltpu.VMEM((TM_MIX, CONV_WIDTH + ATTN_WIDTH), jnp.bfloat16),
        ],
        compiler_params=pltpu.CompilerParams(
            dimension_semantics=("arbitrary", "arbitrary"), vmem_limit_bytes=VMEM_LIMIT),
        name="mixer",
    )(x, *consts)


def _ffn(h, g_pre, w_up, dw_w, dw_b, w_down, g_post):
    B, T, D = h.shape
    row_spec = pl.BlockSpec((None, TM_FFN, D), lambda b, i: (b, i, 0))
    return pl.pallas_call(
        _ffn_kernel,
        grid=(B, T // TM_FFN),
        in_specs=[row_spec,
                  _const_spec((1, D)), _const_spec(w_up.shape), _const_spec(dw_w.shape),
                  _const_spec((1, 2 * D_FF)), _const_spec(w_down.shape), _const_spec((1, D))],
        out_specs=row_spec,
        out_shape=jax.ShapeDtypeStruct(h.shape, h.dtype),
        scratch_shapes=[
            pltpu.VMEM((FFN_HALO, 2 * D_FF), jnp.float32),
            pltpu.VMEM((2, FFN_HALO + TM_FFN, FFN_CW), jnp.float32),
            pltpu.VMEM((2, FFN_HALO + TM_FFN, FFN_CW), jnp.float32),
            pltpu.VMEM((TM_FFN, D_FF), jnp.bfloat16),
        ],
        compiler_params=pltpu.CompilerParams(
            dimension_semantics=("arbitrary", "arbitrary"), vmem_limit_bytes=VMEM_LIMIT),
        name="ffn",
    )(h, g_pre, w_up, dw_w, dw_b, w_down, g_post)


def kernel(x, norm_mix_pre, w_in, conv_dw_w, conv_dw_b, conv_ln_g, conv_ln_b, rel_bias, w_out,
           norm_mix_post, norm_ffn_pre, w_up, ffn_dw_w, ffn_dw_b, w_down, norm_ffn_post):
    bf16 = jnp.bfloat16
    c0 = 2 * CONV_WIDTH
    h = x
    for l in range(norm_mix_pre.shape[0]):
        w = w_in[l].astype(bf16)
        h = _mixer(h, norm_mix_pre[l][None], w, w[:, c0 + 2 * ATTN_WIDTH:].T,
                   jnp.broadcast_to(conv_dw_w[l][:, None, :], (CONV_KERNEL, SUBLANES, CONV_WIDTH)),
                   conv_dw_b[l][None], conv_ln_g[l][None], conv_ln_b[l][None],
                   _rel_distance_row(rel_bias[l]), w_out[l].astype(bf16), norm_mix_post[l][None])
        h = _ffn(h, norm_ffn_pre[l][None], w_up[l].astype(bf16), ffn_dw_w[l],
                 ffn_dw_b[l][None], w_down[l].astype(bf16), norm_ffn_post[l][None])
    return h
```

```python
import jax
import jax.numpy as jnp
from jax import lax
from jax.experimental import pallas as pl
from jax.experimental.pallas import tpu as pltpu

D_MODEL = 1024
CHUNK = 64
N_LEFT_CHUNKS = 8
CONV_WIDTH = 512
ATTN_WIDTH = 512
HEAD_DIM = 64
N_HEADS = ATTN_WIDTH // HEAD_DIM
N_PAIRS = N_HEADS // 2
PAIR_W = 2 * HEAD_DIM
CONV_KERNEL = 31
MAX_REL = 128
D_FF = 2816
FFN_CONV_KERNEL = 3
EPS = 1e-6
NEG_INF = -1e30

TM_MIX = 512
HALO_H = 32
HALO_KV = N_LEFT_CHUNKS * CHUNK
CONV_RB = 32
LN_RB = 128
PROJ_RB = 128
SUBLANES = 8
HSHIFT_ROWS = HALO_H + TM_MIX - SUBLANES
QUAD = 4
QW = QUAD * CHUNK
KW = HALO_KV + QW
N_QUADS = TM_MIX // QW
PIECE = 128
SCORE_ROWS = 384
SOFTMAX_ROWS = 256
HALF_W = 128
FAR_LAG = (HALO_KV - MAX_REL) // CHUNK - 1
LOG2E = 1.4426950408889634
ROLL_W = 1024
TM_FFN = 512
FFN_CW = 256
FFN_SEG = TM_FFN // SUBLANES
FFN_HALO = (FFN_CONV_KERNEL - 1) * SUBLANES
VMEM_LIMIT = 56 * 1024 * 1024


def _rms(xf, g):
    return xf * lax.rsqrt(jnp.mean(xf * xf, axis=-1, keepdims=True) + EPS) * g


def _dot(a, b):
    return jnp.dot(a, b, preferred_element_type=jnp.float32)


def _dot_nt(a, b):
    return lax.dot_general(a, b, (((1,), (1,)), ((), ())), preferred_element_type=jnp.float32)


def _build_bias(rel_ref, bias_t):
    r_idx = lax.broadcasted_iota(jnp.int32, (KW, QW), 0) // CHUNK
    c_idx = lax.broadcasted_iota(jnp.int32, (KW, QW), 1) // CHUNK
    visible = (r_idx >= c_idx) & (r_idx <= c_idx + N_LEFT_CHUNKS)
    for h in range(N_HEADS):
        row = jnp.broadcast_to(rel_ref[h:h + 1, :], (QW, ROLL_W))
        nat = pltpu.roll(row, 0, 1, stride=1, stride_axis=0)[:, 0:KW]
        rel_to_far = (nat.T - rel_ref[h:h + 1, 0:1]) * LOG2E
        bias_t[h] = jnp.where(visible, rel_to_far, NEG_INF)


def _block_kind(piece, half):
    lags = [kc - qc
            for kc in range(piece * PIECE // CHUNK, (piece + 1) * PIECE // CHUNK)
            for qc in range(half * HALF_W // CHUNK, (half + 1) * HALF_W // CHUNK)]
    live = any(0 <= lag <= N_LEFT_CHUNKS for lag in lags)
    plain = all(0 <= lag <= FAR_LAG for lag in lags)
    return live, plain


def _first_live(piece, half, lo):
    return not any(_block_kind(t, half)[0] for t in range(lo // PIECE, piece))


def _unit_blocks(piece):
    for head in range(2):
        for half in range(QW // HALF_W):
            live, plain = _block_kind(piece, half)
            if live:
                start = head * QW + half * HALF_W
                yield head, half, plain, slice(start, start + HALF_W)


def _score_steps(pair, quad, lo, qbuf, kbuf, bias_t, s_all, mx_all):
    s_ref, mx_ref = s_all.at[quad], mx_all.at[quad]
    k0 = quad * QW
    lane = lax.broadcasted_iota(jnp.int32, (1, PAIR_W), 1)
    qq = qbuf[pair, k0:k0 + QW, :]
    zero = jnp.zeros_like(qq)
    qm = jnp.concatenate([jnp.where(lane < HEAD_DIM, qq, zero),
                          jnp.where(lane >= HEAD_DIM, qq, zero)], axis=0)

    def step(r0, r1):
        s = _dot_nt(kbuf[pair, k0 + r0:k0 + r1, :], qm)
        running = {}
        for piece in range(r0 // PIECE, r1 // PIECE):
            r = piece * PIECE
            for head, half, plain, lanes in _unit_blocks(piece):
                sh = s[r - r0:r - r0 + PIECE, lanes]
                if not plain:
                    sh = sh + bias_t[2 * pair + head, r:r + PIECE,
                                     half * HALF_W:(half + 1) * HALF_W]
                s_ref[r:r + PIECE, lanes] = sh
                m = jnp.max(sh.reshape(PIECE // SUBLANES, SUBLANES, HALF_W), axis=0)
                key = (lanes.start, half)
                if key in running:
                    running[key] = jnp.maximum(running[key], m)
                elif _first_live(piece, half, lo):
                    running[key] = m
                else:
                    running[key] = jnp.maximum(mx_ref[:, lanes], m)
        for (start, _), m in running.items():
            mx_ref[:, start:start + HALF_W] = m

    starts = range(lo, KW, SCORE_ROWS)
    return [lambda r0=r0: step(r0, min(r0 + SCORE_ROWS, KW)) for r0 in starts]


def _softmax_steps(quad, lo, s_all, mx_all, p_all, sum_all):
    s_ref, mx_ref, p_ref, sum_ref = s_all.at[quad], mx_all.at[quad], p_all.at[quad], sum_all.at[quad]
    mx = jnp.max(mx_ref[...], axis=0, keepdims=True)

    def step(r0, r1):
        running = {}
        for piece in range(r0 // PIECE, r1 // PIECE):
            r = piece * PIECE
            for _, half, _, lanes in _unit_blocks(piece):
                e = jnp.exp2(s_ref[r:r + PIECE, lanes] - mx[:, lanes])
                p_ref[r:r + PIECE, lanes] = e.astype(jnp.bfloat16)
                part = jnp.sum(e.reshape(PIECE // SUBLANES, SUBLANES, HALF_W), axis=0)
                key = (lanes.start, half)
                if key in running:
                    running[key] = running[key] + part
                elif _first_live(piece, half, lo):
                    running[key] = part
                else:
                    running[key] = sum_ref[:, lanes] + part
        for (start, _), part in running.items():
            sum_ref[:, start:start + HALF_W] = part

    starts = range(lo, KW, SOFTMAX_ROWS)
    return [lambda r0=r0: step(r0, min(r0 + SOFTMAX_ROWS, KW)) for r0 in starts]


def _zero_dead_blocks(p_all):
    p_all[...] = jnp.zeros(p_all.shape, jnp.bfloat16)


def _weighted_values(pair, quad, lo, vtbuf, p_all, sum_all, out_t):
    row0 = pl.multiple_of(pair * PAIR_W, PAIR_W)
    k0 = quad * QW
    denom = jnp.sum(sum_all[quad], axis=0, keepdims=True)
    o_t = _dot(vtbuf[pl.ds(row0, PAIR_W), k0 + lo:k0 + KW], p_all[quad, lo:KW, :])
    inv = 1.0 / denom
    for head in range(2):
        rows = slice(head * HEAD_DIM, (head + 1) * HEAD_DIM)
        lanes = slice(head * QW, (head + 1) * QW)
        out_t[pl.ds(row0 + head * HEAD_DIM, HEAD_DIM), k0:k0 + QW] = o_t[rows, lanes] * inv[:, lanes]


def _interleave(a_steps, b_steps):
    done = 0
    for t, a_step in enumerate(a_steps):
        a_step()
        upto = (t + 1) * len(b_steps) // len(a_steps)
        for b_step in b_steps[done:upto]:
            b_step()
        done = upto


def _mixer_kernel(x_ref, g_pre_ref, w_in_ref, w_vt_ref, dw_w_ref, dw_b_ref,
                  ln_g_ref, ln_b_ref, rel_ref, w_out_ref, g_post_ref, o_ref,
                  hbuf, hshift, cbuf, qbuf, kbuf, vtbuf, bias_t, s_ref, mx_ref, p_ref, sum_ref,
                  out_t, mixbuf):
    b = pl.program_id(0)
    i = pl.program_id(1)

    @pl.when((b == 0) & (i == 0))
    def _():
        _build_bias(rel_ref, bias_t)
        _zero_dead_blocks(p_ref)

    @pl.when(i == 0)
    def _():
        hbuf[0:HALO_H, :] = jnp.zeros((HALO_H, CONV_WIDTH), jnp.float32)
        kbuf[:, 0:HALO_KV, :] = jnp.zeros((N_PAIRS, HALO_KV, PAIR_W), jnp.bfloat16)
        vtbuf[:, 0:HALO_KV] = jnp.zeros((ATTN_WIDTH, HALO_KV), jnp.bfloat16)

    xt = x_ref[...]
    u = _rms(xt, g_pre_ref[...]).astype(jnp.bfloat16)

    c_q = 2 * CONV_WIDTH
    c_k = c_q + ATTN_WIDTH
    a_val = _dot(u, w_in_ref[:, 0:CONV_WIDTH])
    a_gate = _dot(u, w_in_ref[:, CONV_WIDTH:c_q])
    hbuf[HALO_H:HALO_H + TM_MIX, :] = a_val * jax.nn.sigmoid(a_gate)

    for r in range(1, SUBLANES):
        hshift[r - 1] = hbuf[r:r + HSHIFT_ROWS, :]

    def conv_block(base):
        acc = jnp.broadcast_to(dw_b_ref[...][None], (CONV_RB // SUBLANES, SUBLANES, CONV_WIDTH))
        for j in range(CONV_KERNEL):
            off = HALO_H - (CONV_KERNEL - 1) + j
            r = off % SUBLANES
            rows = slice(base + off - r, base + off - r + CONV_RB)
            tap = hbuf[rows, :] if r == 0 else hshift[r - 1, rows, :]
            acc = acc + dw_w_ref[j][None] * tap.reshape(acc.shape)
        cbuf[base:base + CONV_RB, :] = acc.reshape(CONV_RB, CONV_WIDTH)

    for blk in range(TM_MIX // PROJ_RB):
        rows = slice(blk * PROJ_RB, (blk + 1) * PROJ_RB)
        krows = slice(HALO_KV + blk * PROJ_RB, HALO_KV + (blk + 1) * PROJ_RB)
        ub = u[rows, :]
        q = (_dot(ub, w_in_ref[:, c_q:c_k]) * (HEAD_DIM ** -0.5 * LOG2E)).astype(jnp.bfloat16)
        k = _dot(ub, w_in_ref[:, c_k:c_k + ATTN_WIDTH]).astype(jnp.bfloat16)
        for p in range(N_PAIRS):
            qbuf[p, rows, :] = q[:, p * PAIR_W:(p + 1) * PAIR_W]
            kbuf[p, krows, :] = k[:, p * PAIR_W:(p + 1) * PAIR_W]
        if blk % 2 == 0:
            wide = slice(blk * PROJ_RB, (blk + 2) * PROJ_RB)
            vtbuf[:, HALO_KV + wide.start:HALO_KV + wide.stop] = (
                _dot_nt(w_vt_ref[...], u[wide, :]).astype(jnp.bfloat16))
        for rb in range(PROJ_RB // CONV_RB):
            conv_block(blk * PROJ_RB + rb * CONV_RB)

    for blk in range(TM_MIX // LN_RB):
        rows = slice(blk * LN_RB, (blk + 1) * LN_RB)
        c = cbuf[rows, :]
        mu = jnp.mean(c, axis=-1, keepdims=True)
        xc = c - mu
        var = jnp.mean(xc * xc, axis=-1, keepdims=True)
        y = xc * lax.rsqrt(var + EPS) * ln_g_ref[...] + ln_b_ref[...]
        y = y * jax.nn.sigmoid(y)
        mixbuf[rows, 0:CONV_WIDTH] = y.astype(jnp.bfloat16)

    def attn_loop(first_tile):
        lo = [max(0, HALO_KV - quad * QW) if first_tile else 0 for quad in range(N_QUADS)]

        def scores(pair, quad):
            return _score_steps(pair, quad, lo[quad], qbuf, kbuf, bias_t, s_ref, mx_ref)

        for step in scores(0, 0):
            step()

        def body(pair, carry):
            for quad in range(N_QUADS):
                if quad + 1 < N_QUADS:
                    nxt = scores(pair, quad + 1)
                else:
                    nxt = scores(jnp.minimum(pair + 1, N_PAIRS - 1), 0)
                _interleave(nxt, _softmax_steps(quad, lo[quad], s_ref, mx_ref, p_ref, sum_ref))
                _weighted_values(pair, quad, lo[quad], vtbuf, p_ref, sum_ref, out_t)
            return carry
        lax.fori_loop(0, N_PAIRS, body, 0)

    @pl.when(i == 0)
    def _():
        attn_loop(True)

    @pl.when(i > 0)
    def _():
        attn_loop(False)

    mixbuf[:, CONV_WIDTH:CONV_WIDTH + ATTN_WIDTH] = out_t[...].T.astype(jnp.bfloat16)
    mixed = _dot(mixbuf[...], w_out_ref[...])
    o_ref[...] = xt + _rms(mixed, g_post_ref[...])

    hbuf[0:HALO_H, :] = hbuf[TM_MIX:TM_MIX + HALO_H, :]
    kbuf[:, 0:HALO_KV, :] = kbuf[:, TM_MIX:TM_MIX + HALO_KV, :]
    vtbuf[:, 0:HALO_KV] = vtbuf[:, TM_MIX:TM_MIX + HALO_KV]


def _ffn_kernel(h_hbm, g_pre_ref, w_up_ref, dw_w_ref, dw_b_ref, w_down_ref, g_post_ref, o_hbm,
                xin, xout, sem_in, sem_out, carry, gbuf, vbuf, actbuf):
    i = pl.program_id(1)
    n = pl.program_id(0) * pl.num_programs(1) + i
    n_tiles = pl.num_programs(0) * pl.num_programs(1)
    slot = n % 2

    def in_copy(tile, sl, s):
        return pltpu.make_async_copy(h_hbm.at[tile, s], xin.at[sl, :, s, :], sem_in.at[sl])

    def out_copy(tile, sl, s):
        return pltpu.make_async_copy(xout.at[sl, :, s, :], o_hbm.at[tile, s], sem_out.at[sl])

    @pl.when(n == 0)
    def _():
        for s in range(SUBLANES):
            in_copy(0, 0, s).start()

    @pl.when(n + 1 < n_tiles)
    def _():
        for s in range(SUBLANES):
            in_copy(n + 1, 1 - slot, s).start()

    @pl.when(i == 0)
    def _():
        carry[...] = jnp.zeros(carry.shape, jnp.float32)

    for s in range(SUBLANES):
        in_copy(n, slot, s).wait()

    xt = xin[slot].reshape(TM_FFN, D_MODEL)
    u = _rms(xt, g_pre_ref[...]).astype(jnp.bfloat16)
    first_sublane = lax.broadcasted_iota(jnp.int32, (SUBLANES, FFN_CW), 0) == 0

    def up_conv(buf, cols):
        hc = _dot(u, w_up_ref[:, cols])
        for k in (1, 2):
            last = hc[TM_FFN - k * SUBLANES:TM_FFN - (k - 1) * SUBLANES, :]
            prev = carry[(2 - k) * SUBLANES:(3 - k) * SUBLANES, cols]
            buf[(2 - k) * SUBLANES:(3 - k) * SUBLANES, :] = jnp.where(
                first_sublane, pltpu.roll(prev, 1, 0), pltpu.roll(last, 1, 0))
            carry[(2 - k) * SUBLANES:(3 - k) * SUBLANES, cols] = last
        buf[FFN_HALO:FFN_HALO + TM_FFN, :] = hc
        y = dw_b_ref[:, cols] + dw_w_ref[2:3, cols] * hc
        y = y + dw_w_ref[1:2, cols] * buf[SUBLANES:SUBLANES + TM_FFN, :]
        y = y + dw_w_ref[0:1, cols] * buf[0:TM_FFN, :]
        return y

    for c in range(D_FF // FFN_CW):
        gate = up_conv(gbuf.at[c % 2], slice(c * FFN_CW, (c + 1) * FFN_CW))
        val = up_conv(vbuf.at[c % 2], slice(D_FF + c * FFN_CW, D_FF + (c + 1) * FFN_CW))
        actbuf[:, c * FFN_CW:(c + 1) * FFN_CW] = (jax.nn.gelu(gate) * val).astype(jnp.bfloat16)
    f = _dot(actbuf[...], w_down_ref[...])
    y = xt + _rms(f, g_post_ref[...])

    @pl.when(n >= 2)
    def _():
        for s in range(SUBLANES):
            out_copy(n - 2, slot, s).wait()

    xout[slot] = y.reshape(FFN_SEG, SUBLANES, D_MODEL)
    for s in range(SUBLANES):
        out_copy(n, slot, s).start()

    @pl.when(n == n_tiles - 1)
    def _():
        @pl.when(n >= 1)
        def _():
            for s in range(SUBLANES):
                out_copy(n - 1, 1 - slot, s).wait()
        for s in range(SUBLANES):
            out_copy(n, slot, s).wait()


def _rel_distance_row(rel_table):
    h = rel_table.shape[0]
    far = rel_table[:, 2 * MAX_REL:2 * MAX_REL + 1]
    n_far = HALO_KV - MAX_REL + 1
    near = jnp.flip(rel_table, axis=1)[:, 1:]
    n_tail = ROLL_W - n_far - near.shape[1]
    return jnp.concatenate([jnp.broadcast_to(far, (h, n_far)), near,
                            jnp.broadcast_to(far, (h, n_tail))], axis=1)


def _const_spec(shape):
    return pl.BlockSpec(shape, lambda b, i: (0,) * len(shape), pipeline_mode=pl.Buffered(1))


def _mixer(x, g_pre, w_in, w_vt, dw_w, dw_b, ln_g, ln_b, rel_row, w_out, g_post):
    B, T, D = x.shape
    row_spec = pl.BlockSpec((None, TM_MIX, D), lambda b, i: (b, i, 0))
    consts = (g_pre, w_in, w_vt, dw_w, dw_b, ln_g, ln_b, rel_row, w_out, g_post)
    return pl.pallas_call(
        _mixer_kernel,
        grid=(B, T // TM_MIX),
        in_specs=[row_spec] + [_const_spec(c.shape) for c in consts],
        out_specs=row_spec,
        out_shape=jax.ShapeDtypeStruct(x.shape, x.dtype),
        scratch_shapes=[
            pltpu.VMEM((HALO_H + TM_MIX, CONV_WIDTH), jnp.float32),
            pltpu.VMEM((SUBLANES - 1, HSHIFT_ROWS, CONV_WIDTH), jnp.float32),
            pltpu.VMEM((TM_MIX, CONV_WIDTH), jnp.float32),
            pltpu.VMEM((N_PAIRS, TM_MIX, PAIR_W), jnp.bfloat16),
            pltpu.VMEM((N_PAIRS, HALO_KV + TM_MIX, PAIR_W), jnp.bfloat16),
            pltpu.VMEM((ATTN_WIDTH, HALO_KV + TM_MIX), jnp.bfloat16),
            pltpu.VMEM((N_HEADS, KW, QW), jnp.float32),
            pltpu.VMEM((N_QUADS, KW, 2 * QW), jnp.float32),
            pltpu.VMEM((N_QUADS, SUBLANES, 2 * QW), jnp.float32),
            pltpu.VMEM((N_QUADS, KW, 2 * QW), jnp.bfloat16),
            pltpu.VMEM((N_QUADS, SUBLANES, 2 * QW), jnp.float32),
            pltpu.VMEM((ATTN_WIDTH, TM_MIX), jnp.float32),
            pltpu.VMEM((TM_MIX, CONV_WIDTH + ATTN_WIDTH), jnp.bfloat16),
        ],
        compiler_params=pltpu.CompilerParams(
            dimension_semantics=("arbitrary", "arbitrary"), vmem_limit_bytes=VMEM_LIMIT),
        name="mixer",
    )(x, *consts)


def _ffn(h, g_pre, w_up, dw_w, dw_b, w_down, g_post):
    B, T, D = h.shape
    n_t = T // TM_FFN
    tiles = (B * n_t, SUBLANES, FFN_SEG, D)
    hbm_spec = pl.BlockSpec(memory_space=pl.ANY)
    out = pl.pallas_call(
        _ffn_kernel,
        grid=(B, n_t),
        in_specs=[hbm_spec,
                  _const_spec((1, D)), _const_spec(w_up.shape), _const_spec(dw_w.shape),
                  _const_spec((1, 2 * D_FF)), _const_spec(w_down.shape), _const_spec((1, D))],
        out_specs=hbm_spec,
        out_shape=jax.ShapeDtypeStruct(tiles, h.dtype),
        scratch_shapes=[
            pltpu.VMEM((2, FFN_SEG, SUBLANES, D), jnp.float32),
            pltpu.VMEM((2, FFN_SEG, SUBLANES, D), jnp.float32),
            pltpu.SemaphoreType.DMA((2,)),
            pltpu.SemaphoreType.DMA((2,)),
            pltpu.VMEM((FFN_HALO, 2 * D_FF), jnp.float32),
            pltpu.VMEM((2, FFN_HALO + TM_FFN, FFN_CW), jnp.float32),
            pltpu.VMEM((2, FFN_HALO + TM_FFN, FFN_CW), jnp.float32),
            pltpu.VMEM((TM_FFN, D_FF), jnp.bfloat16),
        ],
        compiler_params=pltpu.CompilerParams(
            dimension_semantics=("arbitrary", "arbitrary"), vmem_limit_bytes=VMEM_LIMIT),
        name="ffn",
    )(h.reshape(tiles), g_pre, w_up, dw_w, dw_b, w_down, g_post)
    return out.reshape(B, T, D)


def kernel(x, norm_mix_pre, w_in, conv_dw_w, conv_dw_b, conv_ln_g, conv_ln_b, rel_bias, w_out,
           norm_mix_post, norm_ffn_pre, w_up, ffn_dw_w, ffn_dw_b, w_down, norm_ffn_post):
    bf16 = jnp.bfloat16
    c0 = 2 * CONV_WIDTH
    h = x
    for l in range(norm_mix_pre.shape[0]):
        w = w_in[l].astype(bf16)
        h = _mixer(h, norm_mix_pre[l][None], w, w[:, c0 + 2 * ATTN_WIDTH:].T,
                   jnp.broadcast_to(conv_dw_w[l][:, None, :], (CONV_KERNEL, SUBLANES, CONV_WIDTH)),
                   conv_dw_b[l][None], conv_ln_g[l][None], conv_ln_b[l][None],
                   _rel_distance_row(rel_bias[l]), w_out[l].astype(bf16), norm_mix_post[l][None])
        h = _ffn(h, norm_ffn_pre[l][None], w_up[l].astype(bf16), ffn_dw_w[l],
                 ffn_dw_b[l][None], w_down[l].astype(bf16), norm_ffn_post[l][None])
    return h
```

```python
import jax
import jax.numpy as jnp
from jax import lax
from jax.experimental import pallas as pl
from jax.experimental.pallas import tpu as pltpu

D_MODEL = 1024
CHUNK = 64
N_LEFT_CHUNKS = 8
CONV_WIDTH = 512
ATTN_WIDTH = 512
HEAD_DIM = 64
N_HEADS = ATTN_WIDTH // HEAD_DIM
N_PAIRS = N_HEADS // 2
PAIR_W = 2 * HEAD_DIM
CONV_KERNEL = 31
MAX_REL = 128
D_FF = 2816
FFN_CONV_KERNEL = 3
EPS = 1e-6
NEG_INF = -1e30

TM_MIX = 512
HALO_H = 32
HALO_KV = N_LEFT_CHUNKS * CHUNK
CONV_RB = 32
OUT_RB = 256
PROJ_RB = 256
VT_COLS = 256
SUBLANES = 8
HSHIFT_LEAD = HALO_H - SUBLANES
HSHIFT_ROWS = HALO_H + TM_MIX - SUBLANES
QUAD = 4
QW = QUAD * CHUNK
KW = HALO_KV + QW
N_QUADS = TM_MIX // QW
PIECE = 128
SCORE_ROWS = 384
SOFTMAX_ROWS = 256
HALF_W = 128
FAR_LAG = (HALO_KV - MAX_REL) // CHUNK - 1
LOG2E = 1.4426950408889634
ROLL_W = 1024
TM_FFN = 512
FFN_CW = 256
FFN_SEG = TM_FFN // SUBLANES
FFN_HALO = (FFN_CONV_KERNEL - 1) * SUBLANES
VMEM_LIMIT = 56 * 1024 * 1024


def _rms(xf, g):
    return xf * lax.rsqrt(jnp.mean(xf * xf, axis=-1, keepdims=True) + EPS) * g


def _dot(a, b):
    return jnp.dot(a, b, preferred_element_type=jnp.float32)


def _dot_nt(a, b):
    return lax.dot_general(a, b, (((1,), (1,)), ((), ())), preferred_element_type=jnp.float32)


def _build_bias(rel_ref, bias_t):
    r_idx = lax.broadcasted_iota(jnp.int32, (KW, QW), 0) // CHUNK
    c_idx = lax.broadcasted_iota(jnp.int32, (KW, QW), 1) // CHUNK
    visible = (r_idx >= c_idx) & (r_idx <= c_idx + N_LEFT_CHUNKS)
    for h in range(N_HEADS):
        row = jnp.broadcast_to(rel_ref[h:h + 1, :], (QW, ROLL_W))
        nat = pltpu.roll(row, 0, 1, stride=1, stride_axis=0)[:, 0:KW]
        rel_to_far = (nat.T - rel_ref[h:h + 1, 0:1]) * LOG2E
        bias_t[h] = jnp.where(visible, rel_to_far, NEG_INF)


def _block_kind(piece, half):
    lags = [kc - qc
            for kc in range(piece * PIECE // CHUNK, (piece + 1) * PIECE // CHUNK)
            for qc in range(half * HALF_W // CHUNK, (half + 1) * HALF_W // CHUNK)]
    live = any(0 <= lag <= N_LEFT_CHUNKS for lag in lags)
    plain = all(0 <= lag <= FAR_LAG for lag in lags)
    return live, plain


def _first_live(piece, half, lo):
    return not any(_block_kind(t, half)[0] for t in range(lo // PIECE, piece))


def _unit_blocks(piece):
    for head in range(2):
        for half in range(QW // HALF_W):
            live, plain = _block_kind(piece, half)
            if live:
                start = head * QW + half * HALF_W
                yield head, half, plain, slice(start, start + HALF_W)


def _score_steps(pair, quad, lo, qbuf, kbuf, bias_t, s_all, mx_all):
    s_ref, mx_ref = s_all.at[quad], mx_all.at[quad]
    k0 = quad * QW
    lane = lax.broadcasted_iota(jnp.int32, (1, PAIR_W), 1)
    qq = qbuf[pair, k0:k0 + QW, :]
    zero = jnp.zeros_like(qq)
    qm = jnp.concatenate([jnp.where(lane < HEAD_DIM, qq, zero),
                          jnp.where(lane >= HEAD_DIM, qq, zero)], axis=0)

    def step(r0, r1):
        s = _dot_nt(kbuf[pair, k0 + r0:k0 + r1, :], qm)
        running = {}
        for piece in range(r0 // PIECE, r1 // PIECE):
            r = piece * PIECE
            for head, half, plain, lanes in _unit_blocks(piece):
                sh = s[r - r0:r - r0 + PIECE, lanes]
                if not plain:
                    sh = sh + bias_t[2 * pair + head, r:r + PIECE,
                                     half * HALF_W:(half + 1) * HALF_W]
                s_ref[r:r + PIECE, lanes] = sh
                m = jnp.max(sh.reshape(PIECE // SUBLANES, SUBLANES, HALF_W), axis=0)
                key = (lanes.start, half)
                if key in running:
                    running[key] = jnp.maximum(running[key], m)
                elif _first_live(piece, half, lo):
                    running[key] = m
                else:
                    running[key] = jnp.maximum(mx_ref[:, lanes], m)
        for (start, _), m in running.items():
            mx_ref[:, start:start + HALF_W] = m

    starts = range(lo, KW, SCORE_ROWS)
    return [lambda r0=r0: step(r0, min(r0 + SCORE_ROWS, KW)) for r0 in starts]


def _softmax_steps(quad, lo, s_all, mx_all, p_all, sum_all):
    s_ref, mx_ref, p_ref, sum_ref = s_all.at[quad], mx_all.at[quad], p_all.at[quad], sum_all.at[quad]
    mx = jnp.max(mx_ref[...], axis=0, keepdims=True)

    def step(r0, r1):
        running = {}
        for piece in range(r0 // PIECE, r1 // PIECE):
            r = piece * PIECE
            for _, half, _, lanes in _unit_blocks(piece):
                e = jnp.exp2(s_ref[r:r + PIECE, lanes] - mx[:, lanes])
                p_ref[r:r + PIECE, lanes] = e.astype(jnp.bfloat16)
                part = jnp.sum(e.reshape(PIECE // SUBLANES, SUBLANES, HALF_W), axis=0)
                key = (lanes.start, half)
                if key in running:
                    running[key] = running[key] + part
                elif _first_live(piece, half, lo):
                    running[key] = part
                else:
                    running[key] = sum_ref[:, lanes] + part
        for (start, _), part in running.items():
            sum_ref[:, start:start + HALF_W] = part

    starts = range(lo, KW, SOFTMAX_ROWS)
    return [lambda r0=r0: step(r0, min(r0 + SOFTMAX_ROWS, KW)) for r0 in starts]


def _zero_dead_blocks(p_all):
    p_all[...] = jnp.zeros(p_all.shape, jnp.bfloat16)


def _weighted_values(pair, quad, lo, vtbuf, p_all, sum_all, out_t):
    row0 = pl.multiple_of(pair * PAIR_W, PAIR_W)
    k0 = quad * QW
    denom = jnp.sum(sum_all[quad], axis=0, keepdims=True)
    o_t = _dot(vtbuf[pl.ds(row0, PAIR_W), k0 + lo:k0 + KW], p_all[quad, lo:KW, :])
    inv = 1.0 / denom
    for head in range(2):
        rows = slice(head * HEAD_DIM, (head + 1) * HEAD_DIM)
        lanes = slice(head * QW, (head + 1) * QW)
        out_t[pl.ds(row0 + head * HEAD_DIM, HEAD_DIM), k0:k0 + QW] = o_t[rows, lanes] * inv[:, lanes]


def _interleave(a_steps, b_steps):
    done = 0
    for t, a_step in enumerate(a_steps):
        a_step()
        upto = (t + 1) * len(b_steps) // len(a_steps)
        for b_step in b_steps[done:upto]:
            b_step()
        done = upto


def _alternate(a_steps, b_steps):
    for t in range(max(len(a_steps), len(b_steps))):
        if t < len(a_steps):
            a_steps[t]()
        if t < len(b_steps):
            b_steps[t]()


def _mixer_kernel(x_ref, g_pre_ref, w_in_ref, w_vt_ref, dw_w_ref, dw_b_ref,
                  ln_g_ref, ln_b_ref, rel_ref, w_out_ref, g_post_ref, o_ref,
                  hbuf, hshift, cbuf, qbuf, kbuf, vtbuf, bias_t, s_ref, mx_ref, p_ref, sum_ref,
                  out_t, mixbuf):
    b = pl.program_id(0)
    i = pl.program_id(1)

    @pl.when((b == 0) & (i == 0))
    def _():
        _build_bias(rel_ref, bias_t)
        _zero_dead_blocks(p_ref)

    @pl.when(i == 0)
    def _():
        hbuf[0:HALO_H, :] = jnp.zeros((HALO_H, CONV_WIDTH), jnp.float32)
        kbuf[:, 0:HALO_KV, :] = jnp.zeros((N_PAIRS, HALO_KV, PAIR_W), jnp.bfloat16)
        vtbuf[:, 0:HALO_KV] = jnp.zeros((ATTN_WIDTH, HALO_KV), jnp.bfloat16)

    c_q = 2 * CONV_WIDTH
    c_k = c_q + ATTN_WIDTH

    def conv_block(base):
        acc = jnp.broadcast_to(dw_b_ref[...][None], (CONV_RB // SUBLANES, SUBLANES, CONV_WIDTH))
        for j in range(CONV_KERNEL):
            off = HALO_H - (CONV_KERNEL - 1) + j
            r = off % SUBLANES
            rows = slice(base + off - r, base + off - r + CONV_RB)
            tap = hbuf[rows, :] if r == 0 else hshift[r - 1, rows, :]
            acc = acc + dw_w_ref[j][None] * tap.reshape(acc.shape)
        cbuf[base:base + CONV_RB, :] = acc.reshape(CONV_RB, CONV_WIDTH)

    pending = []

    def project_steps(blk):
        lo_row, hi_row = blk * PROJ_RB, (blk + 1) * PROJ_RB
        rows = slice(lo_row, hi_row)
        state = {}

        def norm_and_value():
            state["u"] = _rms(x_ref[rows, :], g_pre_ref[...]).astype(jnp.bfloat16)
            state["a_val"] = _dot(state["u"], w_in_ref[:, 0:CONV_WIDTH])

        def gate():
            a_gate = _dot(state["u"], w_in_ref[:, CONV_WIDTH:c_q])
            hbuf[HALO_H + lo_row:HALO_H + hi_row, :] = state["a_val"] * jax.nn.sigmoid(a_gate)

        def queries():
            q = _dot(state["u"], w_in_ref[:, c_q:c_k]) * (HEAD_DIM ** -0.5 * LOG2E)
            q = q.astype(jnp.bfloat16)
            for p in range(N_PAIRS):
                qbuf[p, rows, :] = q[:, p * PAIR_W:(p + 1) * PAIR_W]

        def keys_values():
            k = _dot(state["u"], w_in_ref[:, c_k:c_k + ATTN_WIDTH]).astype(jnp.bfloat16)
            for p in range(N_PAIRS):
                kbuf[p, HALO_KV + lo_row:HALO_KV + hi_row, :] = k[:, p * PAIR_W:(p + 1) * PAIR_W]
            pending.append(state["u"])
            if hi_row % VT_COLS == 0:
                u_wide = pending[0] if len(pending) == 1 else jnp.concatenate(pending, axis=0)
                vtbuf[:, HALO_KV + hi_row - VT_COLS:HALO_KV + hi_row] = (
                    _dot_nt(w_vt_ref[...], u_wide).astype(jnp.bfloat16))
                pending.clear()

        return [norm_and_value, gate, queries, keys_values]

    def mix_steps(blk):
        lo_row, hi_row = blk * PROJ_RB, (blk + 1) * PROJ_RB
        rows = slice(lo_row, hi_row)

        def shifted_copies():
            sh_lo = 0 if blk == 0 else lo_row + HSHIFT_LEAD
            sh_hi = hi_row + HSHIFT_LEAD
            for r in range(1, SUBLANES):
                hshift[r - 1, sh_lo:sh_hi, :] = hbuf[sh_lo + r:sh_hi + r, :]

        def norm_swish():
            c = cbuf[rows, :]
            mu = jnp.mean(c, axis=-1, keepdims=True)
            xc = c - mu
            var = jnp.mean(xc * xc, axis=-1, keepdims=True)
            y = xc * lax.rsqrt(var + EPS) * ln_g_ref[...] + ln_b_ref[...]
            y = y * jax.nn.sigmoid(y)
            mixbuf[rows, 0:CONV_WIDTH] = y.astype(jnp.bfloat16)

        convs = [lambda base=lo_row + rb * CONV_RB: conv_block(base)
                 for rb in range(PROJ_RB // CONV_RB)]
        return [shifted_copies] + convs + [norm_swish]

    n_blocks = TM_MIX // PROJ_RB
    for step in project_steps(0):
        step()
    for blk in range(n_blocks):
        nxt = project_steps(blk + 1) if blk + 1 < n_blocks else []
        _alternate(nxt, mix_steps(blk))

    def attn_loop(first_tile):
        lo = [max(0, HALO_KV - quad * QW) if first_tile else 0 for quad in range(N_QUADS)]

        def scores(pair, quad):
            return _score_steps(pair, quad, lo[quad], qbuf, kbuf, bias_t, s_ref, mx_ref)

        for step in scores(0, 0):
            step()

        def body(pair, carry):
            for quad in range(N_QUADS):
                if quad + 1 < N_QUADS:
                    nxt = scores(pair, quad + 1)
                else:
                    nxt = scores(jnp.minimum(pair + 1, N_PAIRS - 1), 0)
                _interleave(nxt, _softmax_steps(quad, lo[quad], s_ref, mx_ref, p_ref, sum_ref))
                _weighted_values(pair, quad, lo[quad], vtbuf, p_ref, sum_ref, out_t)
            return carry
        lax.fori_loop(0, N_PAIRS, body, 0)

    @pl.when(i == 0)
    def _():
        attn_loop(True)

    @pl.when(i > 0)
    def _():
        attn_loop(False)

    for blk in range(TM_MIX // OUT_RB):
        rows = slice(blk * OUT_RB, (blk + 1) * OUT_RB)
        attn = out_t[:, rows].T.astype(jnp.bfloat16)
        mixed = _dot(jnp.concatenate([mixbuf[rows, 0:CONV_WIDTH], attn], axis=1), w_out_ref[...])
        o_ref[rows, :] = x_ref[rows, :] + _rms(mixed, g_post_ref[...])

    hbuf[0:HALO_H, :] = hbuf[TM_MIX:TM_MIX + HALO_H, :]
    kbuf[:, 0:HALO_KV, :] = kbuf[:, TM_MIX:TM_MIX + HALO_KV, :]
    vtbuf[:, 0:HALO_KV] = vtbuf[:, TM_MIX:TM_MIX + HALO_KV]


def _ffn_kernel(h_hbm, g_pre_ref, w_up_ref, dw_w_ref, dw_b_ref, w_down_ref, g_post_ref, o_hbm,
                xin, xout, sem_in, sem_out, carry, gbuf, vbuf, actbuf):
    i = pl.program_id(1)
    n = pl.program_id(0) * pl.num_programs(1) + i
    n_tiles = pl.num_programs(0) * pl.num_programs(1)
    slot = n % 2

    def in_copy(tile, sl, s):
        return pltpu.make_async_copy(h_hbm.at[tile, s], xin.at[sl, :, s, :], sem_in.at[sl])

    def out_copy(tile, sl, s):
        return pltpu.make_async_copy(xout.at[sl, :, s, :], o_hbm.at[tile, s], sem_out.at[sl])

    @pl.when(n == 0)
    def _():
        for s in range(SUBLANES):
            in_copy(0, 0, s).start()

    @pl.when(n + 1 < n_tiles)
    def _():
        for s in range(SUBLANES):
            in_copy(n + 1, 1 - slot, s).start()

    @pl.when(i == 0)
    def _():
        carry[...] = jnp.zeros(carry.shape, jnp.float32)

    for s in range(SUBLANES):
        in_copy(n, slot, s).wait()

    xt = xin[slot].reshape(TM_FFN, D_MODEL)
    u = _rms(xt, g_pre_ref[...]).astype(jnp.bfloat16)
    first_sublane = lax.broadcasted_iota(jnp.int32, (SUBLANES, FFN_CW), 0) == 0

    def up_conv(buf, cols):
        hc = _dot(u, w_up_ref[:, cols])
        for k in (1, 2):
            last = hc[TM_FFN - k * SUBLANES:TM_FFN - (k - 1) * SUBLANES, :]
            prev = carry[(2 - k) * SUBLANES:(3 - k) * SUBLANES, cols]
            buf[(2 - k) * SUBLANES:(3 - k) * SUBLANES, :] = jnp.where(
                first_sublane, pltpu.roll(prev, 1, 0), pltpu.roll(last, 1, 0))
            carry[(2 - k) * SUBLANES:(3 - k) * SUBLANES, cols] = last
        buf[FFN_HALO:FFN_HALO + TM_FFN, :] = hc
        y = dw_b_ref[:, cols] + dw_w_ref[2:3, cols] * hc
        y = y + dw_w_ref[1:2, cols] * buf[SUBLANES:SUBLANES + TM_FFN, :]
        y = y + dw_w_ref[0:1, cols] * buf[0:TM_FFN, :]
        return y

    for c in range(D_FF // FFN_CW):
        gate = up_conv(gbuf.at[c % 2], slice(c * FFN_CW, (c + 1) * FFN_CW))
        val = up_conv(vbuf.at[c % 2], slice(D_FF + c * FFN_CW, D_FF + (c + 1) * FFN_CW))
        actbuf[:, c * FFN_CW:(c + 1) * FFN_CW] = (jax.nn.gelu(gate) * val).astype(jnp.bfloat16)
    f = _dot(actbuf[...], w_down_ref[...])
    y = xt + _rms(f, g_post_ref[...])

    @pl.when(n >= 2)
    def _():
        for s in range(SUBLANES):
            out_copy(n - 2, slot, s).wait()

    xout[slot] = y.reshape(FFN_SEG, SUBLANES, D_MODEL)
    for s in range(SUBLANES):
        out_copy(n, slot, s).start()

    @pl.when(n == n_tiles - 1)
    def _():
        @pl.when(n >= 1)
        def _():
            for s in range(SUBLANES):
                out_copy(n - 1, 1 - slot, s).wait()
        for s in range(SUBLANES):
            out_copy(n, slot, s).wait()


def _rel_distance_row(rel_table):
    h = rel_table.shape[0]
    far = rel_table[:, 2 * MAX_REL:2 * MAX_REL + 1]
    n_far = HALO_KV - MAX_REL + 1
    near = jnp.flip(rel_table, axis=1)[:, 1:]
    n_tail = ROLL_W - n_far - near.shape[1]
    return jnp.concatenate([jnp.broadcast_to(far, (h, n_far)), near,
                            jnp.broadcast_to(far, (h, n_tail))], axis=1)


def _const_spec(shape):
    return pl.BlockSpec(shape, lambda b, i: (0,) * len(shape), pipeline_mode=pl.Buffered(1))


def _mixer(x, g_pre, w_in, w_vt, dw_w, dw_b, ln_g, ln_b, rel_row, w_out, g_post):
    B, T, D = x.shape
    row_spec = pl.BlockSpec((None, TM_MIX, D), lambda b, i: (b, i, 0))
    consts = (g_pre, w_in, w_vt, dw_w, dw_b, ln_g, ln_b, rel_row, w_out, g_post)
    return pl.pallas_call(
        _mixer_kernel,
        grid=(B, T // TM_MIX),
        in_specs=[row_spec] + [_const_spec(c.shape) for c in consts],
        out_specs=row_spec,
        out_shape=jax.ShapeDtypeStruct(x.shape, x.dtype),
        scratch_shapes=[
            pltpu.VMEM((HALO_H + TM_MIX, CONV_WIDTH), jnp.float32),
            pltpu.VMEM((SUBLANES - 1, HSHIFT_ROWS, CONV_WIDTH), jnp.float32),
            pltpu.VMEM((TM_MIX, CONV_WIDTH), jnp.float32),
            pltpu.VMEM((N_PAIRS, TM_MIX, PAIR_W), jnp.bfloat16),
            pltpu.VMEM((N_PAIRS, HALO_KV + TM_MIX, PAIR_W), jnp.bfloat16),
            pltpu.VMEM((ATTN_WIDTH, HALO_KV + TM_MIX), jnp.bfloat16),
            pltpu.VMEM((N_HEADS, KW, QW), jnp.float32),
            pltpu.VMEM((N_QUADS, KW, 2 * QW), jnp.float32),
            pltpu.VMEM((N_QUADS, SUBLANES, 2 * QW), jnp.float32),
            pltpu.VMEM((N_QUADS, KW, 2 * QW), jnp.bfloat16),
            pltpu.VMEM((N_QUADS, SUBLANES, 2 * QW), jnp.float32),
            pltpu.VMEM((ATTN_WIDTH, TM_MIX), jnp.float32),
            pltpu.VMEM((TM_MIX, CONV_WIDTH + ATTN_WIDTH), jnp.bfloat16),
        ],
        compiler_params=pltpu.CompilerParams(
            dimension_semantics=("arbitrary", "arbitrary"), vmem_limit_bytes=VMEM_LIMIT),
        name="mixer",
    )(x, *consts)


def _ffn(h, g_pre, w_up, dw_w, dw_b, w_down, g_post):
    B, T, D = h.shape
    n_t = T // TM_FFN
    tiles = (B * n_t, SUBLANES, FFN_SEG, D)
    hbm_spec = pl.BlockSpec(memory_space=pl.ANY)
    out = pl.pallas_call(
        _ffn_kernel,
        grid=(B, n_t),
        in_specs=[hbm_spec,
                  _const_spec((1, D)), _const_spec(w_up.shape), _const_spec(dw_w.shape),
                  _const_spec((1, 2 * D_FF)), _const_spec(w_down.shape), _const_spec((1, D))],
        out_specs=hbm_spec,
        out_shape=jax.ShapeDtypeStruct(tiles, h.dtype),
        scratch_shapes=[
            pltpu.VMEM((2, FFN_SEG, SUBLANES, D), jnp.float32),
            pltpu.VMEM((2, FFN_SEG, SUBLANES, D), jnp.float32),
            pltpu.SemaphoreType.DMA((2,)),
            pltpu.SemaphoreType.DMA((2,)),
            pltpu.VMEM((FFN_HALO, 2 * D_FF), jnp.float32),
            pltpu.VMEM((2, FFN_HALO + TM_FFN, FFN_CW), jnp.float32),
            pltpu.VMEM((2, FFN_HALO + TM_FFN, FFN_CW), jnp.float32),
            pltpu.VMEM((TM_FFN, D_FF), jnp.bfloat16),
        ],
        compiler_params=pltpu.CompilerParams(
            dimension_semantics=("arbitrary", "arbitrary"), vmem_limit_bytes=VMEM_LIMIT),
        name="ffn",
    )(h.reshape(tiles), g_pre, w_up, dw_w, dw_b, w_down, g_post)
    return out.reshape(B, T, D)


def kernel(x, norm_mix_pre, w_in, conv_dw_w, conv_dw_b, conv_ln_g, conv_ln_b, rel_bias, w_out,
           norm_mix_post, norm_ffn_pre, w_up, ffn_dw_w, ffn_dw_b, w_down, norm_ffn_post):
    bf16 = jnp.bfloat16
    c0 = 2 * CONV_WIDTH
    h = x
    for l in range(norm_mix_pre.shape[0]):
        w = w_in[l].astype(bf16)
        h = _mixer(h, norm_mix_pre[l][None], w, w[:, c0 + 2 * ATTN_WIDTH:].T,
                   jnp.broadcast_to(conv_dw_w[l][:, None, :], (CONV_KERNEL, SUBLANES, CONV_WIDTH)),
                   conv_dw_b[l][None], conv_ln_g[l][None], conv_ln_b[l][None],
                   _rel_distance_row(rel_bias[l]), w_out[l].astype(bf16), norm_mix_post[l][None])
        h = _ffn(h, norm_ffn_pre[l][None], w_up[l].astype(bf16), ffn_dw_w[l],
                 ffn_dw_b[l][None], w_down[l].astype(bf16), norm_ffn_post[l][None])
    return h
```

```python
import jax
import jax.numpy as jnp
from jax import lax
from jax.experimental import pallas as pl
from jax.experimental.pallas import tpu as pltpu

D_MODEL = 1024
CHUNK = 64
N_LEFT_CHUNKS = 8
CONV_WIDTH = 512
ATTN_WIDTH = 512
HEAD_DIM = 64
N_HEADS = ATTN_WIDTH // HEAD_DIM
N_PAIRS = N_HEADS // 2
PAIR_W = 2 * HEAD_DIM
CONV_KERNEL = 31
MAX_REL = 128
D_FF = 2816
FFN_CONV_KERNEL = 3
EPS = 1e-6
NEG_INF = -1e30

TM_MIX = 512
HALO_H = 32
HALO_KV = N_LEFT_CHUNKS * CHUNK
CONV_RB = 32
OUT_RB = 256
PROJ_RB = 256
VT_COLS = 256
SUBLANES = 8
HSHIFT_LEAD = HALO_H - SUBLANES
HSHIFT_ROWS = HALO_H + TM_MIX - SUBLANES
QUAD = 4
QW = QUAD * CHUNK
KW = HALO_KV + QW
N_QUADS = TM_MIX // QW
PIECE = 128
SCORE_ROWS = 384
SOFTMAX_ROWS = 256
HALF_W = 128
FAR_LAG = (HALO_KV - MAX_REL) // CHUNK - 1
LOG2E = 1.4426950408889634
ROLL_W = 1024
TM_FFN = 512
FFN_CW = 256
FFN_SEG = TM_FFN // SUBLANES
FFN_HALO = (FFN_CONV_KERNEL - 1) * SUBLANES
W_IN_CHUNK = 128
W_OUT_CHUNK = 256
W_UP_CHUNK = 128
W_DOWN_CHUNK = 352
VMEM_LIMIT = 56 * 1024 * 1024


def _rms(xf, g):
    return xf * lax.rsqrt(jnp.mean(xf * xf, axis=-1, keepdims=True) + EPS) * g


def _dot(a, b):
    return jnp.dot(a, b, preferred_element_type=jnp.float32)


def _dot_nt(a, b):
    return lax.dot_general(a, b, (((1,), (1,)), ((), ())), preferred_element_type=jnp.float32)


def _build_bias(rel_ref, bias_t):
    r_idx = lax.broadcasted_iota(jnp.int32, (KW, QW), 0) // CHUNK
    c_idx = lax.broadcasted_iota(jnp.int32, (KW, QW), 1) // CHUNK
    visible = (r_idx >= c_idx) & (r_idx <= c_idx + N_LEFT_CHUNKS)
    for h in range(N_HEADS):
        row = jnp.broadcast_to(rel_ref[h:h + 1, :], (QW, ROLL_W))
        nat = pltpu.roll(row, 0, 1, stride=1, stride_axis=0)[:, 0:KW]
        rel_to_far = (nat.T - rel_ref[h:h + 1, 0:1]) * LOG2E
        bias_t[h] = jnp.where(visible, rel_to_far, NEG_INF)


def _block_kind(piece, half):
    lags = [kc - qc
            for kc in range(piece * PIECE // CHUNK, (piece + 1) * PIECE // CHUNK)
            for qc in range(half * HALF_W // CHUNK, (half + 1) * HALF_W // CHUNK)]
    live = any(0 <= lag <= N_LEFT_CHUNKS for lag in lags)
    plain = all(0 <= lag <= FAR_LAG for lag in lags)
    return live, plain


def _first_live(piece, half, lo):
    return not any(_block_kind(t, half)[0] for t in range(lo // PIECE, piece))


def _unit_blocks(piece):
    for head in range(2):
        for half in range(QW // HALF_W):
            live, plain = _block_kind(piece, half)
            if live:
                start = head * QW + half * HALF_W
                yield head, half, plain, slice(start, start + HALF_W)


def _score_steps(pair, quad, lo, qbuf, kbuf, bias_t, s_all, mx_all):
    s_ref, mx_ref = s_all.at[quad], mx_all.at[quad]
    k0 = quad * QW
    lane = lax.broadcasted_iota(jnp.int32, (1, PAIR_W), 1)
    qq = qbuf[pair, k0:k0 + QW, :]
    zero = jnp.zeros_like(qq)
    qm = jnp.concatenate([jnp.where(lane < HEAD_DIM, qq, zero),
                          jnp.where(lane >= HEAD_DIM, qq, zero)], axis=0)

    def step(r0, r1):
        s = _dot_nt(kbuf[pair, k0 + r0:k0 + r1, :], qm)
        running = {}
        for piece in range(r0 // PIECE, r1 // PIECE):
            r = piece * PIECE
            for head, half, plain, lanes in _unit_blocks(piece):
                sh = s[r - r0:r - r0 + PIECE, lanes]
                if not plain:
                    sh = sh + bias_t[2 * pair + head, r:r + PIECE,
                                     half * HALF_W:(half + 1) * HALF_W]
                s_ref[r:r + PIECE, lanes] = sh
                m = jnp.max(sh.reshape(PIECE // SUBLANES, SUBLANES, HALF_W), axis=0)
                key = (lanes.start, half)
                if key in running:
                    running[key] = jnp.maximum(running[key], m)
                elif _first_live(piece, half, lo):
                    running[key] = m
                else:
                    running[key] = jnp.maximum(mx_ref[:, lanes], m)
        for (start, _), m in running.items():
            mx_ref[:, start:start + HALF_W] = m

    starts = range(lo, KW, SCORE_ROWS)
    return [lambda r0=r0: step(r0, min(r0 + SCORE_ROWS, KW)) for r0 in starts]


def _softmax_steps(quad, lo, s_all, mx_all, p_all, sum_all):
    s_ref, mx_ref, p_ref, sum_ref = s_all.at[quad], mx_all.at[quad], p_all.at[quad], sum_all.at[quad]
    mx = jnp.max(mx_ref[...], axis=0, keepdims=True)

    def step(r0, r1):
        running = {}
        for piece in range(r0 // PIECE, r1 // PIECE):
            r = piece * PIECE
            for _, half, _, lanes in _unit_blocks(piece):
                e = jnp.exp2(s_ref[r:r + PIECE, lanes] - mx[:, lanes])
                p_ref[r:r + PIECE, lanes] = e.astype(jnp.bfloat16)
                part = jnp.sum(e.reshape(PIECE // SUBLANES, SUBLANES, HALF_W), axis=0)
                key = (lanes.start, half)
                if key in running:
                    running[key] = running[key] + part
                elif _first_live(piece, half, lo):
                    running[key] = part
                else:
                    running[key] = sum_ref[:, lanes] + part
        for (start, _), part in running.items():
            sum_ref[:, start:start + HALF_W] = part

    starts = range(lo, KW, SOFTMAX_ROWS)
    return [lambda r0=r0: step(r0, min(r0 + SOFTMAX_ROWS, KW)) for r0 in starts]


def _zero_dead_blocks(p_all):
    p_all[...] = jnp.zeros(p_all.shape, jnp.bfloat16)


def _weighted_values(pair, quad, lo, vtbuf, p_all, sum_all, out_t):
    row0 = pl.multiple_of(pair * PAIR_W, PAIR_W)
    k0 = quad * QW
    denom = jnp.sum(sum_all[quad], axis=0, keepdims=True)
    o_t = _dot(vtbuf[pl.ds(row0, PAIR_W), k0 + lo:k0 + KW], p_all[quad, lo:KW, :])
    inv = 1.0 / denom
    for head in range(2):
        rows = slice(head * HEAD_DIM, (head + 1) * HEAD_DIM)
        lanes = slice(head * QW, (head + 1) * QW)
        out_t[pl.ds(row0 + head * HEAD_DIM, HEAD_DIM), k0:k0 + QW] = o_t[rows, lanes] * inv[:, lanes]


def _interleave(a_steps, b_steps):
    done = 0
    for t, a_step in enumerate(a_steps):
        a_step()
        upto = (t + 1) * len(b_steps) // len(a_steps)
        for b_step in b_steps[done:upto]:
            b_step()
        done = upto


def _load_weight_bf16(w_hbm, dst, stage, sem, after_chunk=None):
    chunk_rows = stage.shape[1]
    n_chunks = w_hbm.shape[0] // chunk_rows

    def copy(c):
        rows = slice(c * chunk_rows, (c + 1) * chunk_rows)
        return pltpu.make_async_copy(w_hbm.at[rows, :], stage.at[c % 2], sem.at[c % 2])

    copy(0).start()
    for c in range(n_chunks):
        if c + 1 < n_chunks:
            copy(c + 1).start()
        copy(c).wait()
        rows = slice(c * chunk_rows, (c + 1) * chunk_rows)
        dst[rows, :] = stage[c % 2].astype(jnp.bfloat16)
        if after_chunk is not None:
            after_chunk(c, rows, stage.at[c % 2])


def _alternate(a_steps, b_steps):
    for t in range(max(len(a_steps), len(b_steps))):
        if t < len(a_steps):
            a_steps[t]()
        if t < len(b_steps):
            b_steps[t]()


def _mixer_kernel(x_ref, g_pre_ref, w_in_hbm, dw_w_ref, dw_b_ref,
                  ln_g_ref, ln_b_ref, rel_ref, w_out_hbm, g_post_ref, o_ref,
                  w_in_ref, w_vt_ref, w_out_ref, stage_in, stage_out, w_sem,
                  hbuf, hshift, cbuf, qbuf, kbuf, vtbuf, bias_t, s_ref, mx_ref, p_ref, sum_ref,
                  out_t, mixbuf):
    b = pl.program_id(0)
    i = pl.program_id(1)
    c_v = 2 * CONV_WIDTH + 2 * ATTN_WIDTH

    @pl.when((b == 0) & (i == 0))
    def _():
        def value_weights_transposed(c, rows, staged):
            w_vt_ref[:, rows] = staged[:, c_v:c_v + ATTN_WIDTH].T.astype(jnp.bfloat16)

        _load_weight_bf16(w_in_hbm, w_in_ref, stage_in, w_sem, value_weights_transposed)
        _load_weight_bf16(w_out_hbm, w_out_ref, stage_out, w_sem)
        _build_bias(rel_ref, bias_t)
        _zero_dead_blocks(p_ref)

    @pl.when(i == 0)
    def _():
        hbuf[0:HALO_H, :] = jnp.zeros((HALO_H, CONV_WIDTH), jnp.float32)
        kbuf[:, 0:HALO_KV, :] = jnp.zeros((N_PAIRS, HALO_KV, PAIR_W), jnp.bfloat16)
        vtbuf[:, 0:HALO_KV] = jnp.zeros((ATTN_WIDTH, HALO_KV), jnp.bfloat16)

    c_q = 2 * CONV_WIDTH
    c_k = c_q + ATTN_WIDTH

    def conv_block(base):
        acc = jnp.broadcast_to(dw_b_ref[...][None], (CONV_RB // SUBLANES, SUBLANES, CONV_WIDTH))
        for j in range(CONV_KERNEL):
            off = HALO_H - (CONV_KERNEL - 1) + j
            r = off % SUBLANES
            rows = slice(base + off - r, base + off - r + CONV_RB)
            tap = hbuf[rows, :] if r == 0 else hshift[r - 1, rows, :]
            acc = acc + dw_w_ref[j][None] * tap.reshape(acc.shape)
        cbuf[base:base + CONV_RB, :] = acc.reshape(CONV_RB, CONV_WIDTH)

    pending = []

    def project_steps(blk):
        lo_row, hi_row = blk * PROJ_RB, (blk + 1) * PROJ_RB
        rows = slice(lo_row, hi_row)
        state = {}

        def norm_and_value():
            state["u"] = _rms(x_ref[rows, :], g_pre_ref[...]).astype(jnp.bfloat16)
            state["a_val"] = _dot(state["u"], w_in_ref[:, 0:CONV_WIDTH])

        def gate():
            a_gate = _dot(state["u"], w_in_ref[:, CONV_WIDTH:c_q])
            hbuf[HALO_H + lo_row:HALO_H + hi_row, :] = state["a_val"] * jax.nn.sigmoid(a_gate)

        def queries():
            q = _dot(state["u"], w_in_ref[:, c_q:c_k]) * (HEAD_DIM ** -0.5 * LOG2E)
            q = q.astype(jnp.bfloat16)
            for p in range(N_PAIRS):
                qbuf[p, rows, :] = q[:, p * PAIR_W:(p + 1) * PAIR_W]

        def keys_values():
            k = _dot(state["u"], w_in_ref[:, c_k:c_k + ATTN_WIDTH]).astype(jnp.bfloat16)
            for p in range(N_PAIRS):
                kbuf[p, HALO_KV + lo_row:HALO_KV + hi_row, :] = k[:, p * PAIR_W:(p + 1) * PAIR_W]
            pending.append(state["u"])
            if hi_row % VT_COLS == 0:
                u_wide = pending[0] if len(pending) == 1 else jnp.concatenate(pending, axis=0)
                vtbuf[:, HALO_KV + hi_row - VT_COLS:HALO_KV + hi_row] = (
                    _dot_nt(w_vt_ref[...], u_wide).astype(jnp.bfloat16))
                pending.clear()

        return [norm_and_value, gate, queries, keys_values]

    def mix_steps(blk):
        lo_row, hi_row = blk * PROJ_RB, (blk + 1) * PROJ_RB
        rows = slice(lo_row, hi_row)

        def shifted_copies():
            sh_lo = 0 if blk == 0 else lo_row + HSHIFT_LEAD
            sh_hi = hi_row + HSHIFT_LEAD
            for r in range(1, SUBLANES):
                hshift[r - 1, sh_lo:sh_hi, :] = hbuf[sh_lo + r:sh_hi + r, :]

        def norm_swish():
            c = cbuf[rows, :]
            mu = jnp.mean(c, axis=-1, keepdims=True)
            xc = c - mu
            var = jnp.mean(xc * xc, axis=-1, keepdims=True)
            y = xc * lax.rsqrt(var + EPS) * ln_g_ref[...] + ln_b_ref[...]
            y = y * jax.nn.sigmoid(y)
            mixbuf[rows, 0:CONV_WIDTH] = y.astype(jnp.bfloat16)

        convs = [lambda base=lo_row + rb * CONV_RB: conv_block(base)
                 for rb in range(PROJ_RB // CONV_RB)]
        return [shifted_copies] + convs + [norm_swish]

    n_blocks = TM_MIX // PROJ_RB
    for step in project_steps(0):
        step()
    for blk in range(n_blocks):
        nxt = project_steps(blk + 1) if blk + 1 < n_blocks else []
        _alternate(nxt, mix_steps(blk))

    def attn_loop(first_tile):
        lo = [max(0, HALO_KV - quad * QW) if first_tile else 0 for quad in range(N_QUADS)]

        def scores(pair, quad):
            return _score_steps(pair, quad, lo[quad], qbuf, kbuf, bias_t, s_ref, mx_ref)

        for step in scores(0, 0):
            step()

        def body(pair, carry):
            for quad in range(N_QUADS):
                if quad + 1 < N_QUADS:
                    nxt = scores(pair, quad + 1)
                else:
                    nxt = scores(jnp.minimum(pair + 1, N_PAIRS - 1), 0)
                _interleave(nxt, _softmax_steps(quad, lo[quad], s_ref, mx_ref, p_ref, sum_ref))
                _weighted_values(pair, quad, lo[quad], vtbuf, p_ref, sum_ref, out_t)
            return carry
        lax.fori_loop(0, N_PAIRS, body, 0)

    @pl.when(i == 0)
    def _():
        attn_loop(True)

    @pl.when(i > 0)
    def _():
        attn_loop(False)

    for blk in range(TM_MIX // OUT_RB):
        rows = slice(blk * OUT_RB, (blk + 1) * OUT_RB)
        attn = out_t[:, rows].T.astype(jnp.bfloat16)
        mixed = _dot(jnp.concatenate([mixbuf[rows, 0:CONV_WIDTH], attn], axis=1), w_out_ref[...])
        o_ref[rows, :] = x_ref[rows, :] + _rms(mixed, g_post_ref[...])

    hbuf[0:HALO_H, :] = hbuf[TM_MIX:TM_MIX + HALO_H, :]
    kbuf[:, 0:HALO_KV, :] = kbuf[:, TM_MIX:TM_MIX + HALO_KV, :]
    vtbuf[:, 0:HALO_KV] = vtbuf[:, TM_MIX:TM_MIX + HALO_KV]


def _ffn_kernel(h_hbm, g_pre_ref, w_up_hbm, dw_w_ref, dw_b_ref, w_down_hbm, g_post_ref, o_hbm,
                w_up_ref, w_down_ref, stage_up, stage_down, w_sem,
                xin, xout, sem_in, sem_out, carry, gbuf, vbuf, actbuf):
    i = pl.program_id(1)
    n = pl.program_id(0) * pl.num_programs(1) + i
    n_tiles = pl.num_programs(0) * pl.num_programs(1)
    slot = n % 2

    def in_copy(tile, sl, s):
        return pltpu.make_async_copy(h_hbm.at[tile, s], xin.at[sl, :, s, :], sem_in.at[sl])

    def out_copy(tile, sl, s):
        return pltpu.make_async_copy(xout.at[sl, :, s, :], o_hbm.at[tile, s], sem_out.at[sl])

    @pl.when(n == 0)
    def _():
        for s in range(SUBLANES):
            in_copy(0, 0, s).start()
        _load_weight_bf16(w_up_hbm, w_up_ref, stage_up, w_sem)
        _load_weight_bf16(w_down_hbm, w_down_ref, stage_down, w_sem)

    @pl.when(n + 1 < n_tiles)
    def _():
        for s in range(SUBLANES):
            in_copy(n + 1, 1 - slot, s).start()

    @pl.when(i == 0)
    def _():
        carry[...] = jnp.zeros(carry.shape, jnp.float32)

    for s in range(SUBLANES):
        in_copy(n, slot, s).wait()

    xt = xin[slot].reshape(TM_FFN, D_MODEL)
    u = _rms(xt, g_pre_ref[...]).astype(jnp.bfloat16)
    first_sublane = lax.broadcasted_iota(jnp.int32, (SUBLANES, FFN_CW), 0) == 0

    def up_conv(buf, cols):
        hc = _dot(u, w_up_ref[:, cols])
        for k in (1, 2):
            last = hc[TM_FFN - k * SUBLANES:TM_FFN - (k - 1) * SUBLANES, :]
            prev = carry[(2 - k) * SUBLANES:(3 - k) * SUBLANES, cols]
            buf[(2 - k) * SUBLANES:(3 - k) * SUBLANES, :] = jnp.where(
                first_sublane, pltpu.roll(prev, 1, 0), pltpu.roll(last, 1, 0))
            carry[(2 - k) * SUBLANES:(3 - k) * SUBLANES, cols] = last
        buf[FFN_HALO:FFN_HALO + TM_FFN, :] = hc
        y = dw_b_ref[:, cols] + dw_w_ref[2:3, cols] * hc
        y = y + dw_w_ref[1:2, cols] * buf[SUBLANES:SUBLANES + TM_FFN, :]
        y = y + dw_w_ref[0:1, cols] * buf[0:TM_FFN, :]
        return y

    for c in range(D_FF // FFN_CW):
        gate = up_conv(gbuf.at[c % 2], slice(c * FFN_CW, (c + 1) * FFN_CW))
        val = up_conv(vbuf.at[c % 2], slice(D_FF + c * FFN_CW, D_FF + (c + 1) * FFN_CW))
        actbuf[:, c * FFN_CW:(c + 1) * FFN_CW] = (jax.nn.gelu(gate) * val).astype(jnp.bfloat16)
    f = _dot(actbuf[...], w_down_ref[...])
    y = xt + _rms(f, g_post_ref[...])

    @pl.when(n >= 2)
    def _():
        for s in range(SUBLANES):
            out_copy(n - 2, slot, s).wait()

    xout[slot] = y.reshape(FFN_SEG, SUBLANES, D_MODEL)
    for s in range(SUBLANES):
        out_copy(n, slot, s).start()

    @pl.when(n == n_tiles - 1)
    def _():
        @pl.when(n >= 1)
        def _():
            for s in range(SUBLANES):
                out_copy(n - 1, 1 - slot, s).wait()
        for s in range(SUBLANES):
            out_copy(n, slot, s).wait()


def _rel_distance_row(rel_table):
    h = rel_table.shape[0]
    far = rel_table[:, 2 * MAX_REL:2 * MAX_REL + 1]
    n_far = HALO_KV - MAX_REL + 1
    near = jnp.flip(rel_table, axis=1)[:, 1:]
    n_tail = ROLL_W - n_far - near.shape[1]
    return jnp.concatenate([jnp.broadcast_to(far, (h, n_far)), near,
                            jnp.broadcast_to(far, (h, n_tail))], axis=1)


def _const_spec(shape):
    return pl.BlockSpec(shape, lambda b, i: (0,) * len(shape), pipeline_mode=pl.Buffered(1))


def _mixer(x, g_pre, w_in, dw_w, dw_b, ln_g, ln_b, rel_row, w_out, g_post):
    B, T, D = x.shape
    row_spec = pl.BlockSpec((None, TM_MIX, D), lambda b, i: (b, i, 0))
    hbm_spec = pl.BlockSpec(memory_space=pl.ANY)
    consts = (g_pre, w_in, dw_w, dw_b, ln_g, ln_b, rel_row, w_out, g_post)
    const_specs = [hbm_spec if c is w_in or c is w_out else _const_spec(c.shape) for c in consts]
    return pl.pallas_call(
        _mixer_kernel,
        grid=(B, T // TM_MIX),
        in_specs=[row_spec] + const_specs,
        out_specs=row_spec,
        out_shape=jax.ShapeDtypeStruct(x.shape, x.dtype),
        scratch_shapes=[
            pltpu.VMEM(w_in.shape, jnp.bfloat16),
            pltpu.VMEM((ATTN_WIDTH, D), jnp.bfloat16),
            pltpu.VMEM(w_out.shape, jnp.bfloat16),
            pltpu.VMEM((2, W_IN_CHUNK, w_in.shape[1]), jnp.float32),
            pltpu.VMEM((2, W_OUT_CHUNK, w_out.shape[1]), jnp.float32),
            pltpu.SemaphoreType.DMA((2,)),
            pltpu.VMEM((HALO_H + TM_MIX, CONV_WIDTH), jnp.float32),
            pltpu.VMEM((SUBLANES - 1, HSHIFT_ROWS, CONV_WIDTH), jnp.float32),
            pltpu.VMEM((TM_MIX, CONV_WIDTH), jnp.float32),
            pltpu.VMEM((N_PAIRS, TM_MIX, PAIR_W), jnp.bfloat16),
            pltpu.VMEM((N_PAIRS, HALO_KV + TM_MIX, PAIR_W), jnp.bfloat16),
            pltpu.VMEM((ATTN_WIDTH, HALO_KV + TM_MIX), jnp.bfloat16),
            pltpu.VMEM((N_HEADS, KW, QW), jnp.float32),
            pltpu.VMEM((N_QUADS, KW, 2 * QW), jnp.float32),
            pltpu.VMEM((N_QUADS, SUBLANES, 2 * QW), jnp.float32),
            pltpu.VMEM((N_QUADS, KW, 2 * QW), jnp.bfloat16),
            pltpu.VMEM((N_QUADS, SUBLANES, 2 * QW), jnp.float32),
            pltpu.VMEM((ATTN_WIDTH, TM_MIX), jnp.float32),
            pltpu.VMEM((TM_MIX, CONV_WIDTH + ATTN_WIDTH), jnp.bfloat16),
        ],
        compiler_params=pltpu.CompilerParams(
            dimension_semantics=("arbitrary", "arbitrary"), vmem_limit_bytes=VMEM_LIMIT),
        name="mixer",
    )(x, *consts)


def _ffn(h, g_pre, w_up, dw_w, dw_b, w_down, g_post):
    B, T, D = h.shape
    n_t = T // TM_FFN
    tiles = (B * n_t, SUBLANES, FFN_SEG, D)
    hbm_spec = pl.BlockSpec(memory_space=pl.ANY)
    out = pl.pallas_call(
        _ffn_kernel,
        grid=(B, n_t),
        in_specs=[hbm_spec,
                  _const_spec((1, D)), hbm_spec, _const_spec(dw_w.shape),
                  _const_spec((1, 2 * D_FF)), hbm_spec, _const_spec((1, D))],
        out_specs=hbm_spec,
        out_shape=jax.ShapeDtypeStruct(tiles, h.dtype),
        scratch_shapes=[
            pltpu.VMEM(w_up.shape, jnp.bfloat16),
            pltpu.VMEM(w_down.shape, jnp.bfloat16),
            pltpu.VMEM((2, W_UP_CHUNK, w_up.shape[1]), jnp.float32),
            pltpu.VMEM((2, W_DOWN_CHUNK, w_down.shape[1]), jnp.float32),
            pltpu.SemaphoreType.DMA((2,)),
            pltpu.VMEM((2, FFN_SEG, SUBLANES, D), jnp.float32),
            pltpu.VMEM((2, FFN_SEG, SUBLANES, D), jnp.float32),
            pltpu.SemaphoreType.DMA((2,)),
            pltpu.SemaphoreType.DMA((2,)),
            pltpu.VMEM((FFN_HALO, 2 * D_FF), jnp.float32),
            pltpu.VMEM((2, FFN_HALO + TM_FFN, FFN_CW), jnp.float32),
            pltpu.VMEM((2, FFN_HALO + TM_FFN, FFN_CW), jnp.float32),
            pltpu.VMEM((TM_FFN, D_FF), jnp.bfloat16),
        ],
        compiler_params=pltpu.CompilerParams(
            dimension_semantics=("arbitrary", "arbitrary"), vmem_limit_bytes=VMEM_LIMIT),
        name="ffn",
    )(h.reshape(tiles), g_pre, w_up, dw_w, dw_b, w_down, g_post)
    return out.reshape(B, T, D)


def kernel(x, norm_mix_pre, w_in, conv_dw_w, conv_dw_b, conv_ln_g, conv_ln_b, rel_bias, w_out,
           norm_mix_post, norm_ffn_pre, w_up, ffn_dw_w, ffn_dw_b, w_down, norm_ffn_post):
    h = x
    for l in range(norm_mix_pre.shape[0]):
        h = _mixer(h, norm_mix_pre[l][None], w_in[l],
                   jnp.broadcast_to(conv_dw_w[l][:, None, :], (CONV_KERNEL, SUBLANES, CONV_WIDTH)),
                   conv_dw_b[l][None], conv_ln_g[l][None], conv_ln_b[l][None],
                   _rel_distance_row(rel_bias[l]), w_out[l], norm_mix_post[l][None])
        h = _ffn(h, norm_ffn_pre[l][None], w_up[l], ffn_dw_w[l],
                 ffn_dw_b[l][None], w_down[l], norm_ffn_post[l][None])
    return h
```

```python
import jax
import jax.numpy as jnp
from jax import lax
from jax.experimental import pallas as pl
from jax.experimental.pallas import tpu as pltpu

D_MODEL = 1024
CHUNK = 64
N_LEFT_CHUNKS = 8
CONV_WIDTH = 512
ATTN_WIDTH = 512
HEAD_DIM = 64
N_HEADS = ATTN_WIDTH // HEAD_DIM
N_PAIRS = N_HEADS // 2
PAIR_W = 2 * HEAD_DIM
CONV_KERNEL = 31
MAX_REL = 128
D_FF = 2816
FFN_CONV_KERNEL = 3
EPS = 1e-6
NEG_INF = -1e30

TM_MIX = 512
HALO_H = 32
HALO_KV = N_LEFT_CHUNKS * CHUNK
CONV_RB = 32
OUT_RB = 256
PROJ_RB = 256
VT_COLS = 256
SUBLANES = 8
HSHIFT_LEAD = HALO_H - SUBLANES
HSHIFT_ROWS = HALO_H + TM_MIX - SUBLANES
QUAD = 4
QW = QUAD * CHUNK
KW = HALO_KV + QW
N_QUADS = TM_MIX // QW
PIECE = 128
SCORE_ROWS = 384
SOFTMAX_ROWS = 256
HALF_W = 128
FAR_LAG = (HALO_KV - MAX_REL) // CHUNK - 1
LOG2E = 1.4426950408889634
ROLL_W = 1024
TM_FFN = 512
FFN_CW = 256
FFN_DOWN_RB = 256
FFN_SEG = TM_FFN // SUBLANES
FFN_HALO = (FFN_CONV_KERNEL - 1) * SUBLANES
W_IN_CHUNK = 128
W_OUT_CHUNK = 256
W_UP_CHUNK = 128
W_DOWN_CHUNK = 352
VMEM_LIMIT = 56 * 1024 * 1024


def _rms(xf, g):
    return xf * lax.rsqrt(jnp.mean(xf * xf, axis=-1, keepdims=True) + EPS) * g


def _dot(a, b):
    return jnp.dot(a, b, preferred_element_type=jnp.float32)


def _dot_nt(a, b):
    return lax.dot_general(a, b, (((1,), (1,)), ((), ())), preferred_element_type=jnp.float32)


def _build_bias(rel_ref, bias_t):
    r_idx = lax.broadcasted_iota(jnp.int32, (KW, QW), 0) // CHUNK
    c_idx = lax.broadcasted_iota(jnp.int32, (KW, QW), 1) // CHUNK
    visible = (r_idx >= c_idx) & (r_idx <= c_idx + N_LEFT_CHUNKS)
    for h in range(N_HEADS):
        row = jnp.broadcast_to(rel_ref[h:h + 1, :], (QW, ROLL_W))
        nat = pltpu.roll(row, 0, 1, stride=1, stride_axis=0)[:, 0:KW]
        rel_to_far = (nat.T - rel_ref[h:h + 1, 0:1]) * LOG2E
        bias_t[h] = jnp.where(visible, rel_to_far, NEG_INF)


def _block_kind(piece, half):
    lags = [kc - qc
            for kc in range(piece * PIECE // CHUNK, (piece + 1) * PIECE // CHUNK)
            for qc in range(half * HALF_W // CHUNK, (half + 1) * HALF_W // CHUNK)]
    live = any(0 <= lag <= N_LEFT_CHUNKS for lag in lags)
    plain = all(0 <= lag <= FAR_LAG for lag in lags)
    return live, plain


def _first_live(piece, half, lo):
    return not any(_block_kind(t, half)[0] for t in range(lo // PIECE, piece))


def _unit_blocks(piece):
    for head in range(2):
        for half in range(QW // HALF_W):
            live, plain = _block_kind(piece, half)
            if live:
                start = head * QW + half * HALF_W
                yield head, half, plain, slice(start, start + HALF_W)


def _score_steps(pair, quad, lo, qbuf, kbuf, bias_t, s_all, mx_all):
    s_ref, mx_ref = s_all.at[quad], mx_all.at[quad]
    k0 = quad * QW
    lane = lax.broadcasted_iota(jnp.int32, (1, PAIR_W), 1)
    qq = qbuf[pair, k0:k0 + QW, :]
    zero = jnp.zeros_like(qq)
    qm = jnp.concatenate([jnp.where(lane < HEAD_DIM, qq, zero),
                          jnp.where(lane >= HEAD_DIM, qq, zero)], axis=0)

    def step(r0, r1):
        s = _dot_nt(kbuf[pair, k0 + r0:k0 + r1, :], qm)
        running = {}
        for piece in range(r0 // PIECE, r1 // PIECE):
            r = piece * PIECE
            for head, half, plain, lanes in _unit_blocks(piece):
                sh = s[r - r0:r - r0 + PIECE, lanes]
                if not plain:
                    sh = sh + bias_t[2 * pair + head, r:r + PIECE,
                                     half * HALF_W:(half + 1) * HALF_W]
                s_ref[r:r + PIECE, lanes] = sh
                m = jnp.max(sh.reshape(PIECE // SUBLANES, SUBLANES, HALF_W), axis=0)
                key = (lanes.start, half)
                if key in running:
                    running[key] = jnp.maximum(running[key], m)
                elif _first_live(piece, half, lo):
                    running[key] = m
                else:
                    running[key] = jnp.maximum(mx_ref[:, lanes], m)
        for (start, _), m in running.items():
            mx_ref[:, start:start + HALF_W] = m

    starts = range(lo, KW, SCORE_ROWS)
    return [lambda r0=r0: step(r0, min(r0 + SCORE_ROWS, KW)) for r0 in starts]


def _softmax_steps(quad, lo, s_all, mx_all, p_all, sum_all):
    s_ref, mx_ref, p_ref, sum_ref = s_all.at[quad], mx_all.at[quad], p_all.at[quad], sum_all.at[quad]
    mx = jnp.max(mx_ref[...], axis=0, keepdims=True)

    def step(r0, r1):
        running = {}
        for piece in range(r0 // PIECE, r1 // PIECE):
            r = piece * PIECE
            for _, half, _, lanes in _unit_blocks(piece):
                e = jnp.exp2(s_ref[r:r + PIECE, lanes] - mx[:, lanes])
                p_ref[r:r + PIECE, lanes] = e.astype(jnp.bfloat16)
                part = jnp.sum(e.reshape(PIECE // SUBLANES, SUBLANES, HALF_W), axis=0)
                key = (lanes.start, half)
                if key in running:
                    running[key] = running[key] + part
                elif _first_live(piece, half, lo):
                    running[key] = part
                else:
                    running[key] = sum_ref[:, lanes] + part
        for (start, _), part in running.items():
            sum_ref[:, start:start + HALF_W] = part

    starts = range(lo, KW, SOFTMAX_ROWS)
    return [lambda r0=r0: step(r0, min(r0 + SOFTMAX_ROWS, KW)) for r0 in starts]


def _zero_dead_blocks(p_all):
    p_all[...] = jnp.zeros(p_all.shape, jnp.bfloat16)


def _weighted_values(pair, quad, lo, vtbuf, p_all, sum_all, out_t):
    row0 = pl.multiple_of(pair * PAIR_W, PAIR_W)
    k0 = quad * QW
    denom = jnp.sum(sum_all[quad], axis=0, keepdims=True)
    o_t = _dot(vtbuf[pl.ds(row0, PAIR_W), k0 + lo:k0 + KW], p_all[quad, lo:KW, :])
    inv = 1.0 / denom
    for head in range(2):
        rows = slice(head * HEAD_DIM, (head + 1) * HEAD_DIM)
        lanes = slice(head * QW, (head + 1) * QW)
        out_t[pl.ds(row0 + head * HEAD_DIM, HEAD_DIM), k0:k0 + QW] = o_t[rows, lanes] * inv[:, lanes]


def _interleave(a_steps, b_steps):
    done = 0
    for t, a_step in enumerate(a_steps):
        a_step()
        upto = (t + 1) * len(b_steps) // len(a_steps)
        for b_step in b_steps[done:upto]:
            b_step()
        done = upto


def _load_weight_bf16(w_hbm, dst, stage, sem, after_chunk=None):
    chunk_rows = stage.shape[1]
    n_chunks = w_hbm.shape[0] // chunk_rows

    def copy(c):
        rows = slice(c * chunk_rows, (c + 1) * chunk_rows)
        return pltpu.make_async_copy(w_hbm.at[rows, :], stage.at[c % 2], sem.at[c % 2])

    copy(0).start()
    for c in range(n_chunks):
        if c + 1 < n_chunks:
            copy(c + 1).start()
        copy(c).wait()
        rows = slice(c * chunk_rows, (c + 1) * chunk_rows)
        dst[rows, :] = stage[c % 2].astype(jnp.bfloat16)
        if after_chunk is not None:
            after_chunk(c, rows, stage.at[c % 2])


def _alternate(a_steps, b_steps):
    for t in range(max(len(a_steps), len(b_steps))):
        if t < len(a_steps):
            a_steps[t]()
        if t < len(b_steps):
            b_steps[t]()


def _mixer_kernel(x_ref, g_pre_ref, w_in_hbm, dw_w_ref, dw_b_ref,
                  ln_g_ref, ln_b_ref, rel_ref, w_out_hbm, g_post_ref, o_ref,
                  w_in_ref, w_vt_ref, w_out_ref, stage_in, stage_out, w_sem,
                  hbuf, hshift, cbuf, qbuf, kbuf, vtbuf, bias_t, s_ref, mx_ref, p_ref, sum_ref,
                  out_t, mixbuf):
    b = pl.program_id(0)
    i = pl.program_id(1)
    c_v = 2 * CONV_WIDTH + 2 * ATTN_WIDTH

    @pl.when((b == 0) & (i == 0))
    def _():
        def value_weights_transposed(c, rows, staged):
            w_vt_ref[:, rows] = staged[:, c_v:c_v + ATTN_WIDTH].T.astype(jnp.bfloat16)

        _load_weight_bf16(w_in_hbm, w_in_ref, stage_in, w_sem, value_weights_transposed)
        _load_weight_bf16(w_out_hbm, w_out_ref, stage_out, w_sem)
        _build_bias(rel_ref, bias_t)
        _zero_dead_blocks(p_ref)

    @pl.when(i == 0)
    def _():
        hbuf[0:HALO_H, :] = jnp.zeros((HALO_H, CONV_WIDTH), jnp.float32)
        kbuf[:, 0:HALO_KV, :] = jnp.zeros((N_PAIRS, HALO_KV, PAIR_W), jnp.bfloat16)
        vtbuf[:, 0:HALO_KV] = jnp.zeros((ATTN_WIDTH, HALO_KV), jnp.bfloat16)

    c_q = 2 * CONV_WIDTH
    c_k = c_q + ATTN_WIDTH

    def conv_block(base):
        acc = jnp.broadcast_to(dw_b_ref[...][None], (CONV_RB // SUBLANES, SUBLANES, CONV_WIDTH))
        for j in range(CONV_KERNEL):
            off = HALO_H - (CONV_KERNEL - 1) + j
            r = off % SUBLANES
            rows = slice(base + off - r, base + off - r + CONV_RB)
            tap = hbuf[rows, :] if r == 0 else hshift[r - 1, rows, :]
            acc = acc + dw_w_ref[j][None] * tap.reshape(acc.shape)
        cbuf[base:base + CONV_RB, :] = acc.reshape(CONV_RB, CONV_WIDTH)

    pending = []

    def project_steps(blk):
        lo_row, hi_row = blk * PROJ_RB, (blk + 1) * PROJ_RB
        rows = slice(lo_row, hi_row)
        state = {}

        def norm_and_value():
            state["u"] = _rms(x_ref[rows, :], g_pre_ref[...]).astype(jnp.bfloat16)
            state["a_val"] = _dot(state["u"], w_in_ref[:, 0:CONV_WIDTH])

        def gate():
            a_gate = _dot(state["u"], w_in_ref[:, CONV_WIDTH:c_q])
            hbuf[HALO_H + lo_row:HALO_H + hi_row, :] = state["a_val"] * jax.nn.sigmoid(a_gate)

        def queries():
            q = _dot(state["u"], w_in_ref[:, c_q:c_k]) * (HEAD_DIM ** -0.5 * LOG2E)
            q = q.astype(jnp.bfloat16)
            for p in range(N_PAIRS):
                qbuf[p, rows, :] = q[:, p * PAIR_W:(p + 1) * PAIR_W]

        def keys_values():
            k = _dot(state["u"], w_in_ref[:, c_k:c_k + ATTN_WIDTH]).astype(jnp.bfloat16)
            for p in range(N_PAIRS):
                kbuf[p, HALO_KV + lo_row:HALO_KV + hi_row, :] = k[:, p * PAIR_W:(p + 1) * PAIR_W]
            pending.append(state["u"])
            if hi_row % VT_COLS == 0:
                u_wide = pending[0] if len(pending) == 1 else jnp.concatenate(pending, axis=0)
                vtbuf[:, HALO_KV + hi_row - VT_COLS:HALO_KV + hi_row] = (
                    _dot_nt(w_vt_ref[...], u_wide).astype(jnp.bfloat16))
                pending.clear()

        return [norm_and_value, gate, queries, keys_values]

    def mix_steps(blk):
        lo_row, hi_row = blk * PROJ_RB, (blk + 1) * PROJ_RB
        rows = slice(lo_row, hi_row)

        def shifted_copies():
            sh_lo = 0 if blk == 0 else lo_row + HSHIFT_LEAD
            sh_hi = hi_row + HSHIFT_LEAD
            for r in range(1, SUBLANES):
                hshift[r - 1, sh_lo:sh_hi, :] = hbuf[sh_lo + r:sh_hi + r, :]

        def norm_swish():
            c = cbuf[rows, :]
            mu = jnp.mean(c, axis=-1, keepdims=True)
            xc = c - mu
            var = jnp.mean(xc * xc, axis=-1, keepdims=True)
            y = xc * lax.rsqrt(var + EPS) * ln_g_ref[...] + ln_b_ref[...]
            y = y * jax.nn.sigmoid(y)
            mixbuf[rows, 0:CONV_WIDTH] = y.astype(jnp.bfloat16)

        convs = [lambda base=lo_row + rb * CONV_RB: conv_block(base)
                 for rb in range(PROJ_RB // CONV_RB)]
        return [shifted_copies] + convs + [norm_swish]

    n_blocks = TM_MIX // PROJ_RB
    for step in project_steps(0):
        step()
    for blk in range(n_blocks):
        nxt = project_steps(blk + 1) if blk + 1 < n_blocks else []
        _alternate(nxt, mix_steps(blk))

    def attn_loop(first_tile):
        lo = [max(0, HALO_KV - quad * QW) if first_tile else 0 for quad in range(N_QUADS)]

        def scores(pair, quad):
            return _score_steps(pair, quad, lo[quad], qbuf, kbuf, bias_t, s_ref, mx_ref)

        for step in scores(0, 0):
            step()

        def body(pair, carry):
            for quad in range(N_QUADS):
                if quad + 1 < N_QUADS:
                    nxt = scores(pair, quad + 1)
                else:
                    nxt = scores(jnp.minimum(pair + 1, N_PAIRS - 1), 0)
                _interleave(nxt, _softmax_steps(quad, lo[quad], s_ref, mx_ref, p_ref, sum_ref))
                _weighted_values(pair, quad, lo[quad], vtbuf, p_ref, sum_ref, out_t)
            return carry
        lax.fori_loop(0, N_PAIRS, body, 0)

    @pl.when(i == 0)
    def _():
        attn_loop(True)

    @pl.when(i > 0)
    def _():
        attn_loop(False)

    for blk in range(TM_MIX // OUT_RB):
        rows = slice(blk * OUT_RB, (blk + 1) * OUT_RB)
        attn = out_t[:, rows].T.astype(jnp.bfloat16)
        mixed = _dot(jnp.concatenate([mixbuf[rows, 0:CONV_WIDTH], attn], axis=1), w_out_ref[...])
        o_ref[rows, :] = x_ref[rows, :] + _rms(mixed, g_post_ref[...])

    hbuf[0:HALO_H, :] = hbuf[TM_MIX:TM_MIX + HALO_H, :]
    kbuf[:, 0:HALO_KV, :] = kbuf[:, TM_MIX:TM_MIX + HALO_KV, :]
    vtbuf[:, 0:HALO_KV] = vtbuf[:, TM_MIX:TM_MIX + HALO_KV]


def _ffn_kernel(h_hbm, g_pre_ref, w_up_hbm, dw_w_ref, dw_b_ref, w_down_hbm, g_post_ref, o_hbm,
                w_up_ref, w_down_ref, stage_up, stage_down, w_sem,
                xin, xout, sem_in, sem_out, carry, gbuf, vbuf, actbuf):
    i = pl.program_id(1)
    n = pl.program_id(0) * pl.num_programs(1) + i
    n_tiles = pl.num_programs(0) * pl.num_programs(1)
    slot = n % 2

    def in_copy(tile, sl, s):
        return pltpu.make_async_copy(h_hbm.at[tile, s], xin.at[sl, :, s, :], sem_in.at[sl])

    def out_copy(tile, sl, s):
        return pltpu.make_async_copy(xout.at[sl, :, s, :], o_hbm.at[tile, s], sem_out.at[sl])

    @pl.when(n == 0)
    def _():
        for s in range(SUBLANES):
            in_copy(0, 0, s).start()
        _load_weight_bf16(w_up_hbm, w_up_ref, stage_up, w_sem)
        _load_weight_bf16(w_down_hbm, w_down_ref, stage_down, w_sem)

    @pl.when(n + 1 < n_tiles)
    def _():
        for s in range(SUBLANES):
            in_copy(n + 1, 1 - slot, s).start()

    @pl.when(i == 0)
    def _():
        carry[...] = jnp.zeros(carry.shape, jnp.float32)

    @pl.when(n >= 2)
    def _():
        for s in range(SUBLANES):
            out_copy(n - 2, slot, s).wait()

    for s in range(SUBLANES):
        in_copy(n, slot, s).wait()

    xt = xin[slot].reshape(TM_FFN, D_MODEL)
    u = _rms(xt, g_pre_ref[...]).astype(jnp.bfloat16)
    first_sublane = lax.broadcasted_iota(jnp.int32, (SUBLANES, FFN_CW), 0) == 0

    def up_conv(buf, cols):
        hc = _dot(u, w_up_ref[:, cols])
        for k in (1, 2):
            last = hc[TM_FFN - k * SUBLANES:TM_FFN - (k - 1) * SUBLANES, :]
            prev = carry[(2 - k) * SUBLANES:(3 - k) * SUBLANES, cols]
            buf[(2 - k) * SUBLANES:(3 - k) * SUBLANES, :] = jnp.where(
                first_sublane, pltpu.roll(prev, 1, 0), pltpu.roll(last, 1, 0))
            carry[(2 - k) * SUBLANES:(3 - k) * SUBLANES, cols] = last
        buf[FFN_HALO:FFN_HALO + TM_FFN, :] = hc
        y = dw_b_ref[:, cols] + dw_w_ref[2:3, cols] * hc
        y = y + dw_w_ref[1:2, cols] * buf[SUBLANES:SUBLANES + TM_FFN, :]
        y = y + dw_w_ref[0:1, cols] * buf[0:TM_FFN, :]
        return y

    for c in range(D_FF // FFN_CW):
        gate = up_conv(gbuf.at[c % 2], slice(c * FFN_CW, (c + 1) * FFN_CW))
        val = up_conv(vbuf.at[c % 2], slice(D_FF + c * FFN_CW, D_FF + (c + 1) * FFN_CW))
        actbuf[:, c * FFN_CW:(c + 1) * FFN_CW] = (jax.nn.gelu(gate) * val).astype(jnp.bfloat16)
    for blk in range(TM_FFN // FFN_DOWN_RB):
        rows = slice(blk * FFN_DOWN_RB, (blk + 1) * FFN_DOWN_RB)
        slabs = slice(blk * FFN_DOWN_RB // SUBLANES, (blk + 1) * FFN_DOWN_RB // SUBLANES)
        f = _dot(actbuf[rows, :], w_down_ref[...])
        y = xt[rows, :] + _rms(f, g_post_ref[...])
        xout[slot, slabs] = y.reshape(FFN_DOWN_RB // SUBLANES, SUBLANES, D_MODEL)
    for s in range(SUBLANES):
        out_copy(n, slot, s).start()

    @pl.when(n == n_tiles - 1)
    def _():
        @pl.when(n >= 1)
        def _():
            for s in range(SUBLANES):
                out_copy(n - 1, 1 - slot, s).wait()
        for s in range(SUBLANES):
            out_copy(n, slot, s).wait()


def _rel_distance_row(rel_table):
    h = rel_table.shape[0]
    far = rel_table[:, 2 * MAX_REL:2 * MAX_REL + 1]
    n_far = HALO_KV - MAX_REL + 1
    near = jnp.flip(rel_table, axis=1)[:, 1:]
    n_tail = ROLL_W - n_far - near.shape[1]
    return jnp.concatenate([jnp.broadcast_to(far, (h, n_far)), near,
                            jnp.broadcast_to(far, (h, n_tail))], axis=1)


def _const_spec(shape):
    return pl.BlockSpec(shape, lambda b, i: (0,) * len(shape), pipeline_mode=pl.Buffered(1))


def _mixer(x, g_pre, w_in, dw_w, dw_b, ln_g, ln_b, rel_row, w_out, g_post):
    B, T, D = x.shape
    row_spec = pl.BlockSpec((None, TM_MIX, D), lambda b, i: (b, i, 0))
    hbm_spec = pl.BlockSpec(memory_space=pl.ANY)
    consts = (g_pre, w_in, dw_w, dw_b, ln_g, ln_b, rel_row, w_out, g_post)
    const_specs = [hbm_spec if c is w_in or c is w_out else _const_spec(c.shape) for c in consts]
    return pl.pallas_call(
        _mixer_kernel,
        grid=(B, T // TM_MIX),
        in_specs=[row_spec] + const_specs,
        out_specs=row_spec,
        out_shape=jax.ShapeDtypeStruct(x.shape, x.dtype),
        scratch_shapes=[
            pltpu.VMEM(w_in.shape, jnp.bfloat16),
            pltpu.VMEM((ATTN_WIDTH, D), jnp.bfloat16),
            pltpu.VMEM(w_out.shape, jnp.bfloat16),
            pltpu.VMEM((2, W_IN_CHUNK, w_in.shape[1]), jnp.float32),
            pltpu.VMEM((2, W_OUT_CHUNK, w_out.shape[1]), jnp.float32),
            pltpu.SemaphoreType.DMA((2,)),
            pltpu.VMEM((HALO_H + TM_MIX, CONV_WIDTH), jnp.float32),
            pltpu.VMEM((SUBLANES - 1, HSHIFT_ROWS, CONV_WIDTH), jnp.float32),
            pltpu.VMEM((TM_MIX, CONV_WIDTH), jnp.float32),
            pltpu.VMEM((N_PAIRS, TM_MIX, PAIR_W), jnp.bfloat16),
            pltpu.VMEM((N_PAIRS, HALO_KV + TM_MIX, PAIR_W), jnp.bfloat16),
            pltpu.VMEM((ATTN_WIDTH, HALO_KV + TM_MIX), jnp.bfloat16),
            pltpu.VMEM((N_HEADS, KW, QW), jnp.float32),
            pltpu.VMEM((N_QUADS, KW, 2 * QW), jnp.float32),
            pltpu.VMEM((N_QUADS, SUBLANES, 2 * QW), jnp.float32),
            pltpu.VMEM((N_QUADS, KW, 2 * QW), jnp.bfloat16),
            pltpu.VMEM((N_QUADS, SUBLANES, 2 * QW), jnp.float32),
            pltpu.VMEM((ATTN_WIDTH, TM_MIX), jnp.float32),
            pltpu.VMEM((TM_MIX, CONV_WIDTH + ATTN_WIDTH), jnp.bfloat16),
        ],
        compiler_params=pltpu.CompilerParams(
            dimension_semantics=("arbitrary", "arbitrary"), vmem_limit_bytes=VMEM_LIMIT),
        name="mixer",
    )(x, *consts)


def _ffn(h, g_pre, w_up, dw_w, dw_b, w_down, g_post):
    B, T, D = h.shape
    n_t = T // TM_FFN
    tiles = (B * n_t, SUBLANES, FFN_SEG, D)
    hbm_spec = pl.BlockSpec(memory_space=pl.ANY)
    out = pl.pallas_call(
        _ffn_kernel,
        grid=(B, n_t),
        in_specs=[hbm_spec,
                  _const_spec((1, D)), hbm_spec, _const_spec(dw_w.shape),
                  _const_spec((1, 2 * D_FF)), hbm_spec, _const_spec((1, D))],
        out_specs=hbm_spec,
        out_shape=jax.ShapeDtypeStruct(tiles, h.dtype),
        scratch_shapes=[
            pltpu.VMEM(w_up.shape, jnp.bfloat16),
            pltpu.VMEM(w_down.shape, jnp.bfloat16),
            pltpu.VMEM((2, W_UP_CHUNK, w_up.shape[1]), jnp.float32),
            pltpu.VMEM((2, W_DOWN_CHUNK, w_down.shape[1]), jnp.float32),
            pltpu.SemaphoreType.DMA((2,)),
            pltpu.VMEM((2, FFN_SEG, SUBLANES, D), jnp.float32),
            pltpu.VMEM((2, FFN_SEG, SUBLANES, D), jnp.float32),
            pltpu.SemaphoreType.DMA((2,)),
            pltpu.SemaphoreType.DMA((2,)),
            pltpu.VMEM((FFN_HALO, 2 * D_FF), jnp.float32),
            pltpu.VMEM((2, FFN_HALO + TM_FFN, FFN_CW), jnp.float32),
            pltpu.VMEM((2, FFN_HALO + TM_FFN, FFN_CW), jnp.float32),
            pltpu.VMEM((TM_FFN, D_FF), jnp.bfloat16),
        ],
        compiler_params=pltpu.CompilerParams(
            dimension_semantics=("arbitrary", "arbitrary"), vmem_limit_bytes=VMEM_LIMIT),
        name="ffn",
    )(h.reshape(tiles), g_pre, w_up, dw_w, dw_b, w_down, g_post)
    return out.reshape(B, T, D)


def kernel(x, norm_mix_pre, w_in, conv_dw_w, conv_dw_b, conv_ln_g, conv_ln_b, rel_bias, w_out,
           norm_mix_post, norm_ffn_pre, w_up, ffn_dw_w, ffn_dw_b, w_down, norm_ffn_post):
    h = x
    for l in range(norm_mix_pre.shape[0]):
        h = _mixer(h, norm_mix_pre[l][None], w_in[l],
                   jnp.broadcast_to(conv_dw_w[l][:, None, :], (CONV_KERNEL, SUBLANES, CONV_WIDTH)),
                   conv_dw_b[l][None], conv_ln_g[l][None], conv_ln_b[l][None],
                   _rel_distance_row(rel_bias[l]), w_out[l], norm_mix_post[l][None])
        h = _ffn(h, norm_ffn_pre[l][None], w_up[l], ffn_dw_w[l],
                 ffn_dw_b[l][None], w_down[l], norm_ffn_post[l][None])
    return h
```

```python
import jax
import jax.numpy as jnp
from jax import lax
from jax.experimental import pallas as pl
from jax.experimental.pallas import tpu as pltpu

D_MODEL = 1024
CHUNK = 64
N_LEFT_CHUNKS = 8
CONV_WIDTH = 512
ATTN_WIDTH = 512
HEAD_DIM = 64
N_HEADS = ATTN_WIDTH // HEAD_DIM
N_PAIRS = N_HEADS // 2
PAIR_W = 2 * HEAD_DIM
CONV_KERNEL = 31
MAX_REL = 128
D_FF = 2816
FFN_CONV_KERNEL = 3
EPS = 1e-6
NEG_INF = -1e30

TM_MIX = 512
HALO_H = 32
HALO_KV = N_LEFT_CHUNKS * CHUNK
CONV_RB = 32
OUT_RB = 256
PROJ_RB = 256
SUBLANES = 8
HSHIFT_LEAD = HALO_H - SUBLANES
HSHIFT_ROWS = HALO_H + TM_MIX - SUBLANES
QUAD = 4
QW = QUAD * CHUNK
KW = HALO_KV + QW
N_QUADS = TM_MIX // QW
PIECE = 128
SCORE_ROWS = 384
SOFTMAX_ROWS = 256
HALF_W = 128
FAR_LAG = (HALO_KV - MAX_REL) // CHUNK - 1
LOG2E = 1.4426950408889634
ROLL_W = 1024
TM_FFN = 512
FFN_CW = 256
FFN_DOWN_RB = 256
FFN_SEG = TM_FFN // SUBLANES
FFN_HALO = (FFN_CONV_KERNEL - 1) * SUBLANES
W_IN_CHUNK = 128
W_OUT_CHUNK = 256
W_UP_CHUNK = 128
W_DOWN_CHUNK = 352
VMEM_LIMIT = 56 * 1024 * 1024


def _rms(xf, g):
    return xf * lax.rsqrt(jnp.mean(xf * xf, axis=-1, keepdims=True) + EPS) * g


def _dot(a, b):
    return jnp.dot(a, b, preferred_element_type=jnp.float32)


def _build_bias(rel_ref, bias_t):
    r_idx = lax.broadcasted_iota(jnp.int32, (KW, QW), 0) // CHUNK
    c_idx = lax.broadcasted_iota(jnp.int32, (KW, QW), 1) // CHUNK
    visible = (r_idx >= c_idx) & (r_idx <= c_idx + N_LEFT_CHUNKS)
    for h in range(N_HEADS):
        row = jnp.broadcast_to(rel_ref[h:h + 1, :], (QW, ROLL_W))
        nat = pltpu.roll(row, 0, 1, stride=1, stride_axis=0)[:, 0:KW]
        rel_to_far = (nat.T - rel_ref[h:h + 1, 0:1]) * LOG2E
        bias_t[h] = jnp.where(visible, rel_to_far, NEG_INF)


def _block_kind(piece, half):
    lags = [kc - qc
            for kc in range(piece * PIECE // CHUNK, (piece + 1) * PIECE // CHUNK)
            for qc in range(half * HALF_W // CHUNK, (half + 1) * HALF_W // CHUNK)]
    live = any(0 <= lag <= N_LEFT_CHUNKS for lag in lags)
    plain = all(0 <= lag <= FAR_LAG for lag in lags)
    return live, plain


def _first_live(piece, half, lo):
    return not any(_block_kind(t, half)[0] for t in range(lo // PIECE, piece))


def _unit_blocks(piece):
    for head in range(2):
        for half in range(QW // HALF_W):
            live, plain = _block_kind(piece, half)
            if live:
                start = head * QW + half * HALF_W
                yield head, half, plain, slice(start, start + HALF_W)


def _score_steps(pair, quad, lo, qbuf, kbuf, bias_t, s_all, mx_all):
    s_ref, mx_ref = s_all.at[quad], mx_all.at[quad]
    k0 = quad * QW
    qt = qbuf[pair, :, k0:k0 + QW]
    zero = jnp.zeros((HEAD_DIM, QW), qt.dtype)
    qm = jnp.concatenate([jnp.concatenate([qt[0:HEAD_DIM], zero], axis=0),
                          jnp.concatenate([zero, qt[HEAD_DIM:PAIR_W]], axis=0)], axis=1)

    def step(r0, r1):
        s = _dot(kbuf[pair, k0 + r0:k0 + r1, :], qm)
        running = {}
        for piece in range(r0 // PIECE, r1 // PIECE):
            r = piece * PIECE
            for head, half, plain, lanes in _unit_blocks(piece):
                sh = s[r - r0:r - r0 + PIECE, lanes]
                if not plain:
                    sh = sh + bias_t[2 * pair + head, r:r + PIECE,
                                     half * HALF_W:(half + 1) * HALF_W]
                s_ref[r:r + PIECE, lanes] = sh
                m = jnp.max(sh.reshape(PIECE // SUBLANES, SUBLANES, HALF_W), axis=0)
                key = (lanes.start, half)
                if key in running:
                    running[key] = jnp.maximum(running[key], m)
                elif _first_live(piece, half, lo):
                    running[key] = m
                else:
                    running[key] = jnp.maximum(mx_ref[:, lanes], m)
        for (start, _), m in running.items():
            mx_ref[:, start:start + HALF_W] = m

    starts = range(lo, KW, SCORE_ROWS)
    return [lambda r0=r0: step(r0, min(r0 + SCORE_ROWS, KW)) for r0 in starts]


def _softmax_steps(quad, lo, s_all, mx_all, p_all, sum_all):
    s_ref, mx_ref, p_ref, sum_ref = s_all.at[quad], mx_all.at[quad], p_all.at[quad], sum_all.at[quad]
    mx = jnp.max(mx_ref[...], axis=0, keepdims=True)

    def step(r0, r1):
        running = {}
        for piece in range(r0 // PIECE, r1 // PIECE):
            r = piece * PIECE
            for _, half, _, lanes in _unit_blocks(piece):
                e = jnp.exp2(s_ref[r:r + PIECE, lanes] - mx[:, lanes])
                p_ref[r:r + PIECE, lanes] = e.astype(jnp.bfloat16)
                part = jnp.sum(e.reshape(PIECE // SUBLANES, SUBLANES, HALF_W), axis=0)
                key = (lanes.start, half)
                if key in running:
                    running[key] = running[key] + part
                elif _first_live(piece, half, lo):
                    running[key] = part
                else:
                    running[key] = sum_ref[:, lanes] + part
        for (start, _), part in running.items():
            sum_ref[:, start:start + HALF_W] = part

    starts = range(lo, KW, SOFTMAX_ROWS)
    return [lambda r0=r0: step(r0, min(r0 + SOFTMAX_ROWS, KW)) for r0 in starts]


def _zero_dead_blocks(p_all):
    p_all[...] = jnp.zeros(p_all.shape, jnp.bfloat16)


def _weighted_values(pair, quad, lo, vtbuf, p_all, sum_all, out_t):
    row0 = pl.multiple_of(pair * PAIR_W, PAIR_W)
    k0 = quad * QW
    denom = jnp.sum(sum_all[quad], axis=0, keepdims=True)
    o_t = _dot(vtbuf[pl.ds(row0, PAIR_W), k0 + lo:k0 + KW], p_all[quad, lo:KW, :])
    inv = 1.0 / denom
    for head in range(2):
        rows = slice(head * HEAD_DIM, (head + 1) * HEAD_DIM)
        lanes = slice(head * QW, (head + 1) * QW)
        out_t[pl.ds(row0 + head * HEAD_DIM, HEAD_DIM), k0:k0 + QW] = o_t[rows, lanes] * inv[:, lanes]


def _interleave(a_steps, b_steps):
    done = 0
    for t, a_step in enumerate(a_steps):
        a_step()
        upto = (t + 1) * len(b_steps) // len(a_steps)
        for b_step in b_steps[done:upto]:
            b_step()
        done = upto


def _load_weight_bf16(w_hbm, dst, stage, sem, after_chunk=None):
    chunk_rows = stage.shape[1]
    n_chunks = w_hbm.shape[0] // chunk_rows

    def copy(c):
        rows = slice(c * chunk_rows, (c + 1) * chunk_rows)
        return pltpu.make_async_copy(w_hbm.at[rows, :], stage.at[c % 2], sem.at[c % 2])

    copy(0).start()
    for c in range(n_chunks):
        if c + 1 < n_chunks:
            copy(c + 1).start()
        copy(c).wait()
        rows = slice(c * chunk_rows, (c + 1) * chunk_rows)
        dst[rows, :] = stage[c % 2].astype(jnp.bfloat16)
        if after_chunk is not None:
            after_chunk(c, rows, stage.at[c % 2])


def _alternate(a_steps, b_steps):
    for t in range(max(len(a_steps), len(b_steps))):
        if t < len(a_steps):
            a_steps[t]()
        if t < len(b_steps):
            b_steps[t]()


def _mixer_kernel(x_ref, g_pre_ref, w_in_hbm, dw_w_ref, dw_b_ref,
                  ln_g_ref, ln_b_ref, rel_ref, w_out_hbm, g_post_ref, o_ref,
                  w_in_ref, w_out_ref, stage_in, stage_out, w_sem,
                  hbuf, hshift, cbuf, qbuf, kbuf, vtbuf, bias_t, s_ref, mx_ref, p_ref, sum_ref,
                  out_t, mixbuf):
    b = pl.program_id(0)
    i = pl.program_id(1)
    c_v = 2 * CONV_WIDTH + 2 * ATTN_WIDTH

    @pl.when((b == 0) & (i == 0))
    def _():
        _load_weight_bf16(w_in_hbm, w_in_ref, stage_in, w_sem)
        _load_weight_bf16(w_out_hbm, w_out_ref, stage_out, w_sem)
        _build_bias(rel_ref, bias_t)
        _zero_dead_blocks(p_ref)

    @pl.when(i == 0)
    def _():
        hbuf[0:HALO_H, :] = jnp.zeros((HALO_H, CONV_WIDTH), jnp.float32)
        kbuf[:, 0:HALO_KV, :] = jnp.zeros((N_PAIRS, HALO_KV, PAIR_W), jnp.bfloat16)
        vtbuf[:, 0:HALO_KV] = jnp.zeros((ATTN_WIDTH, HALO_KV), jnp.bfloat16)

    c_q = 2 * CONV_WIDTH
    c_k = c_q + ATTN_WIDTH

    def conv_block(base):
        acc = jnp.broadcast_to(dw_b_ref[...][None], (CONV_RB // SUBLANES, SUBLANES, CONV_WIDTH))
        for j in range(CONV_KERNEL):
            off = HALO_H - (CONV_KERNEL - 1) + j
            r = off % SUBLANES
            rows = slice(base + off - r, base + off - r + CONV_RB)
            tap = hbuf[rows, :] if r == 0 else hshift[r - 1, rows, :]
            acc = acc + dw_w_ref[j][None] * tap.reshape(acc.shape)
        cbuf[base:base + CONV_RB, :] = acc.reshape(CONV_RB, CONV_WIDTH)


    def project_steps(blk):
        lo_row, hi_row = blk * PROJ_RB, (blk + 1) * PROJ_RB
        rows = slice(lo_row, hi_row)
        state = {}

        def norm_and_value():
            state["u"] = _rms(x_ref[rows, :], g_pre_ref[...]).astype(jnp.bfloat16)
            state["a_val"] = _dot(state["u"], w_in_ref[:, 0:CONV_WIDTH])

        def gate():
            a_gate = _dot(state["u"], w_in_ref[:, CONV_WIDTH:c_q])
            hbuf[HALO_H + lo_row:HALO_H + hi_row, :] = state["a_val"] * jax.nn.sigmoid(a_gate)

        def queries():
            q = _dot(state["u"], w_in_ref[:, c_q:c_k]) * (HEAD_DIM ** -0.5 * LOG2E)
            for p in range(N_PAIRS):
                qbuf[p, :, rows] = q[:, p * PAIR_W:(p + 1) * PAIR_W].T.astype(jnp.bfloat16)

        def keys_values():
            k = _dot(state["u"], w_in_ref[:, c_k:c_k + ATTN_WIDTH]).astype(jnp.bfloat16)
            for p in range(N_PAIRS):
                kbuf[p, HALO_KV + lo_row:HALO_KV + hi_row, :] = k[:, p * PAIR_W:(p + 1) * PAIR_W]
            v = _dot(state["u"], w_in_ref[:, c_v:c_v + ATTN_WIDTH])
            vtbuf[:, HALO_KV + lo_row:HALO_KV + hi_row] = v.T.astype(jnp.bfloat16)

        return [norm_and_value, gate, queries, keys_values]

    def mix_steps(blk):
        lo_row, hi_row = blk * PROJ_RB, (blk + 1) * PROJ_RB
        rows = slice(lo_row, hi_row)

        def shifted_copies():
            sh_lo = 0 if blk == 0 else lo_row + HSHIFT_LEAD
            sh_hi = hi_row + HSHIFT_LEAD
            for r in range(1, SUBLANES):
                hshift[r - 1, sh_lo:sh_hi, :] = hbuf[sh_lo + r:sh_hi + r, :]

        def norm_swish():
            c = cbuf[rows, :]
            mu = jnp.mean(c, axis=-1, keepdims=True)
            xc = c - mu
            var = jnp.mean(xc * xc, axis=-1, keepdims=True)
            y = xc * lax.rsqrt(var + EPS) * ln_g_ref[...] + ln_b_ref[...]
            y = y * jax.nn.sigmoid(y)
            mixbuf[rows, 0:CONV_WIDTH] = y.astype(jnp.bfloat16)

        convs = [lambda base=lo_row + rb * CONV_RB: conv_block(base)
                 for rb in range(PROJ_RB // CONV_RB)]
        return [shifted_copies] + convs + [norm_swish]

    n_blocks = TM_MIX // PROJ_RB
    for step in project_steps(0):
        step()
    for blk in range(n_blocks):
        nxt = project_steps(blk + 1) if blk + 1 < n_blocks else []
        _alternate(nxt, mix_steps(blk))

    def attn_loop(first_tile):
        lo = [max(0, HALO_KV - quad * QW) if first_tile else 0 for quad in range(N_QUADS)]

        def scores(pair, quad):
            return _score_steps(pair, quad, lo[quad], qbuf, kbuf, bias_t, s_ref, mx_ref)

        for step in scores(0, 0):
            step()

        def body(pair, carry):
            for quad in range(N_QUADS):
                if quad + 1 < N_QUADS:
                    nxt = scores(pair, quad + 1)
                else:
                    nxt = scores(jnp.minimum(pair + 1, N_PAIRS - 1), 0)
                _interleave(nxt, _softmax_steps(quad, lo[quad], s_ref, mx_ref, p_ref, sum_ref))
                _weighted_values(pair, quad, lo[quad], vtbuf, p_ref, sum_ref, out_t)
            return carry
        lax.fori_loop(0, N_PAIRS, body, 0)

    @pl.when(i == 0)
    def _():
        attn_loop(True)

    @pl.when(i > 0)
    def _():
        attn_loop(False)

    for blk in range(TM_MIX // OUT_RB):
        rows = slice(blk * OUT_RB, (blk + 1) * OUT_RB)
        attn = out_t[:, rows].T.astype(jnp.bfloat16)
        mixed = _dot(jnp.concatenate([mixbuf[rows, 0:CONV_WIDTH], attn], axis=1), w_out_ref[...])
        o_ref[rows, :] = x_ref[rows, :] + _rms(mixed, g_post_ref[...])

    hbuf[0:HALO_H, :] = hbuf[TM_MIX:TM_MIX + HALO_H, :]
    kbuf[:, 0:HALO_KV, :] = kbuf[:, TM_MIX:TM_MIX + HALO_KV, :]
    vtbuf[:, 0:HALO_KV] = vtbuf[:, TM_MIX:TM_MIX + HALO_KV]


def _ffn_kernel(h_hbm, g_pre_ref, w_up_hbm, dw_w_ref, dw_b_ref, w_down_hbm, g_post_ref, o_hbm,
                w_up_ref, w_down_ref, stage_up, stage_down, w_sem,
                xin, xout, sem_in, sem_out, carry, gbuf, vbuf, actbuf):
    i = pl.program_id(1)
    n = pl.program_id(0) * pl.num_programs(1) + i
    n_tiles = pl.num_programs(0) * pl.num_programs(1)
    slot = n % 2

    def in_copy(tile, sl, s):
        return pltpu.make_async_copy(h_hbm.at[tile, s], xin.at[sl, :, s, :], sem_in.at[sl])

    def out_copy(tile, sl, s):
        return pltpu.make_async_copy(xout.at[sl, :, s, :], o_hbm.at[tile, s], sem_out.at[sl])

    @pl.when(n == 0)
    def _():
        for s in range(SUBLANES):
            in_copy(0, 0, s).start()
        _load_weight_bf16(w_up_hbm, w_up_ref, stage_up, w_sem)
        _load_weight_bf16(w_down_hbm, w_down_ref, stage_down, w_sem)

    @pl.when(n + 1 < n_tiles)
    def _():
        for s in range(SUBLANES):
            in_copy(n + 1, 1 - slot, s).start()

    @pl.when(i == 0)
    def _():
        carry[...] = jnp.zeros(carry.shape, jnp.float32)

    @pl.when(n >= 2)
    def _():
        for s in range(SUBLANES):
            out_copy(n - 2, slot, s).wait()

    for s in range(SUBLANES):
        in_copy(n, slot, s).wait()

    xt = xin[slot].reshape(TM_FFN, D_MODEL)
    u = _rms(xt, g_pre_ref[...]).astype(jnp.bfloat16)
    first_sublane = lax.broadcasted_iota(jnp.int32, (SUBLANES, FFN_CW), 0) == 0

    def up_conv(buf, cols):
        hc = _dot(u, w_up_ref[:, cols])
        for k in (1, 2):
            last = hc[TM_FFN - k * SUBLANES:TM_FFN - (k - 1) * SUBLANES, :]
            prev = carry[(2 - k) * SUBLANES:(3 - k) * SUBLANES, cols]
            buf[(2 - k) * SUBLANES:(3 - k) * SUBLANES, :] = jnp.where(
                first_sublane, pltpu.roll(prev, 1, 0), pltpu.roll(last, 1, 0))
            carry[(2 - k) * SUBLANES:(3 - k) * SUBLANES, cols] = last
        buf[FFN_HALO:FFN_HALO + TM_FFN, :] = hc
        y = dw_b_ref[:, cols] + dw_w_ref[2:3, cols] * hc
        y = y + dw_w_ref[1:2, cols] * buf[SUBLANES:SUBLANES + TM_FFN, :]
        y = y + dw_w_ref[0:1, cols] * buf[0:TM_FFN, :]
        return y

    for c in range(D_FF // FFN_CW):
        gate = up_conv(gbuf.at[c % 2], slice(c * FFN_CW, (c + 1) * FFN_CW))
        val = up_conv(vbuf.at[c % 2], slice(D_FF + c * FFN_CW, D_FF + (c + 1) * FFN_CW))
        actbuf[:, c * FFN_CW:(c + 1) * FFN_CW] = (jax.nn.gelu(gate) * val).astype(jnp.bfloat16)
    for blk in range(TM_FFN // FFN_DOWN_RB):
        rows = slice(blk * FFN_DOWN_RB, (blk + 1) * FFN_DOWN_RB)
        slabs = slice(blk * FFN_DOWN_RB // SUBLANES, (blk + 1) * FFN_DOWN_RB // SUBLANES)
        f = _dot(actbuf[rows, :], w_down_ref[...])
        y = xt[rows, :] + _rms(f, g_post_ref[...])
        xout[slot, slabs] = y.reshape(FFN_DOWN_RB // SUBLANES, SUBLANES, D_MODEL)
    for s in range(SUBLANES):
        out_copy(n, slot, s).start()

    @pl.when(n == n_tiles - 1)
    def _():
        @pl.when(n >= 1)
        def _():
            for s in range(SUBLANES):
                out_copy(n - 1, 1 - slot, s).wait()
        for s in range(SUBLANES):
            out_copy(n, slot, s).wait()


def _rel_distance_row(rel_table):
    h = rel_table.shape[0]
    far = rel_table[:, 2 * MAX_REL:2 * MAX_REL + 1]
    n_far = HALO_KV - MAX_REL + 1
    near = jnp.flip(rel_table, axis=1)[:, 1:]
    n_tail = ROLL_W - n_far - near.shape[1]
    return jnp.concatenate([jnp.broadcast_to(far, (h, n_far)), near,
                            jnp.broadcast_to(far, (h, n_tail))], axis=1)


def _const_spec(shape):
    return pl.BlockSpec(shape, lambda b, i: (0,) * len(shape), pipeline_mode=pl.Buffered(1))


def _mixer(x, g_pre, w_in, dw_w, dw_b, ln_g, ln_b, rel_row, w_out, g_post):
    B, T, D = x.shape
    row_spec = pl.BlockSpec((None, TM_MIX, D), lambda b, i: (b, i, 0))
    hbm_spec = pl.BlockSpec(memory_space=pl.ANY)
    consts = (g_pre, w_in, dw_w, dw_b, ln_g, ln_b, rel_row, w_out, g_post)
    const_specs = [hbm_spec if c is w_in or c is w_out else _const_spec(c.shape) for c in consts]
    return pl.pallas_call(
        _mixer_kernel,
        grid=(B, T // TM_MIX),
        in_specs=[row_spec] + const_specs,
        out_specs=row_spec,
        out_shape=jax.ShapeDtypeStruct(x.shape, x.dtype),
        scratch_shapes=[
            pltpu.VMEM(w_in.shape, jnp.bfloat16),
            pltpu.VMEM(w_out.shape, jnp.bfloat16),
            pltpu.VMEM((2, W_IN_CHUNK, w_in.shape[1]), jnp.float32),
            pltpu.VMEM((2, W_OUT_CHUNK, w_out.shape[1]), jnp.float32),
            pltpu.SemaphoreType.DMA((2,)),
            pltpu.VMEM((HALO_H + TM_MIX, CONV_WIDTH), jnp.float32),
            pltpu.VMEM((SUBLANES - 1, HSHIFT_ROWS, CONV_WIDTH), jnp.float32),
            pltpu.VMEM((TM_MIX, CONV_WIDTH), jnp.float32),
            pltpu.VMEM((N_PAIRS, PAIR_W, TM_MIX), jnp.bfloat16),
            pltpu.VMEM((N_PAIRS, HALO_KV + TM_MIX, PAIR_W), jnp.bfloat16),
            pltpu.VMEM((ATTN_WIDTH, HALO_KV + TM_MIX), jnp.bfloat16),
            pltpu.VMEM((N_HEADS, KW, QW), jnp.float32),
            pltpu.VMEM((N_QUADS, KW, 2 * QW), jnp.float32),
            pltpu.VMEM((N_QUADS, SUBLANES, 2 * QW), jnp.float32),
            pltpu.VMEM((N_QUADS, KW, 2 * QW), jnp.bfloat16),
            pltpu.VMEM((N_QUADS, SUBLANES, 2 * QW), jnp.float32),
            pltpu.VMEM((ATTN_WIDTH, TM_MIX), jnp.float32),
            pltpu.VMEM((TM_MIX, CONV_WIDTH + ATTN_WIDTH), jnp.bfloat16),
        ],
        compiler_params=pltpu.CompilerParams(
            dimension_semantics=("arbitrary", "arbitrary"), vmem_limit_bytes=VMEM_LIMIT),
        name="mixer",
    )(x, *consts)


def _ffn(h, g_pre, w_up, dw_w, dw_b, w_down, g_post):
    B, T, D = h.shape
    n_t = T // TM_FFN
    tiles = (B * n_t, SUBLANES, FFN_SEG, D)
    hbm_spec = pl.BlockSpec(memory_space=pl.ANY)
    out = pl.pallas_call(
        _ffn_kernel,
        grid=(B, n_t),
        in_specs=[hbm_spec,
                  _const_spec((1, D)), hbm_spec, _const_spec(dw_w.shape),
                  _const_spec((1, 2 * D_FF)), hbm_spec, _const_spec((1, D))],
        out_specs=hbm_spec,
        out_shape=jax.ShapeDtypeStruct(tiles, h.dtype),
        scratch_shapes=[
            pltpu.VMEM(w_up.shape, jnp.bfloat16),
            pltpu.VMEM(w_down.shape, jnp.bfloat16),
            pltpu.VMEM((2, W_UP_CHUNK, w_up.shape[1]), jnp.float32),
            pltpu.VMEM((2, W_DOWN_CHUNK, w_down.shape[1]), jnp.float32),
            pltpu.SemaphoreType.DMA((2,)),
            pltpu.VMEM((2, FFN_SEG, SUBLANES, D), jnp.float32),
            pltpu.VMEM((2, FFN_SEG, SUBLANES, D), jnp.float32),
            pltpu.SemaphoreType.DMA((2,)),
            pltpu.SemaphoreType.DMA((2,)),
            pltpu.VMEM((FFN_HALO, 2 * D_FF), jnp.float32),
            pltpu.VMEM((2, FFN_HALO + TM_FFN, FFN_CW), jnp.float32),
            pltpu.VMEM((2, FFN_HALO + TM_FFN, FFN_CW), jnp.float32),
            pltpu.VMEM((TM_FFN, D_FF), jnp.bfloat16),
        ],
        compiler_params=pltpu.CompilerParams(
            dimension_semantics=("arbitrary", "arbitrary"), vmem_limit_bytes=VMEM_LIMIT),
        name="ffn",
    )(h.reshape(tiles), g_pre, w_up, dw_w, dw_b, w_down, g_post)
    return out.reshape(B, T, D)


def kernel(x, norm_mix_pre, w_in, conv_dw_w, conv_dw_b, conv_ln_g, conv_ln_b, rel_bias, w_out,
           norm_mix_post, norm_ffn_pre, w_up, ffn_dw_w, ffn_dw_b, w_down, norm_ffn_post):
    h = x
    for l in range(norm_mix_pre.shape[0]):
        h = _mixer(h, norm_mix_pre[l][None], w_in[l],
                   jnp.broadcast_to(conv_dw_w[l][:, None, :], (CONV_KERNEL, SUBLANES, CONV_WIDTH)),
                   conv_dw_b[l][None], conv_ln_g[l][None], conv_ln_b[l][None],
                   _rel_distance_row(rel_bias[l]), w_out[l], norm_mix_post[l][None])
        h = _ffn(h, norm_ffn_pre[l][None], w_up[l], ffn_dw_w[l],
                 ffn_dw_b[l][None], w_down[l], norm_ffn_post[l][None])
    return h
```

```python
import jax
import jax.numpy as jnp
from jax import lax
from jax.experimental import pallas as pl
from jax.experimental.pallas import tpu as pltpu

D_MODEL = 1024
CHUNK = 64
N_LEFT_CHUNKS = 8
CONV_WIDTH = 512
ATTN_WIDTH = 512
HEAD_DIM = 64
N_HEADS = ATTN_WIDTH // HEAD_DIM
N_PAIRS = N_HEADS // 2
PAIR_W = 2 * HEAD_DIM
CONV_KERNEL = 31
MAX_REL = 128
D_FF = 2816
FFN_CONV_KERNEL = 3
EPS = 1e-6
NEG_INF = -1e30

TM_MIX = 512
HALO_H = 32
HALO_KV = N_LEFT_CHUNKS * CHUNK
CONV_RB = 32
OUT_RB = 256
PROJ_RB = 256
VT_COLS = 256
SUBLANES = 8
HSHIFT_LEAD = HALO_H - SUBLANES
HSHIFT_ROWS = HALO_H + TM_MIX - SUBLANES
QUAD = 4
QW = QUAD * CHUNK
KW = HALO_KV + QW
N_QUADS = TM_MIX // QW
PIECE = 128
SCORE_ROWS = 384
SOFTMAX_ROWS = 256
HALF_W = 128
FAR_LAG = (HALO_KV - MAX_REL) // CHUNK - 1
LOG2E = 1.4426950408889634
ROLL_W = 1024
TM_FFN = 512
FFN_CW = 256
FFN_DOWN_RB = 256
FFN_SEG = TM_FFN // SUBLANES
FFN_HALO = (FFN_CONV_KERNEL - 1) * SUBLANES
W_IN_CHUNK = 128
W_OUT_CHUNK = 256
W_UP_CHUNK = 128
W_DOWN_CHUNK = 352
VMEM_LIMIT = 56 * 1024 * 1024


def _rms(xf, g):
    return xf * lax.rsqrt(jnp.mean(xf * xf, axis=-1, keepdims=True) + EPS) * g


def _dot(a, b):
    return jnp.dot(a, b, preferred_element_type=jnp.float32)


def _dot_nt(a, b):
    return lax.dot_general(a, b, (((1,), (1,)), ((), ())), preferred_element_type=jnp.float32)


def _build_bias(rel_ref, bias_t):
    r_idx = lax.broadcasted_iota(jnp.int32, (KW, QW), 0) // CHUNK
    c_idx = lax.broadcasted_iota(jnp.int32, (KW, QW), 1) // CHUNK
    visible = (r_idx >= c_idx) & (r_idx <= c_idx + N_LEFT_CHUNKS)
    for h in range(N_HEADS):
        row = jnp.broadcast_to(rel_ref[h:h + 1, :], (QW, ROLL_W))
        nat = pltpu.roll(row, 0, 1, stride=1, stride_axis=0)[:, 0:KW]
        rel_to_far = (nat.T - rel_ref[h:h + 1, 0:1]) * LOG2E
        bias_t[h] = jnp.where(visible, rel_to_far, NEG_INF)


def _block_kind(piece, half):
    lags = [kc - qc
            for kc in range(piece * PIECE // CHUNK, (piece + 1) * PIECE // CHUNK)
            for qc in range(half * HALF_W // CHUNK, (half + 1) * HALF_W // CHUNK)]
    live = any(0 <= lag <= N_LEFT_CHUNKS for lag in lags)
    plain = all(0 <= lag <= FAR_LAG for lag in lags)
    return live, plain


def _first_live(piece, half, lo):
    return not any(_block_kind(t, half)[0] for t in range(lo // PIECE, piece))


def _unit_blocks(piece):
    for head in range(2):
        for half in range(QW // HALF_W):
            live, plain = _block_kind(piece, half)
            if live:
                start = head * QW + half * HALF_W
                yield head, half, plain, slice(start, start + HALF_W)


def _score_steps(pair, quad, lo, qbuf, kbuf, bias_t, s_all, mx_all):
    s_ref, mx_ref = s_all.at[quad], mx_all.at[quad]
    k0 = quad * QW
    lane = lax.broadcasted_iota(jnp.int32, (1, PAIR_W), 1)
    qq = qbuf[pair, k0:k0 + QW, :]
    zero = jnp.zeros_like(qq)
    qm = jnp.concatenate([jnp.where(lane < HEAD_DIM, qq, zero),
                          jnp.where(lane >= HEAD_DIM, qq, zero)], axis=0)

    def step(r0, r1):
        s = _dot_nt(kbuf[pair, k0 + r0:k0 + r1, :], qm)
        running = {}
        for piece in range(r0 // PIECE, r1 // PIECE):
            r = piece * PIECE
            for head, half, plain, lanes in _unit_blocks(piece):
                sh = s[r - r0:r - r0 + PIECE, lanes]
                if not plain:
                    sh = sh + bias_t[2 * pair + head, r:r + PIECE,
                                     half * HALF_W:(half + 1) * HALF_W]
                s_ref[r:r + PIECE, lanes] = sh
                m = jnp.max(sh.reshape(PIECE // SUBLANES, SUBLANES, HALF_W), axis=0)
                key = (lanes.start, half)
                if key in running:
                    running[key] = jnp.maximum(running[key], m)
                elif _first_live(piece, half, lo):
                    running[key] = m
                else:
                    running[key] = jnp.maximum(mx_ref[:, lanes], m)
        for (start, _), m in running.items():
            mx_ref[:, start:start + HALF_W] = m

    starts = range(lo, KW, SCORE_ROWS)
    return [lambda r0=r0: step(r0, min(r0 + SCORE_ROWS, KW)) for r0 in starts]


def _softmax_steps(quad, lo, s_all, mx_all, p_all, sum_all):
    s_ref, mx_ref, p_ref, sum_ref = s_all.at[quad], mx_all.at[quad], p_all.at[quad], sum_all.at[quad]
    mx = jnp.max(mx_ref[...], axis=0, keepdims=True)

    def step(r0, r1):
        running = {}
        for piece in range(r0 // PIECE, r1 // PIECE):
            r = piece * PIECE
            for _, half, _, lanes in _unit_blocks(piece):
                e = jnp.exp2(s_ref[r:r + PIECE, lanes] - mx[:, lanes])
                p_ref[r:r + PIECE, lanes] = e.astype(jnp.bfloat16)
                part = jnp.sum(e.reshape(PIECE // SUBLANES, SUBLANES, HALF_W), axis=0)
                key = (lanes.start, half)
                if key in running:
                    running[key] = running[key] + part
                elif _first_live(piece, half, lo):
                    running[key] = part
                else:
                    running[key] = sum_ref[:, lanes] + part
        for (start, _), part in running.items():
            sum_ref[:, start:start + HALF_W] = part

    starts = range(lo, KW, SOFTMAX_ROWS)
    return [lambda r0=r0: step(r0, min(r0 + SOFTMAX_ROWS, KW)) for r0 in starts]


def _zero_dead_blocks(p_all):
    p_all[...] = jnp.zeros(p_all.shape, jnp.bfloat16)


def _weighted_values(pair, quad, lo, vtbuf, p_all, sum_all, out_t):
    row0 = pl.multiple_of(pair * PAIR_W, PAIR_W)
    k0 = quad * QW
    denom = jnp.sum(sum_all[quad], axis=0, keepdims=True)
    o_t = _dot(vtbuf[pl.ds(row0, PAIR_W), k0 + lo:k0 + KW], p_all[quad, lo:KW, :])
    inv = 1.0 / denom
    for head in range(2):
        rows = slice(head * HEAD_DIM, (head + 1) * HEAD_DIM)
        lanes = slice(head * QW, (head + 1) * QW)
        out_t[pl.ds(row0 + head * HEAD_DIM, HEAD_DIM), k0:k0 + QW] = o_t[rows, lanes] * inv[:, lanes]


def _interleave(a_steps, b_steps):
    done = 0
    for t, a_step in enumerate(a_steps):
        a_step()
        upto = (t + 1) * len(b_steps) // len(a_steps)
        for b_step in b_steps[done:upto]:
            b_step()
        done = upto


def _load_weight_bf16(w_hbm, dst, stage, sem, after_chunk=None):
    chunk_rows = stage.shape[1]
    n_chunks = w_hbm.shape[0] // chunk_rows

    def copy(c):
        rows = slice(c * chunk_rows, (c + 1) * chunk_rows)
        return pltpu.make_async_copy(w_hbm.at[rows, :], stage.at[c % 2], sem.at[c % 2])

    copy(0).start()
    for c in range(n_chunks):
        if c + 1 < n_chunks:
            copy(c + 1).start()
        copy(c).wait()
        rows = slice(c * chunk_rows, (c + 1) * chunk_rows)
        dst[rows, :] = stage[c % 2].astype(jnp.bfloat16)
        if after_chunk is not None:
            after_chunk(c, rows, stage.at[c % 2])


def _alternate(a_steps, b_steps):
    for t in range(max(len(a_steps), len(b_steps))):
        if t < len(a_steps):
            a_steps[t]()
        if t < len(b_steps):
            b_steps[t]()


def _mixer_kernel(x_ref, g_pre_ref, w_in_hbm, dw_w_ref, dw_b_ref,
                  ln_g_ref, ln_b_ref, rel_ref, w_out_hbm, g_post_ref, o_ref,
                  w_in_ref, w_vt_ref, w_out_ref, stage_in, stage_out, w_sem,
                  hbuf, hshift, cbuf, qbuf, kbuf, vtbuf, bias_t, s_ref, mx_ref, p_ref, sum_ref,
                  out_t, mixbuf):
    b = pl.program_id(0)
    i = pl.program_id(1)
    c_v = 2 * CONV_WIDTH + 2 * ATTN_WIDTH

    @pl.when((b == 0) & (i == 0))
    def _():
        def value_weights_transposed(c, rows, staged):
            w_vt_ref[:, rows] = staged[:, c_v:c_v + ATTN_WIDTH].T.astype(jnp.bfloat16)

        _load_weight_bf16(w_in_hbm, w_in_ref, stage_in, w_sem, value_weights_transposed)
        _load_weight_bf16(w_out_hbm, w_out_ref, stage_out, w_sem)
        _build_bias(rel_ref, bias_t)
        _zero_dead_blocks(p_ref)

    @pl.when(i == 0)
    def _():
        hbuf[0:HALO_H, :] = jnp.zeros((HALO_H, CONV_WIDTH), jnp.float32)
        kbuf[:, 0:HALO_KV, :] = jnp.zeros((N_PAIRS, HALO_KV, PAIR_W), jnp.bfloat16)
        vtbuf[:, 0:HALO_KV] = jnp.zeros((ATTN_WIDTH, HALO_KV), jnp.bfloat16)

    c_q = 2 * CONV_WIDTH
    c_k = c_q + ATTN_WIDTH

    def conv_block(base):
        acc = jnp.broadcast_to(dw_b_ref[...][None], (CONV_RB // SUBLANES, SUBLANES, CONV_WIDTH))
        for j in range(CONV_KERNEL):
            off = HALO_H - (CONV_KERNEL - 1) + j
            r = off % SUBLANES
            rows = slice(base + off - r, base + off - r + CONV_RB)
            tap = hbuf[rows, :] if r == 0 else hshift[r - 1, rows, :]
            acc = acc + dw_w_ref[j][None] * tap.reshape(acc.shape)
        cbuf[base:base + CONV_RB, :] = acc.reshape(CONV_RB, CONV_WIDTH)

    pending = []

    def project_steps(blk):
        lo_row, hi_row = blk * PROJ_RB, (blk + 1) * PROJ_RB
        rows = slice(lo_row, hi_row)
        state = {}

        def norm_and_value():
            state["u"] = _rms(x_ref[rows, :], g_pre_ref[...]).astype(jnp.bfloat16)
            state["a_val"] = _dot(state["u"], w_in_ref[:, 0:CONV_WIDTH])

        def gate():
            a_gate = _dot(state["u"], w_in_ref[:, CONV_WIDTH:c_q])
            hbuf[HALO_H + lo_row:HALO_H + hi_row, :] = state["a_val"] * jax.nn.sigmoid(a_gate)

        def queries():
            q = _dot(state["u"], w_in_ref[:, c_q:c_k]) * (HEAD_DIM ** -0.5 * LOG2E)
            q = q.astype(jnp.bfloat16)
            for p in range(N_PAIRS):
                qbuf[p, rows, :] = q[:, p * PAIR_W:(p + 1) * PAIR_W]

        def keys_values():
            k = _dot(state["u"], w_in_ref[:, c_k:c_k + ATTN_WIDTH]).astype(jnp.bfloat16)
            for p in range(N_PAIRS):
                kbuf[p, HALO_KV + lo_row:HALO_KV + hi_row, :] = k[:, p * PAIR_W:(p + 1) * PAIR_W]
            pending.append(state["u"])
            if hi_row % VT_COLS == 0:
                u_wide = pending[0] if len(pending) == 1 else jnp.concatenate(pending, axis=0)
                vtbuf[:, HALO_KV + hi_row - VT_COLS:HALO_KV + hi_row] = (
                    _dot_nt(w_vt_ref[...], u_wide).astype(jnp.bfloat16))
                pending.clear()

        return [norm_and_value, gate, queries, keys_values]

    def mix_steps(blk):
        lo_row, hi_row = blk * PROJ_RB, (blk + 1) * PROJ_RB
        rows = slice(lo_row, hi_row)

        def shifted_copies():
            sh_lo = 0 if blk == 0 else lo_row + HSHIFT_LEAD
            sh_hi = hi_row + HSHIFT_LEAD
            for r in range(1, SUBLANES):
                hshift[r - 1, sh_lo:sh_hi, :] = hbuf[sh_lo + r:sh_hi + r, :]

        def norm_swish():
            c = cbuf[rows, :]
            mu = jnp.mean(c, axis=-1, keepdims=True)
            xc = c - mu
            var = jnp.mean(xc * xc, axis=-1, keepdims=True)
            y = xc * lax.rsqrt(var + EPS) * ln_g_ref[...] + ln_b_ref[...]
            y = y * jax.nn.sigmoid(y)
            mixbuf[rows, 0:CONV_WIDTH] = y.astype(jnp.bfloat16)

        convs = [lambda base=lo_row + rb * CONV_RB: conv_block(base)
                 for rb in range(PROJ_RB // CONV_RB)]
        return [shifted_copies] + convs + [norm_swish]

    n_blocks = TM_MIX // PROJ_RB
    for step in project_steps(0):
        step()
    for blk in range(n_blocks):
        nxt = project_steps(blk + 1) if blk + 1 < n_blocks else []
        _alternate(nxt, mix_steps(blk))

    def attn_loop(first_tile):
        lo = [max(0, HALO_KV - quad * QW) if first_tile else 0 for quad in range(N_QUADS)]

        def scores(pair, quad):
            return _score_steps(pair, quad, lo[quad], qbuf, kbuf, bias_t, s_ref, mx_ref)

        for step in scores(0, 0):
            step()

        def body(pair, carry):
            for quad in range(N_QUADS):
                if quad + 1 < N_QUADS:
                    nxt = scores(pair, quad + 1)
                else:
                    nxt = scores(jnp.minimum(pair + 1, N_PAIRS - 1), 0)
                _interleave(nxt, _softmax_steps(quad, lo[quad], s_ref, mx_ref, p_ref, sum_ref))
                _weighted_values(pair, quad, lo[quad], vtbuf, p_ref, sum_ref, out_t)
            return carry
        lax.fori_loop(0, N_PAIRS, body, 0)

    @pl.when(i == 0)
    def _():
        attn_loop(True)

    @pl.when(i > 0)
    def _():
        attn_loop(False)

    for blk in range(TM_MIX // OUT_RB):
        rows = slice(blk * OUT_RB, (blk + 1) * OUT_RB)
        attn = out_t[:, rows].T.astype(jnp.bfloat16)
        mixed = _dot(jnp.concatenate([mixbuf[rows, 0:CONV_WIDTH], attn], axis=1), w_out_ref[...])
        o_ref[rows, :] = x_ref[rows, :] + _rms(mixed, g_post_ref[...])

    hbuf[0:HALO_H, :] = hbuf[TM_MIX:TM_MIX + HALO_H, :]
    kbuf[:, 0:HALO_KV, :] = kbuf[:, TM_MIX:TM_MIX + HALO_KV, :]
    vtbuf[:, 0:HALO_KV] = vtbuf[:, TM_MIX:TM_MIX + HALO_KV]


def _ffn_kernel(h_hbm, g_pre_ref, w_up_hbm, dw_w_ref, dw_b_ref, w_down_hbm, g_post_ref, o_hbm,
                w_up_ref, w_down_ref, stage_up, stage_down, w_sem,
                xin, xout, sem_in, sem_out, carry, gbuf, vbuf, actbuf):
    i = pl.program_id(1)
    n = pl.program_id(0) * pl.num_programs(1) + i
    n_tiles = pl.num_programs(0) * pl.num_programs(1)
    slot = n % 2

    def in_copy(tile, sl, s):
        return pltpu.make_async_copy(h_hbm.at[tile, s], xin.at[sl, :, s, :], sem_in.at[sl])

    def out_copy(tile, sl, s):
        return pltpu.make_async_copy(xout.at[sl, :, s, :], o_hbm.at[tile, s], sem_out.at[sl])

    @pl.when(n == 0)
    def _():
        for s in range(SUBLANES):
            in_copy(0, 0, s).start()
        _load_weight_bf16(w_up_hbm, w_up_ref, stage_up, w_sem)
        _load_weight_bf16(w_down_hbm, w_down_ref, stage_down, w_sem)

    @pl.when(n + 1 < n_tiles)
    def _():
        for s in range(SUBLANES):
            in_copy(n + 1, 1 - slot, s).start()

    @pl.when(i == 0)
    def _():
        carry[...] = jnp.zeros(carry.shape, jnp.float32)

    @pl.when(n >= 2)
    def _():
        for s in range(SUBLANES):
            out_copy(n - 2, slot, s).wait()

    for s in range(SUBLANES):
        in_copy(n, slot, s).wait()

    xt = xin[slot].reshape(TM_FFN, D_MODEL)
    u = _rms(xt, g_pre_ref[...]).astype(jnp.bfloat16)
    first_sublane = lax.broadcasted_iota(jnp.int32, (SUBLANES, FFN_CW), 0) == 0

    def up_conv(buf, cols):
        hc = _dot(u, w_up_ref[:, cols])
        for k in (1, 2):
            last = hc[TM_FFN - k * SUBLANES:TM_FFN - (k - 1) * SUBLANES, :]
            prev = carry[(2 - k) * SUBLANES:(3 - k) * SUBLANES, cols]
            buf[(2 - k) * SUBLANES:(3 - k) * SUBLANES, :] = jnp.where(
                first_sublane, pltpu.roll(prev, 1, 0), pltpu.roll(last, 1, 0))
            carry[(2 - k) * SUBLANES:(3 - k) * SUBLANES, cols] = last
        buf[FFN_HALO:FFN_HALO + TM_FFN, :] = hc
        y = dw_b_ref[:, cols] + dw_w_ref[2:3, cols] * hc
        y = y + dw_w_ref[1:2, cols] * buf[SUBLANES:SUBLANES + TM_FFN, :]
        y = y + dw_w_ref[0:1, cols] * buf[0:TM_FFN, :]
        return y

    for c in range(D_FF // FFN_CW):
        gate = up_conv(gbuf.at[c % 2], slice(c * FFN_CW, (c + 1) * FFN_CW))
        val = up_conv(vbuf.at[c % 2], slice(D_FF + c * FFN_CW, D_FF + (c + 1) * FFN_CW))
        actbuf[:, c * FFN_CW:(c + 1) * FFN_CW] = (jax.nn.gelu(gate) * val).astype(jnp.bfloat16)
    for blk in range(TM_FFN // FFN_DOWN_RB):
        rows = slice(blk * FFN_DOWN_RB, (blk + 1) * FFN_DOWN_RB)
        slabs = slice(blk * FFN_DOWN_RB // SUBLANES, (blk + 1) * FFN_DOWN_RB // SUBLANES)
        f = _dot(actbuf[rows, :], w_down_ref[...])
        y = xt[rows, :] + _rms(f, g_post_ref[...])
        xout[slot, slabs] = y.reshape(FFN_DOWN_RB // SUBLANES, SUBLANES, D_MODEL)
    for s in range(SUBLANES):
        out_copy(n, slot, s).start()

    @pl.when(n == n_tiles - 1)
    def _():
        @pl.when(n >= 1)
        def _():
            for s in range(SUBLANES):
                out_copy(n - 1, 1 - slot, s).wait()
        for s in range(SUBLANES):
            out_copy(n, slot, s).wait()


def _rel_distance_row(rel_table):
    h = rel_table.shape[0]
    far = rel_table[:, 2 * MAX_REL:2 * MAX_REL + 1]
    n_far = HALO_KV - MAX_REL + 1
    near = jnp.flip(rel_table, axis=1)[:, 1:]
    n_tail = ROLL_W - n_far - near.shape[1]
    return jnp.concatenate([jnp.broadcast_to(far, (h, n_far)), near,
                            jnp.broadcast_to(far, (h, n_tail))], axis=1)


def _const_spec(shape):
    return pl.BlockSpec(shape, lambda b, i: (0,) * len(shape))


def _mixer(x, g_pre, w_in, dw_w, dw_b, ln_g, ln_b, rel_row, w_out, g_post):
    B, T, D = x.shape
    row_spec = pl.BlockSpec((None, TM_MIX, D), lambda b, i: (b, i, 0))
    hbm_spec = pl.BlockSpec(memory_space=pl.ANY)
    consts = (g_pre, w_in, dw_w, dw_b, ln_g, ln_b, rel_row, w_out, g_post)
    const_specs = [hbm_spec if c is w_in or c is w_out else _const_spec(c.shape) for c in consts]
    return pl.pallas_call(
        _mixer_kernel,
        grid=(B, T // TM_MIX),
        in_specs=[row_spec] + const_specs,
        out_specs=row_spec,
        out_shape=jax.ShapeDtypeStruct(x.shape, x.dtype),
        scratch_shapes=[
            pltpu.VMEM(w_in.shape, jnp.bfloat16),
            pltpu.VMEM((ATTN_WIDTH, D), jnp.bfloat16),
            pltpu.VMEM(w_out.shape, jnp.bfloat16),
            pltpu.VMEM((2, W_IN_CHUNK, w_in.shape[1]), jnp.float32),
            pltpu.VMEM((2, W_OUT_CHUNK, w_out.shape[1]), jnp.float32),
            pltpu.SemaphoreType.DMA((2,)),
            pltpu.VMEM((HALO_H + TM_MIX, CONV_WIDTH), jnp.float32),
            pltpu.VMEM((SUBLANES - 1, HSHIFT_ROWS, CONV_WIDTH), jnp.float32),
            pltpu.VMEM((TM_MIX, CONV_WIDTH), jnp.float32),
            pltpu.VMEM((N_PAIRS, TM_MIX, PAIR_W), jnp.bfloat16),
            pltpu.VMEM((N_PAIRS, HALO_KV + TM_MIX, PAIR_W), jnp.bfloat16),
            pltpu.VMEM((ATTN_WIDTH, HALO_KV + TM_MIX), jnp.bfloat16),
            pltpu.VMEM((N_HEADS, KW, QW), jnp.float32),
            pltpu.VMEM((N_QUADS, KW, 2 * QW), jnp.float32),
            pltpu.VMEM((N_QUADS, SUBLANES, 2 * QW), jnp.float32),
            pltpu.VMEM((N_QUADS, KW, 2 * QW), jnp.bfloat16),
            pltpu.VMEM((N_QUADS, SUBLANES, 2 * QW), jnp.float32),
            pltpu.VMEM((ATTN_WIDTH, TM_MIX), jnp.float32),
            pltpu.VMEM((TM_MIX, CONV_WIDTH + ATTN_WIDTH), jnp.bfloat16),
        ],
        compiler_params=pltpu.CompilerParams(
            dimension_semantics=("arbitrary", "arbitrary"), vmem_limit_bytes=VMEM_LIMIT),
        name="mixer",
    )(x, *consts)


def _ffn(h, g_pre, w_up, dw_w, dw_b, w_down, g_post):
    B, T, D = h.shape
    n_t = T // TM_FFN
    tiles = (B * n_t, SUBLANES, FFN_SEG, D)
    hbm_spec = pl.BlockSpec(memory_space=pl.ANY)
    out = pl.pallas_call(
        _ffn_kernel,
        grid=(B, n_t),
        in_specs=[hbm_spec,
                  _const_spec((1, D)), hbm_spec, _const_spec(dw_w.shape),
                  _const_spec((1, 2 * D_FF)), hbm_spec, _const_spec((1, D))],
        out_specs=hbm_spec,
        out_shape=jax.ShapeDtypeStruct(tiles, h.dtype),
        scratch_shapes=[
            pltpu.VMEM(w_up.shape, jnp.bfloat16),
            pltpu.VMEM(w_down.shape, jnp.bfloat16),
            pltpu.VMEM((2, W_UP_CHUNK, w_up.shape[1]), jnp.float32),
            pltpu.VMEM((2, W_DOWN_CHUNK, w_down.shape[1]), jnp.float32),
            pltpu.SemaphoreType.DMA((2,)),
            pltpu.VMEM((2, FFN_SEG, SUBLANES, D), jnp.float32),
            pltpu.VMEM((2, FFN_SEG, SUBLANES, D), jnp.float32),
            pltpu.SemaphoreType.DMA((2,)),
            pltpu.SemaphoreType.DMA((2,)),
            pltpu.VMEM((FFN_HALO, 2 * D_FF), jnp.float32),
            pltpu.VMEM((2, FFN_HALO + TM_FFN, FFN_CW), jnp.float32),
            pltpu.VMEM((2, FFN_HALO + TM_FFN, FFN_CW), jnp.float32),
            pltpu.VMEM((TM_FFN, D_FF), jnp.bfloat16),
        ],
        compiler_params=pltpu.CompilerParams(
            dimension_semantics=("arbitrary", "arbitrary"), vmem_limit_bytes=VMEM_LIMIT),
        name="ffn",
    )(h.reshape(tiles), g_pre, w_up, dw_w, dw_b, w_down, g_post)
    return out.reshape(B, T, D)


def kernel(x, norm_mix_pre, w_in, conv_dw_w, conv_dw_b, conv_ln_g, conv_ln_b, rel_bias, w_out,
           norm_mix_post, norm_ffn_pre, w_up, ffn_dw_w, ffn_dw_b, w_down, norm_ffn_post):
    h = x
    for l in range(norm_mix_pre.shape[0]):
        h = _mixer(h, norm_mix_pre[l][None], w_in[l],
                   jnp.broadcast_to(conv_dw_w[l][:, None, :], (CONV_KERNEL, SUBLANES, CONV_WIDTH)),
                   conv_dw_b[l][None], conv_ln_g[l][None], conv_ln_b[l][None],
                   _rel_distance_row(rel_bias[l]), w_out[l], norm_mix_post[l][None])
        h = _ffn(h, norm_ffn_pre[l][None], w_up[l], ffn_dw_w[l],
                 ffn_dw_b[l][None], w_down[l], norm_ffn_post[l][None])
    return h
```

```python
import jax
import jax.numpy as jnp
from jax import lax
from jax.experimental import pallas as pl
from jax.experimental.pallas import tpu as pltpu

D_MODEL = 1024
CHUNK = 64
N_LEFT_CHUNKS = 8
CONV_WIDTH = 512
ATTN_WIDTH = 512
HEAD_DIM = 64
N_HEADS = ATTN_WIDTH // HEAD_DIM
N_PAIRS = N_HEADS // 2
PAIR_W = 2 * HEAD_DIM
CONV_KERNEL = 31
MAX_REL = 128
D_FF = 2816
FFN_CONV_KERNEL = 3
EPS = 1e-6
NEG_INF = -1e30

TM_MIX = 512
HALO_H = 32
HALO_KV = N_LEFT_CHUNKS * CHUNK
CONV_RB = 32
OUT_RB = 256
PROJ_RB = 256
VT_COLS = 256
SUBLANES = 8
HSHIFT_LEAD = HALO_H - SUBLANES
HSHIFT_ROWS = HALO_H + TM_MIX - SUBLANES
QUAD = 4
QW = QUAD * CHUNK
KW = HALO_KV + QW
N_QUADS = TM_MIX // QW
PIECE = 128
SCORE_ROWS = 384
SOFTMAX_ROWS = 256
HALF_W = 128
FAR_LAG = (HALO_KV - MAX_REL) // CHUNK - 1
LOG2E = 1.4426950408889634
ROLL_W = 1024
TM_FFN = 512
FFN_CW = 256
FFN_DOWN_RB = 256
FFN_SEG = TM_FFN // SUBLANES
FFN_HALO = (FFN_CONV_KERNEL - 1) * SUBLANES
W_IN_CHUNK = 128
W_OUT_CHUNK = 256
W_UP_CHUNK = 128
W_DOWN_CHUNK = 352
VMEM_LIMIT = 56 * 1024 * 1024


def _rms(xf, g):
    return xf * lax.rsqrt(jnp.mean(xf * xf, axis=-1, keepdims=True) + EPS) * g


def _dot(a, b):
    return jnp.dot(a, b, preferred_element_type=jnp.float32)


def _dot_nt(a, b):
    return lax.dot_general(a, b, (((1,), (1,)), ((), ())), preferred_element_type=jnp.float32)


def _build_bias(rel_ref, bias_t):
    r_idx = lax.broadcasted_iota(jnp.int32, (KW, QW), 0) // CHUNK
    c_idx = lax.broadcasted_iota(jnp.int32, (KW, QW), 1) // CHUNK
    visible = (r_idx >= c_idx) & (r_idx <= c_idx + N_LEFT_CHUNKS)
    for h in range(N_HEADS):
        row = jnp.broadcast_to(rel_ref[h:h + 1, :], (QW, ROLL_W))
        nat = pltpu.roll(row, 0, 1, stride=1, stride_axis=0)[:, 0:KW]
        rel_to_far = (nat.T - rel_ref[h:h + 1, 0:1]) * LOG2E
        bias_t[h] = jnp.where(visible, rel_to_far, NEG_INF)


def _block_kind(piece, half):
    lags = [kc - qc
            for kc in range(piece * PIECE // CHUNK, (piece + 1) * PIECE // CHUNK)
            for qc in range(half * HALF_W // CHUNK, (half + 1) * HALF_W // CHUNK)]
    live = any(0 <= lag <= N_LEFT_CHUNKS for lag in lags)
    plain = all(0 <= lag <= FAR_LAG for lag in lags)
    return live, plain


def _first_live(piece, half, lo):
    return not any(_block_kind(t, half)[0] for t in range(lo // PIECE, piece))


def _unit_blocks(piece):
    for head in range(2):
        for half in range(QW // HALF_W):
            live, plain = _block_kind(piece, half)
            if live:
                start = head * QW + half * HALF_W
                yield head, half, plain, slice(start, start + HALF_W)


def _score_steps(pair, quad, lo, qbuf, kbuf, bias_t, s_all, mx_all):
    s_ref, mx_ref = s_all.at[quad], mx_all.at[quad]
    k0 = quad * QW
    lane = lax.broadcasted_iota(jnp.int32, (1, PAIR_W), 1)
    qq = qbuf[pair, k0:k0 + QW, :]
    zero = jnp.zeros_like(qq)
    qm = jnp.concatenate([jnp.where(lane < HEAD_DIM, qq, zero),
                          jnp.where(lane >= HEAD_DIM, qq, zero)], axis=0)

    def step(r0, r1):
        s = _dot_nt(kbuf[pair, k0 + r0:k0 + r1, :], qm)
        running = {}
        for piece in range(r0 // PIECE, r1 // PIECE):
            r = piece * PIECE
            for head, half, plain, lanes in _unit_blocks(piece):
                sh = s[r - r0:r - r0 + PIECE, lanes]
                if not plain:
                    sh = sh + bias_t[2 * pair + head, r:r + PIECE,
                                     half * HALF_W:(half + 1) * HALF_W]
                s_ref[r:r + PIECE, lanes] = sh
                m = jnp.max(sh.reshape(PIECE // SUBLANES, SUBLANES, HALF_W), axis=0)
                key = (lanes.start, half)
                if key in running:
                    running[key] = jnp.maximum(running[key], m)
                elif _first_live(piece, half, lo):
                    running[key] = m
                else:
                    running[key] = jnp.maximum(mx_ref[:, lanes], m)
        for (start, _), m in running.items():
            mx_ref[:, start:start + HALF_W] = m

    starts = range(lo, KW, SCORE_ROWS)
    return [lambda r0=r0: step(r0, min(r0 + SCORE_ROWS, KW)) for r0 in starts]


def _softmax_steps(quad, lo, s_all, mx_all, p_all, sum_all):
    s_ref, mx_ref, p_ref, sum_ref = s_all.at[quad], mx_all.at[quad], p_all.at[quad], sum_all.at[quad]
    mx = jnp.max(mx_ref[...], axis=0, keepdims=True)

    def step(r0, r1):
        running = {}
        for piece in range(r0 // PIECE, r1 // PIECE):
            r = piece * PIECE
            for _, half, _, lanes in _unit_blocks(piece):
                e = jnp.exp2(s_ref[r:r + PIECE, lanes] - mx[:, lanes])
                p_ref[r:r + PIECE, lanes] = e.astype(jnp.bfloat16)
                part = jnp.sum(e.reshape(PIECE // SUBLANES, SUBLANES, HALF_W), axis=0)
                key = (lanes.start, half)
                if key in running:
                    running[key] = running[key] + part
                elif _first_live(piece, half, lo):
                    running[key] = part
                else:
                    running[key] = sum_ref[:, lanes] + part
        for (start, _), part in running.items():
            sum_ref[:, start:start + HALF_W] = part

    starts = range(lo, KW, SOFTMAX_ROWS)
    return [lambda r0=r0: step(r0, min(r0 + SOFTMAX_ROWS, KW)) for r0 in starts]


def _zero_dead_blocks(p_all):
    p_all[...] = jnp.zeros(p_all.shape, jnp.bfloat16)


def _weighted_values(pair, quad, lo, vtbuf, p_all, sum_all, out_t):
    row0 = pl.multiple_of(pair * PAIR_W, PAIR_W)
    k0 = quad * QW
    denom = jnp.sum(sum_all[quad], axis=0, keepdims=True)
    o_t = _dot(vtbuf[pl.ds(row0, PAIR_W), k0 + lo:k0 + KW], p_all[quad, lo:KW, :])
    inv = 1.0 / denom
    for head in range(2):
        rows = slice(head * HEAD_DIM, (head + 1) * HEAD_DIM)
        lanes = slice(head * QW, (head + 1) * QW)
        out_t[pl.ds(row0 + head * HEAD_DIM, HEAD_DIM), k0:k0 + QW] = o_t[rows, lanes] * inv[:, lanes]


def _interleave(a_steps, b_steps):
    done = 0
    for t, a_step in enumerate(a_steps):
        a_step()
        upto = (t + 1) * len(b_steps) // len(a_steps)
        for b_step in b_steps[done:upto]:
            b_step()
        done = upto


def _load_weight_bf16(w_hbm, dst, stage, sem, after_chunk=None):
    chunk_rows = stage.shape[1]
    n_chunks = w_hbm.shape[0] // chunk_rows

    def copy(c):
        rows = slice(c * chunk_rows, (c + 1) * chunk_rows)
        return pltpu.make_async_copy(w_hbm.at[rows, :], stage.at[c % 2], sem.at[c % 2])

    copy(0).start()
    for c in range(n_chunks):
        if c + 1 < n_chunks:
            copy(c + 1).start()
        copy(c).wait()
        rows = slice(c * chunk_rows, (c + 1) * chunk_rows)
        dst[rows, :] = stage[c % 2].astype(jnp.bfloat16)
        if after_chunk is not None:
            after_chunk(c, rows, stage.at[c % 2])


def _alternate(a_steps, b_steps):
    for t in range(max(len(a_steps), len(b_steps))):
        if t < len(a_steps):
            a_steps[t]()
        if t < len(b_steps):
            b_steps[t]()


def _mixer_kernel(x_hbm, g_pre_ref, w_in_hbm, dw_w_ref, dw_b_ref,
                  ln_g_ref, ln_b_ref, rel_ref, w_out_hbm, g_post_ref, o_hbm,
                  w_in_ref, w_vt_ref, w_out_ref, stage_in, stage_out, w_sem,
                  xin, xout, sem_in, sem_out,
                  hbuf, hshift, cbuf, qbuf, kbuf, vtbuf, bias_t, s_ref, mx_ref, p_ref, sum_ref,
                  out_t, mixbuf):
    b = pl.program_id(0)
    i = pl.program_id(1)
    n = b * pl.num_programs(1) + i
    n_tiles = pl.num_programs(0) * pl.num_programs(1)
    slot = n % 2
    c_v = 2 * CONV_WIDTH + 2 * ATTN_WIDTH

    def in_copy(tile, sl):
        return pltpu.make_async_copy(x_hbm.at[tile], xin.at[sl], sem_in.at[sl])

    def out_copy(tile, sl):
        return pltpu.make_async_copy(xout.at[sl], o_hbm.at[tile], sem_out.at[sl])

    @pl.when(n == 0)
    def _():
        in_copy(0, 0).start()

    @pl.when(n + 1 < n_tiles)
    def _():
        in_copy(n + 1, 1 - slot).start()

    @pl.when((b == 0) & (i == 0))
    def _():
        def value_weights_transposed(c, rows, staged):
            w_vt_ref[:, rows] = staged[:, c_v:c_v + ATTN_WIDTH].T.astype(jnp.bfloat16)

        _load_weight_bf16(w_in_hbm, w_in_ref, stage_in, w_sem, value_weights_transposed)
        _load_weight_bf16(w_out_hbm, w_out_ref, stage_out, w_sem)
        _build_bias(rel_ref, bias_t)
        _zero_dead_blocks(p_ref)

    @pl.when(i == 0)
    def _():
        hbuf[0:HALO_H, :] = jnp.zeros((HALO_H, CONV_WIDTH), jnp.float32)
        kbuf[:, 0:HALO_KV, :] = jnp.zeros((N_PAIRS, HALO_KV, PAIR_W), jnp.bfloat16)
        vtbuf[:, 0:HALO_KV] = jnp.zeros((ATTN_WIDTH, HALO_KV), jnp.bfloat16)

    @pl.when(n >= 2)
    def _():
        out_copy(n - 2, slot).wait()

    in_copy(n, slot).wait()
    x_ref = xin.at[slot]
    o_ref = xout.at[slot]

    c_q = 2 * CONV_WIDTH
    c_k = c_q + ATTN_WIDTH

    def conv_block(base):
        acc = jnp.broadcast_to(dw_b_ref[...][None], (CONV_RB // SUBLANES, SUBLANES, CONV_WIDTH))
        for j in range(CONV_KERNEL):
            off = HALO_H - (CONV_KERNEL - 1) + j
            r = off % SUBLANES
            rows = slice(base + off - r, base + off - r + CONV_RB)
            tap = hbuf[rows, :] if r == 0 else hshift[r - 1, rows, :]
            acc = acc + dw_w_ref[j][None] * tap.reshape(acc.shape)
        cbuf[base:base + CONV_RB, :] = acc.reshape(CONV_RB, CONV_WIDTH)

    pending = []

    def project_steps(blk):
        lo_row, hi_row = blk * PROJ_RB, (blk + 1) * PROJ_RB
        rows = slice(lo_row, hi_row)
        state = {}

        def norm_and_value():
            state["u"] = _rms(x_ref[rows, :], g_pre_ref[...]).astype(jnp.bfloat16)
            state["a_val"] = _dot(state["u"], w_in_ref[:, 0:CONV_WIDTH])

        def gate():
            a_gate = _dot(state["u"], w_in_ref[:, CONV_WIDTH:c_q])
            hbuf[HALO_H + lo_row:HALO_H + hi_row, :] = state["a_val"] * jax.nn.sigmoid(a_gate)

        def queries():
            q = _dot(state["u"], w_in_ref[:, c_q:c_k]) * (HEAD_DIM ** -0.5 * LOG2E)
            q = q.astype(jnp.bfloat16)
            for p in range(N_PAIRS):
                qbuf[p, rows, :] = q[:, p * PAIR_W:(p + 1) * PAIR_W]

        def keys_values():
            k = _dot(state["u"], w_in_ref[:, c_k:c_k + ATTN_WIDTH]).astype(jnp.bfloat16)
            for p in range(N_PAIRS):
                kbuf[p, HALO_KV + lo_row:HALO_KV + hi_row, :] = k[:, p * PAIR_W:(p + 1) * PAIR_W]
            pending.append(state["u"])
            if hi_row % VT_COLS == 0:
                u_wide = pending[0] if len(pending) == 1 else jnp.concatenate(pending, axis=0)
                vtbuf[:, HALO_KV + hi_row - VT_COLS:HALO_KV + hi_row] = (
                    _dot_nt(w_vt_ref[...], u_wide).astype(jnp.bfloat16))
                pending.clear()

        return [norm_and_value, gate, queries, keys_values]

    def mix_steps(blk):
        lo_row, hi_row = blk * PROJ_RB, (blk + 1) * PROJ_RB
        rows = slice(lo_row, hi_row)

        def shifted_copies():
            sh_lo = 0 if blk == 0 else lo_row + HSHIFT_LEAD
            sh_hi = hi_row + HSHIFT_LEAD
            for r in range(1, SUBLANES):
                hshift[r - 1, sh_lo:sh_hi, :] = hbuf[sh_lo + r:sh_hi + r, :]

        def norm_swish():
            c = cbuf[rows, :]
            mu = jnp.mean(c, axis=-1, keepdims=True)
            xc = c - mu
            var = jnp.mean(xc * xc, axis=-1, keepdims=True)
            y = xc * lax.rsqrt(var + EPS) * ln_g_ref[...] + ln_b_ref[...]
            y = y * jax.nn.sigmoid(y)
            mixbuf[rows, 0:CONV_WIDTH] = y.astype(jnp.bfloat16)

        convs = [lambda base=lo_row + rb * CONV_RB: conv_block(base)
                 for rb in range(PROJ_RB // CONV_RB)]
        return [shifted_copies] + convs + [norm_swish]

    n_blocks = TM_MIX // PROJ_RB
    for step in project_steps(0):
        step()
    for blk in range(n_blocks):
        nxt = project_steps(blk + 1) if blk + 1 < n_blocks else []
        _alternate(nxt, mix_steps(blk))

    def attn_loop(first_tile):
        lo = [max(0, HALO_KV - quad * QW) if first_tile else 0 for quad in range(N_QUADS)]

        def scores(pair, quad):
            return _score_steps(pair, quad, lo[quad], qbuf, kbuf, bias_t, s_ref, mx_ref)

        for step in scores(0, 0):
            step()

        def body(pair, carry):
            for quad in range(N_QUADS):
                if quad + 1 < N_QUADS:
                    nxt = scores(pair, quad + 1)
                else:
                    nxt = scores(jnp.minimum(pair + 1, N_PAIRS - 1), 0)
                _interleave(nxt, _softmax_steps(quad, lo[quad], s_ref, mx_ref, p_ref, sum_ref))
                _weighted_values(pair, quad, lo[quad], vtbuf, p_ref, sum_ref, out_t)
            return carry
        lax.fori_loop(0, N_PAIRS, body, 0)

    @pl.when(i == 0)
    def _():
        attn_loop(True)

    @pl.when(i > 0)
    def _():
        attn_loop(False)

    for blk in range(TM_MIX // OUT_RB):
        rows = slice(blk * OUT_RB, (blk + 1) * OUT_RB)
        attn = out_t[:, rows].T.astype(jnp.bfloat16)
        mixed = _dot(jnp.concatenate([mixbuf[rows, 0:CONV_WIDTH], attn], axis=1), w_out_ref[...])
        o_ref[rows, :] = x_ref[rows, :] + _rms(mixed, g_post_ref[...])

    out_copy(n, slot).start()

    hbuf[0:HALO_H, :] = hbuf[TM_MIX:TM_MIX + HALO_H, :]
    kbuf[:, 0:HALO_KV, :] = kbuf[:, TM_MIX:TM_MIX + HALO_KV, :]
    vtbuf[:, 0:HALO_KV] = vtbuf[:, TM_MIX:TM_MIX + HALO_KV]

    @pl.when(n == n_tiles - 1)
    def _():
        @pl.when(n >= 1)
        def _():
            out_copy(n - 1, 1 - slot).wait()
        out_copy(n, slot).wait()


def _ffn_kernel(h_hbm, g_pre_ref, w_up_hbm, dw_w_ref, dw_b_ref, w_down_hbm, g_post_ref, o_hbm,
                w_up_ref, w_down_ref, stage_up, stage_down, w_sem,
                xin, xout, sem_in, sem_out, carry, gbuf, vbuf, actbuf):
    i = pl.program_id(1)
    n = pl.program_id(0) * pl.num_programs(1) + i
    n_tiles = pl.num_programs(0) * pl.num_programs(1)
    slot = n % 2

    def in_copy(tile, sl, s):
        return pltpu.make_async_copy(h_hbm.at[tile, s], xin.at[sl, :, s, :], sem_in.at[sl])

    def out_copy(tile, sl, s):
        return pltpu.make_async_copy(xout.at[sl, :, s, :], o_hbm.at[tile, s], sem_out.at[sl])

    @pl.when(n == 0)
    def _():
        for s in range(SUBLANES):
            in_copy(0, 0, s).start()
        _load_weight_bf16(w_up_hbm, w_up_ref, stage_up, w_sem)
        _load_weight_bf16(w_down_hbm, w_down_ref, stage_down, w_sem)

    @pl.when(n + 1 < n_tiles)
    def _():
        for s in range(SUBLANES):
            in_copy(n + 1, 1 - slot, s).start()

    @pl.when(i == 0)
    def _():
        carry[...] = jnp.zeros(carry.shape, jnp.float32)

    @pl.when(n >= 2)
    def _():
        for s in range(SUBLANES):
            out_copy(n - 2, slot, s).wait()

    for s in range(SUBLANES):
        in_copy(n, slot, s).wait()

    xt = xin[slot].reshape(TM_FFN, D_MODEL)
    u = _rms(xt, g_pre_ref[...]).astype(jnp.bfloat16)
    first_sublane = lax.broadcasted_iota(jnp.int32, (SUBLANES, FFN_CW), 0) == 0

    def up_conv(buf, cols):
        hc = _dot(u, w_up_ref[:, cols])
        for k in (1, 2):
            last = hc[TM_FFN - k * SUBLANES:TM_FFN - (k - 1) * SUBLANES, :]
            prev = carry[(2 - k) * SUBLANES:(3 - k) * SUBLANES, cols]
            buf[(2 - k) * SUBLANES:(3 - k) * SUBLANES, :] = jnp.where(
                first_sublane, pltpu.roll(prev, 1, 0), pltpu.roll(last, 1, 0))
            carry[(2 - k) * SUBLANES:(3 - k) * SUBLANES, cols] = last
        buf[FFN_HALO:FFN_HALO + TM_FFN, :] = hc
        y = dw_b_ref[:, cols] + dw_w_ref[2:3, cols] * hc
        y = y + dw_w_ref[1:2, cols] * buf[SUBLANES:SUBLANES + TM_FFN, :]
        y = y + dw_w_ref[0:1, cols] * buf[0:TM_FFN, :]
        return y

    for c in range(D_FF // FFN_CW):
        gate = up_conv(gbuf.at[c % 2], slice(c * FFN_CW, (c + 1) * FFN_CW))
        val = up_conv(vbuf.at[c % 2], slice(D_FF + c * FFN_CW, D_FF + (c + 1) * FFN_CW))
        actbuf[:, c * FFN_CW:(c + 1) * FFN_CW] = (jax.nn.gelu(gate) * val).astype(jnp.bfloat16)
    for blk in range(TM_FFN // FFN_DOWN_RB):
        rows = slice(blk * FFN_DOWN_RB, (blk + 1) * FFN_DOWN_RB)
        slabs = slice(blk * FFN_DOWN_RB // SUBLANES, (blk + 1) * FFN_DOWN_RB // SUBLANES)
        f = _dot(actbuf[rows, :], w_down_ref[...])
        y = xt[rows, :] + _rms(f, g_post_ref[...])
        xout[slot, slabs] = y.reshape(FFN_DOWN_RB // SUBLANES, SUBLANES, D_MODEL)
    for s in range(SUBLANES):
        out_copy(n, slot, s).start()

    @pl.when(n == n_tiles - 1)
    def _():
        @pl.when(n >= 1)
        def _():
            for s in range(SUBLANES):
                out_copy(n - 1, 1 - slot, s).wait()
        for s in range(SUBLANES):
            out_copy(n, slot, s).wait()


def _rel_distance_row(rel_table):
    h = rel_table.shape[0]
    far = rel_table[:, 2 * MAX_REL:2 * MAX_REL + 1]
    n_far = HALO_KV - MAX_REL + 1
    near = jnp.flip(rel_table, axis=1)[:, 1:]
    n_tail = ROLL_W - n_far - near.shape[1]
    return jnp.concatenate([jnp.broadcast_to(far, (h, n_far)), near,
                            jnp.broadcast_to(far, (h, n_tail))], axis=1)


def _const_spec(shape):
    return pl.BlockSpec(shape, lambda b, i: (0,) * len(shape), pipeline_mode=pl.Buffered(1))


def _mixer(x, g_pre, w_in, dw_w, dw_b, ln_g, ln_b, rel_row, w_out, g_post):
    B, T, D = x.shape
    n_t = T // TM_MIX
    hbm_spec = pl.BlockSpec(memory_space=pl.ANY)
    consts = (g_pre, w_in, dw_w, dw_b, ln_g, ln_b, rel_row, w_out, g_post)
    const_specs = [hbm_spec if c is w_in or c is w_out else _const_spec(c.shape) for c in consts]
    out = pl.pallas_call(
        _mixer_kernel,
        grid=(B, n_t),
        in_specs=[hbm_spec] + const_specs,
        out_specs=hbm_spec,
        out_shape=jax.ShapeDtypeStruct((B * n_t, TM_MIX, D), x.dtype),
        scratch_shapes=[
            pltpu.VMEM(w_in.shape, jnp.bfloat16),
            pltpu.VMEM((ATTN_WIDTH, D), jnp.bfloat16),
            pltpu.VMEM(w_out.shape, jnp.bfloat16),
            pltpu.VMEM((2, W_IN_CHUNK, w_in.shape[1]), jnp.float32),
            pltpu.VMEM((2, W_OUT_CHUNK, w_out.shape[1]), jnp.float32),
            pltpu.SemaphoreType.DMA((2,)),
            pltpu.VMEM((2, TM_MIX, D), jnp.float32),
            pltpu.VMEM((2, TM_MIX, D), jnp.float32),
            pltpu.SemaphoreType.DMA((2,)),
            pltpu.SemaphoreType.DMA((2,)),
            pltpu.VMEM((HALO_H + TM_MIX, CONV_WIDTH), jnp.float32),
            pltpu.VMEM((SUBLANES - 1, HSHIFT_ROWS, CONV_WIDTH), jnp.float32),
            pltpu.VMEM((TM_MIX, CONV_WIDTH), jnp.float32),
            pltpu.VMEM((N_PAIRS, TM_MIX, PAIR_W), jnp.bfloat16),
            pltpu.VMEM((N_PAIRS, HALO_KV + TM_MIX, PAIR_W), jnp.bfloat16),
            pltpu.VMEM((ATTN_WIDTH, HALO_KV + TM_MIX), jnp.bfloat16),
            pltpu.VMEM((N_HEADS, KW, QW), jnp.float32),
            pltpu.VMEM((N_QUADS, KW, 2 * QW), jnp.float32),
            pltpu.VMEM((N_QUADS, SUBLANES, 2 * QW), jnp.float32),
            pltpu.VMEM((N_QUADS, KW, 2 * QW), jnp.bfloat16),
            pltpu.VMEM((N_QUADS, SUBLANES, 2 * QW), jnp.float32),
            pltpu.VMEM((ATTN_WIDTH, TM_MIX), jnp.float32),
            pltpu.VMEM((TM_MIX, CONV_WIDTH + ATTN_WIDTH), jnp.bfloat16),
        ],
        compiler_params=pltpu.CompilerParams(
            dimension_semantics=("arbitrary", "arbitrary"), vmem_limit_bytes=VMEM_LIMIT),
        name="mixer",
    )(x.reshape(B * n_t, TM_MIX, D), *consts)
    return out.reshape(B, T, D)


def _ffn(h, g_pre, w_up, dw_w, dw_b, w_down, g_post):
    B, T, D = h.shape
    n_t = T // TM_FFN
    tiles = (B * n_t, SUBLANES, FFN_SEG, D)
    hbm_spec = pl.BlockSpec(memory_space=pl.ANY)
    out = pl.pallas_call(
        _ffn_kernel,
        grid=(B, n_t),
        in_specs=[hbm_spec,
                  _const_spec((1, D)), hbm_spec, _const_spec(dw_w.shape),
                  _const_spec((1, 2 * D_FF)), hbm_spec, _const_spec((1, D))],
        out_specs=hbm_spec,
        out_shape=jax.ShapeDtypeStruct(tiles, h.dtype),
        scratch_shapes=[
            pltpu.VMEM(w_up.shape, jnp.bfloat16),
            pltpu.VMEM(w_down.shape, jnp.bfloat16),
            pltpu.VMEM((2, W_UP_CHUNK, w_up.shape[1]), jnp.float32),
            pltpu.VMEM((2, W_DOWN_CHUNK, w_down.shape[1]), jnp.float32),
            pltpu.SemaphoreType.DMA((2,)),
            pltpu.VMEM((2, FFN_SEG, SUBLANES, D), jnp.float32),
            pltpu.VMEM((2, FFN_SEG, SUBLANES, D), jnp.float32),
            pltpu.SemaphoreType.DMA((2,)),
            pltpu.SemaphoreType.DMA((2,)),
            pltpu.VMEM((FFN_HALO, 2 * D_FF), jnp.float32),
            pltpu.VMEM((2, FFN_HALO + TM_FFN, FFN_CW), jnp.float32),
            pltpu.VMEM((2, FFN_HALO + TM_FFN, FFN_CW), jnp.float32),
            pltpu.VMEM((TM_FFN, D_FF), jnp.bfloat16),
        ],
        compiler_params=pltpu.CompilerParams(
            dimension_semantics=("arbitrary", "arbitrary"), vmem_limit_bytes=VMEM_LIMIT),
        name="ffn",
    )(h.reshape(tiles), g_pre, w_up, dw_w, dw_b, w_down, g_post)
    return out.reshape(B, T, D)


def kernel(x, norm_mix_pre, w_in, conv_dw_w, conv_dw_b, conv_ln_g, conv_ln_b, rel_bias, w_out,
           norm_mix_post, norm_ffn_pre, w_up, ffn_dw_w, ffn_dw_b, w_down, norm_ffn_post):
    h = x
    for l in range(norm_mix_pre.shape[0]):
        h = _mixer(h, norm_mix_pre[l][None], w_in[l],
                   jnp.broadcast_to(conv_dw_w[l][:, None, :], (CONV_KERNEL, SUBLANES, CONV_WIDTH)),
                   conv_dw_b[l][None], conv_ln_g[l][None], conv_ln_b[l][None],
                   _rel_distance_row(rel_bias[l]), w_out[l], norm_mix_post[l][None])
        h = _ffn(h, norm_ffn_pre[l][None], w_up[l], ffn_dw_w[l],
                 ffn_dw_b[l][None], w_down[l], norm_ffn_post[l][None])
    return h
```

```python
import jax
import jax.numpy as jnp
from jax import lax
from jax.experimental import pallas as pl
from jax.experimental.pallas import tpu as pltpu

D_MODEL = 1024
CHUNK = 64
N_LEFT_CHUNKS = 8
CONV_WIDTH = 512
ATTN_WIDTH = 512
HEAD_DIM = 64
N_HEADS = ATTN_WIDTH // HEAD_DIM
N_PAIRS = N_HEADS // 2
PAIR_W = 2 * HEAD_DIM
CONV_KERNEL = 31
MAX_REL = 128
D_FF = 2816
FFN_CONV_KERNEL = 3
EPS = 1e-6
NEG_INF = -1e30

TM_MIX = 512
HALO_H = 32
HALO_KV = N_LEFT_CHUNKS * CHUNK
CONV_RB = 32
OUT_RB = 256
PROJ_RB = 256
VT_COLS = 256
SUBLANES = 8
HSHIFT_LEAD = HALO_H - SUBLANES
HSHIFT_ROWS = HALO_H + TM_MIX - SUBLANES
QUAD = 4
QW = QUAD * CHUNK
KW = HALO_KV + QW
N_QUADS = TM_MIX // QW
PIECE = 128
SCORE_ROWS = 384
SOFTMAX_ROWS = 256
HALF_W = 128
FAR_LAG = (HALO_KV - MAX_REL) // CHUNK - 1
LOG2E = 1.4426950408889634
ROLL_W = 1024
TM_FFN = 512
FFN_CW = 256
FFN_DOWN_RB = 256
FFN_SEG = TM_FFN // SUBLANES
FFN_HALO = (FFN_CONV_KERNEL - 1) * SUBLANES
W_IN_CHUNK = 128
W_OUT_CHUNK = 256
W_UP_CHUNK = 128
W_DOWN_CHUNK = 352
VMEM_LIMIT = 56 * 1024 * 1024


def _rms(xf, g):
    return xf * lax.rsqrt(jnp.mean(xf * xf, axis=-1, keepdims=True) + EPS) * g


def _dot(a, b):
    return jnp.dot(a, b, preferred_element_type=jnp.float32)


def _dot_nt(a, b):
    return lax.dot_general(a, b, (((1,), (1,)), ((), ())), preferred_element_type=jnp.float32)


def _build_bias(rel_ref, bias_t):
    r_idx = lax.broadcasted_iota(jnp.int32, (KW, QW), 0) // CHUNK
    c_idx = lax.broadcasted_iota(jnp.int32, (KW, QW), 1) // CHUNK
    visible = (r_idx >= c_idx) & (r_idx <= c_idx + N_LEFT_CHUNKS)
    for h in range(N_HEADS):
        row = jnp.broadcast_to(rel_ref[h:h + 1, :], (QW, ROLL_W))
        nat = pltpu.roll(row, 0, 1, stride=1, stride_axis=0)[:, 0:KW]
        rel_to_far = (nat.T - rel_ref[h:h + 1, 0:1]) * LOG2E
        masked = jnp.where(visible, rel_to_far, NEG_INF)
        for half in range(QW // HALF_W):
            bias_t[h, half] = masked[:, half * HALF_W:(half + 1) * HALF_W]


def _block_kind(piece, half):
    lags = [kc - qc
            for kc in range(piece * PIECE // CHUNK, (piece + 1) * PIECE // CHUNK)
            for qc in range(half * HALF_W // CHUNK, (half + 1) * HALF_W // CHUNK)]
    live = any(0 <= lag <= N_LEFT_CHUNKS for lag in lags)
    plain = all(0 <= lag <= FAR_LAG for lag in lags)
    return live, plain


def _first_live(piece, half, lo):
    return not any(_block_kind(t, half)[0] for t in range(lo // PIECE, piece))


def _unit_blocks(piece):
    for head in range(2):
        for half in range(QW // HALF_W):
            live, plain = _block_kind(piece, half)
            if live:
                start = head * QW + half * HALF_W
                yield head, half, plain, slice(start, start + HALF_W)


N_LANE_BLOCKS = 2 * QW // HALF_W


def _score_steps(pair, quad, lo, qbuf, kbuf, bias_t, s_all, mx_all):
    s_ref, mx_ref = s_all.at[quad], mx_all.at[quad]
    k0 = quad * QW
    lane = lax.broadcasted_iota(jnp.int32, (1, PAIR_W), 1)
    qq = qbuf[pair, k0:k0 + QW, :]
    zero = jnp.zeros_like(qq)
    qm = jnp.concatenate([jnp.where(lane < HEAD_DIM, qq, zero),
                          jnp.where(lane >= HEAD_DIM, qq, zero)], axis=0)

    def step(r0, r1):
        s = _dot_nt(kbuf[pair, k0 + r0:k0 + r1, :], qm)
        running = {}
        for piece in range(r0 // PIECE, r1 // PIECE):
            r = piece * PIECE
            for head, half, plain, lanes in _unit_blocks(piece):
                sh = s[r - r0:r - r0 + PIECE, lanes]
                if not plain:
                    sh = sh + bias_t[2 * pair + head, half, r:r + PIECE, :]
                s_ref[lanes.start // HALF_W, r:r + PIECE, :] = sh
                m = jnp.max(sh.reshape(PIECE // SUBLANES, SUBLANES, HALF_W), axis=0)
                key = (lanes.start, half)
                if key in running:
                    running[key] = jnp.maximum(running[key], m)
                elif _first_live(piece, half, lo):
                    running[key] = m
                else:
                    running[key] = jnp.maximum(mx_ref[:, lanes], m)
        for (start, _), m in running.items():
            mx_ref[:, start:start + HALF_W] = m

    starts = range(lo, KW, SCORE_ROWS)
    return [lambda r0=r0: step(r0, min(r0 + SCORE_ROWS, KW)) for r0 in starts]


def _softmax_steps(quad, lo, s_all, mx_all, p_all, sum_all):
    s_ref, mx_ref, p_ref, sum_ref = s_all.at[quad], mx_all.at[quad], p_all.at[quad], sum_all.at[quad]
    mx = jnp.max(mx_ref[...], axis=0, keepdims=True)

    def step(r0, r1):
        running = {}
        for piece in range(r0 // PIECE, r1 // PIECE):
            r = piece * PIECE
            for _, half, _, lanes in _unit_blocks(piece):
                blk = lanes.start // HALF_W
                e = jnp.exp2(s_ref[blk, r:r + PIECE, :] - mx[:, lanes])
                p_ref[blk, r:r + PIECE, :] = e.astype(jnp.bfloat16)
                part = jnp.sum(e.reshape(PIECE // SUBLANES, SUBLANES, HALF_W), axis=0)
                key = (lanes.start, half)
                if key in running:
                    running[key] = running[key] + part
                elif _first_live(piece, half, lo):
                    running[key] = part
                else:
                    running[key] = sum_ref[:, lanes] + part
        for (start, _), part in running.items():
            sum_ref[:, start:start + HALF_W] = part

    starts = range(lo, KW, SOFTMAX_ROWS)
    return [lambda r0=r0: step(r0, min(r0 + SOFTMAX_ROWS, KW)) for r0 in starts]


def _zero_dead_blocks(p_all):
    p_all[...] = jnp.zeros(p_all.shape, jnp.bfloat16)


def _weighted_values(pair, quad, lo, vtbuf, p_all, sum_all, out_t):
    row0 = pl.multiple_of(pair * PAIR_W, PAIR_W)
    k0 = quad * QW
    denom = jnp.sum(sum_all[quad], axis=0, keepdims=True)
    p = jnp.concatenate([p_all[quad, blk, lo:KW, :] for blk in range(N_LANE_BLOCKS)], axis=1)
    o_t = _dot(vtbuf[pl.ds(row0, PAIR_W), k0 + lo:k0 + KW], p)
    inv = 1.0 / denom
    for head in range(2):
        rows = slice(head * HEAD_DIM, (head + 1) * HEAD_DIM)
        lanes = slice(head * QW, (head + 1) * QW)
        out_t[pl.ds(row0 + head * HEAD_DIM, HEAD_DIM), k0:k0 + QW] = o_t[rows, lanes] * inv[:, lanes]


def _interleave(a_steps, b_steps):
    done = 0
    for t, a_step in enumerate(a_steps):
        a_step()
        upto = (t + 1) * len(b_steps) // len(a_steps)
        for b_step in b_steps[done:upto]:
            b_step()
        done = upto


def _load_weight_bf16(w_hbm, dst, stage, sem, after_chunk=None):
    chunk_rows = stage.shape[1]
    n_chunks = w_hbm.shape[0] // chunk_rows

    def copy(c):
        rows = slice(c * chunk_rows, (c + 1) * chunk_rows)
        return pltpu.make_async_copy(w_hbm.at[rows, :], stage.at[c % 2], sem.at[c % 2])

    copy(0).start()
    for c in range(n_chunks):
        if c + 1 < n_chunks:
            copy(c + 1).start()
        copy(c).wait()
        rows = slice(c * chunk_rows, (c + 1) * chunk_rows)
        dst[rows, :] = stage[c % 2].astype(jnp.bfloat16)
        if after_chunk is not None:
            after_chunk(c, rows, stage.at[c % 2])


def _alternate(a_steps, b_steps):
    for t in range(max(len(a_steps), len(b_steps))):
        if t < len(a_steps):
            a_steps[t]()
        if t < len(b_steps):
            b_steps[t]()


def _mixer_kernel(x_ref, g_pre_ref, w_in_hbm, dw_w_ref, dw_b_ref,
                  ln_g_ref, ln_b_ref, rel_ref, w_out_hbm, g_post_ref, o_ref,
                  w_in_ref, w_vt_ref, w_out_ref, stage_in, stage_out, w_sem,
                  hbuf, hshift, cbuf, qbuf, kbuf, vtbuf, bias_t, s_ref, mx_ref, p_ref, sum_ref,
                  out_t, mixbuf):
    b = pl.program_id(0)
    i = pl.program_id(1)
    c_v = 2 * CONV_WIDTH + 2 * ATTN_WIDTH

    @pl.when((b == 0) & (i == 0))
    def _():
        def value_weights_transposed(c, rows, staged):
            w_vt_ref[:, rows] = staged[:, c_v:c_v + ATTN_WIDTH].T.astype(jnp.bfloat16)

        _load_weight_bf16(w_in_hbm, w_in_ref, stage_in, w_sem, value_weights_transposed)
        _load_weight_bf16(w_out_hbm, w_out_ref, stage_out, w_sem)
        _build_bias(rel_ref, bias_t)
        _zero_dead_blocks(p_ref)

    @pl.when(i == 0)
    def _():
        hbuf[0:HALO_H, :] = jnp.zeros((HALO_H, CONV_WIDTH), jnp.float32)
        kbuf[:, 0:HALO_KV, :] = jnp.zeros((N_PAIRS, HALO_KV, PAIR_W), jnp.bfloat16)
        vtbuf[:, 0:HALO_KV] = jnp.zeros((ATTN_WIDTH, HALO_KV), jnp.bfloat16)

    c_q = 2 * CONV_WIDTH
    c_k = c_q + ATTN_WIDTH

    def conv_block(base):
        acc = jnp.broadcast_to(dw_b_ref[...][None], (CONV_RB // SUBLANES, SUBLANES, CONV_WIDTH))
        for j in range(CONV_KERNEL):
            off = HALO_H - (CONV_KERNEL - 1) + j
            r = off % SUBLANES
            rows = slice(base + off - r, base + off - r + CONV_RB)
            tap = hbuf[rows, :] if r == 0 else hshift[r - 1, rows, :]
            acc = acc + dw_w_ref[j][None] * tap.reshape(acc.shape)
        cbuf[base:base + CONV_RB, :] = acc.reshape(CONV_RB, CONV_WIDTH)

    pending = []

    def project_steps(blk):
        lo_row, hi_row = blk * PROJ_RB, (blk + 1) * PROJ_RB
        rows = slice(lo_row, hi_row)
        state = {}

        def norm_and_value():
            state["u"] = _rms(x_ref[rows, :], g_pre_ref[...]).astype(jnp.bfloat16)
            state["a_val"] = _dot(state["u"], w_in_ref[:, 0:CONV_WIDTH])

        def gate():
            a_gate = _dot(state["u"], w_in_ref[:, CONV_WIDTH:c_q])
            hbuf[HALO_H + lo_row:HALO_H + hi_row, :] = state["a_val"] * jax.nn.sigmoid(a_gate)

        def queries():
            q = _dot(state["u"], w_in_ref[:, c_q:c_k]) * (HEAD_DIM ** -0.5 * LOG2E)
            q = q.astype(jnp.bfloat16)
            for p in range(N_PAIRS):
                qbuf[p, rows, :] = q[:, p * PAIR_W:(p + 1) * PAIR_W]

        def keys_values():
            k = _dot(state["u"], w_in_ref[:, c_k:c_k + ATTN_WIDTH]).astype(jnp.bfloat16)
            for p in range(N_PAIRS):
                kbuf[p, HALO_KV + lo_row:HALO_KV + hi_row, :] = k[:, p * PAIR_W:(p + 1) * PAIR_W]
            pending.append(state["u"])
            if hi_row % VT_COLS == 0:
                u_wide = pending[0] if len(pending) == 1 else jnp.concatenate(pending, axis=0)
                vtbuf[:, HALO_KV + hi_row - VT_COLS:HALO_KV + hi_row] = (
                    _dot_nt(w_vt_ref[...], u_wide).astype(jnp.bfloat16))
                pending.clear()

        return [norm_and_value, gate, queries, keys_values]

    def mix_steps(blk):
        lo_row, hi_row = blk * PROJ_RB, (blk + 1) * PROJ_RB
        rows = slice(lo_row, hi_row)

        def shifted_copies():
            sh_lo = 0 if blk == 0 else lo_row + HSHIFT_LEAD
            sh_hi = hi_row + HSHIFT_LEAD
            for r in range(1, SUBLANES):
                hshift[r - 1, sh_lo:sh_hi, :] = hbuf[sh_lo + r:sh_hi + r, :]

        def norm_swish():
            c = cbuf[rows, :]
            mu = jnp.mean(c, axis=-1, keepdims=True)
            xc = c - mu
            var = jnp.mean(xc * xc, axis=-1, keepdims=True)
            y = xc * lax.rsqrt(var + EPS) * ln_g_ref[...] + ln_b_ref[...]
            y = y * jax.nn.sigmoid(y)
            mixbuf[rows, 0:CONV_WIDTH] = y.astype(jnp.bfloat16)

        convs = [lambda base=lo_row + rb * CONV_RB: conv_block(base)
                 for rb in range(PROJ_RB // CONV_RB)]
        return [shifted_copies] + convs + [norm_swish]

    n_blocks = TM_MIX // PROJ_RB
    for step in project_steps(0):
        step()
    for blk in range(n_blocks):
        nxt = project_steps(blk + 1) if blk + 1 < n_blocks else []
        _alternate(nxt, mix_steps(blk))

    def attn_loop(first_tile):
        lo = [max(0, HALO_KV - quad * QW) if first_tile else 0 for quad in range(N_QUADS)]

        def scores(pair, quad):
            return _score_steps(pair, quad, lo[quad], qbuf, kbuf, bias_t, s_ref, mx_ref)

        for step in scores(0, 0):
            step()

        def body(pair, carry):
            for quad in range(N_QUADS):
                if quad + 1 < N_QUADS:
                    nxt = scores(pair, quad + 1)
                else:
                    nxt = scores(jnp.minimum(pair + 1, N_PAIRS - 1), 0)
                _interleave(nxt, _softmax_steps(quad, lo[quad], s_ref, mx_ref, p_ref, sum_ref))
                _weighted_values(pair, quad, lo[quad], vtbuf, p_ref, sum_ref, out_t)
            return carry
        lax.fori_loop(0, N_PAIRS, body, 0)

    @pl.when(i == 0)
    def _():
        attn_loop(True)

    @pl.when(i > 0)
    def _():
        attn_loop(False)

    for blk in range(TM_MIX // OUT_RB):
        rows = slice(blk * OUT_RB, (blk + 1) * OUT_RB)
        attn = out_t[:, rows].T.astype(jnp.bfloat16)
        mixed = _dot(jnp.concatenate([mixbuf[rows, 0:CONV_WIDTH], attn], axis=1), w_out_ref[...])
        o_ref[rows, :] = x_ref[rows, :] + _rms(mixed, g_post_ref[...])

    hbuf[0:HALO_H, :] = hbuf[TM_MIX:TM_MIX + HALO_H, :]
    kbuf[:, 0:HALO_KV, :] = kbuf[:, TM_MIX:TM_MIX + HALO_KV, :]
    vtbuf[:, 0:HALO_KV] = vtbuf[:, TM_MIX:TM_MIX + HALO_KV]


def _ffn_kernel(h_hbm, g_pre_ref, w_up_hbm, dw_w_ref, dw_b_ref, w_down_hbm, g_post_ref, o_hbm,
                w_up_ref, w_down_ref, stage_up, stage_down, w_sem,
                xin, xout, sem_in, sem_out, carry, gbuf, vbuf, actbuf):
    i = pl.program_id(1)
    n = pl.program_id(0) * pl.num_programs(1) + i
    n_tiles = pl.num_programs(0) * pl.num_programs(1)
    slot = n % 2

    def in_copy(tile, sl, s):
        return pltpu.make_async_copy(h_hbm.at[tile, s], xin.at[sl, :, s, :], sem_in.at[sl])

    def out_copy(tile, sl, s):
        return pltpu.make_async_copy(xout.at[sl, :, s, :], o_hbm.at[tile, s], sem_out.at[sl])

    @pl.when(n == 0)
    def _():
        for s in range(SUBLANES):
            in_copy(0, 0, s).start()
        _load_weight_bf16(w_up_hbm, w_up_ref, stage_up, w_sem)
        _load_weight_bf16(w_down_hbm, w_down_ref, stage_down, w_sem)

    @pl.when(n + 1 < n_tiles)
    def _():
        for s in range(SUBLANES):
            in_copy(n + 1, 1 - slot, s).start()

    @pl.when(i == 0)
    def _():
        carry[...] = jnp.zeros(carry.shape, jnp.float32)

    @pl.when(n >= 2)
    def _():
        for s in range(SUBLANES):
            out_copy(n - 2, slot, s).wait()

    for s in range(SUBLANES):
        in_copy(n, slot, s).wait()

    xt = xin[slot].reshape(TM_FFN, D_MODEL)
    u = _rms(xt, g_pre_ref[...]).astype(jnp.bfloat16)
    first_sublane = lax.broadcasted_iota(jnp.int32, (SUBLANES, FFN_CW), 0) == 0

    def up_conv(buf, cols):
        hc = _dot(u, w_up_ref[:, cols])
        for k in (1, 2):
            last = hc[TM_FFN - k * SUBLANES:TM_FFN - (k - 1) * SUBLANES, :]
            prev = carry[(2 - k) * SUBLANES:(3 - k) * SUBLANES, cols]
            buf[(2 - k) * SUBLANES:(3 - k) * SUBLANES, :] = jnp.where(
                first_sublane, pltpu.roll(prev, 1, 0), pltpu.roll(last, 1, 0))
            carry[(2 - k) * SUBLANES:(3 - k) * SUBLANES, cols] = last
        buf[FFN_HALO:FFN_HALO + TM_FFN, :] = hc
        y = dw_b_ref[:, cols] + dw_w_ref[2:3, cols] * hc
        y = y + dw_w_ref[1:2, cols] * buf[SUBLANES:SUBLANES + TM_FFN, :]
        y = y + dw_w_ref[0:1, cols] * buf[0:TM_FFN, :]
        return y

    for c in range(D_FF // FFN_CW):
        gate = up_conv(gbuf.at[c % 2], slice(c * FFN_CW, (c + 1) * FFN_CW))
        val = up_conv(vbuf.at[c % 2], slice(D_FF + c * FFN_CW, D_FF + (c + 1) * FFN_CW))
        actbuf[:, c * FFN_CW:(c + 1) * FFN_CW] = (jax.nn.gelu(gate) * val).astype(jnp.bfloat16)
    for blk in range(TM_FFN // FFN_DOWN_RB):
        rows = slice(blk * FFN_DOWN_RB, (blk + 1) * FFN_DOWN_RB)
        slabs = slice(blk * FFN_DOWN_RB // SUBLANES, (blk + 1) * FFN_DOWN_RB // SUBLANES)
        f = _dot(actbuf[rows, :], w_down_ref[...])
        y = xt[rows, :] + _rms(f, g_post_ref[...])
        xout[slot, slabs] = y.reshape(FFN_DOWN_RB // SUBLANES, SUBLANES, D_MODEL)
    for s in range(SUBLANES):
        out_copy(n, slot, s).start()

    @pl.when(n == n_tiles - 1)
    def _():
        @pl.when(n >= 1)
        def _():
            for s in range(SUBLANES):
                out_copy(n - 1, 1 - slot, s).wait()
        for s in range(SUBLANES):
            out_copy(n, slot, s).wait()


def _rel_distance_row(rel_table):
    h = rel_table.shape[0]
    far = rel_table[:, 2 * MAX_REL:2 * MAX_REL + 1]
    n_far = HALO_KV - MAX_REL + 1
    near = jnp.flip(rel_table, axis=1)[:, 1:]
    n_tail = ROLL_W - n_far - near.shape[1]
    return jnp.concatenate([jnp.broadcast_to(far, (h, n_far)), near,
                            jnp.broadcast_to(far, (h, n_tail))], axis=1)


def _const_spec(shape):
    return pl.BlockSpec(shape, lambda b, i: (0,) * len(shape), pipeline_mode=pl.Buffered(1))


def _mixer(x, g_pre, w_in, dw_w, dw_b, ln_g, ln_b, rel_row, w_out, g_post):
    B, T, D = x.shape
    row_spec = pl.BlockSpec((None, TM_MIX, D), lambda b, i: (b, i, 0))
    hbm_spec = pl.BlockSpec(memory_space=pl.ANY)
    consts = (g_pre, w_in, dw_w, dw_b, ln_g, ln_b, rel_row, w_out, g_post)
    const_specs = [hbm_spec if c is w_in or c is w_out else _const_spec(c.shape) for c in consts]
    return pl.pallas_call(
        _mixer_kernel,
        grid=(B, T // TM_MIX),
        in_specs=[row_spec] + const_specs,
        out_specs=row_spec,
        out_shape=jax.ShapeDtypeStruct(x.shape, x.dtype),
        scratch_shapes=[
            pltpu.VMEM(w_in.shape, jnp.bfloat16),
            pltpu.VMEM((ATTN_WIDTH, D), jnp.bfloat16),
            pltpu.VMEM(w_out.shape, jnp.bfloat16),
            pltpu.VMEM((2, W_IN_CHUNK, w_in.shape[1]), jnp.float32),
            pltpu.VMEM((2, W_OUT_CHUNK, w_out.shape[1]), jnp.float32),
            pltpu.SemaphoreType.DMA((2,)),
            pltpu.VMEM((HALO_H + TM_MIX, CONV_WIDTH), jnp.float32),
            pltpu.VMEM((SUBLANES - 1, HSHIFT_ROWS, CONV_WIDTH), jnp.float32),
            pltpu.VMEM((TM_MIX, CONV_WIDTH), jnp.float32),
            pltpu.VMEM((N_PAIRS, TM_MIX, PAIR_W), jnp.bfloat16),
            pltpu.VMEM((N_PAIRS, HALO_KV + TM_MIX, PAIR_W), jnp.bfloat16),
            pltpu.VMEM((ATTN_WIDTH, HALO_KV + TM_MIX), jnp.bfloat16),
            pltpu.VMEM((N_HEADS, QW // HALF_W, KW, HALF_W), jnp.float32),
            pltpu.VMEM((N_QUADS, N_LANE_BLOCKS, KW, HALF_W), jnp.float32),
            pltpu.VMEM((N_QUADS, SUBLANES, 2 * QW), jnp.float32),
            pltpu.VMEM((N_QUADS, N_LANE_BLOCKS, KW, HALF_W), jnp.bfloat16),
            pltpu.VMEM((N_QUADS, SUBLANES, 2 * QW), jnp.float32),
            pltpu.VMEM((ATTN_WIDTH, TM_MIX), jnp.float32),
            pltpu.VMEM((TM_MIX, CONV_WIDTH + ATTN_WIDTH), jnp.bfloat16),
        ],
        compiler_params=pltpu.CompilerParams(
            dimension_semantics=("arbitrary", "arbitrary"), vmem_limit_bytes=VMEM_LIMIT),
        name="mixer",
    )(x, *consts)


def _ffn(h, g_pre, w_up, dw_w, dw_b, w_down, g_post):
    B, T, D = h.shape
    n_t = T // TM_FFN
    tiles = (B * n_t, SUBLANES, FFN_SEG, D)
    hbm_spec = pl.BlockSpec(memory_space=pl.ANY)
    out = pl.pallas_call(
        _ffn_kernel,
        grid=(B, n_t),
        in_specs=[hbm_spec,
                  _const_spec((1, D)), hbm_spec, _const_spec(dw_w.shape),
                  _const_spec((1, 2 * D_FF)), hbm_spec, _const_spec((1, D))],
        out_specs=hbm_spec,
        out_shape=jax.ShapeDtypeStruct(tiles, h.dtype),
        scratch_shapes=[
            pltpu.VMEM(w_up.shape, jnp.bfloat16),
            pltpu.VMEM(w_down.shape, jnp.bfloat16),
            pltpu.VMEM((2, W_UP_CHUNK, w_up.shape[1]), jnp.float32),
            pltpu.VMEM((2, W_DOWN_CHUNK, w_down.shape[1]), jnp.float32),
            pltpu.SemaphoreType.DMA((2,)),
            pltpu.VMEM((2, FFN_SEG, SUBLANES, D), jnp.float32),
            pltpu.VMEM((2, FFN_SEG, SUBLANES, D), jnp.float32),
            pltpu.SemaphoreType.DMA((2,)),
            pltpu.SemaphoreType.DMA((2,)),
            pltpu.VMEM((FFN_HALO, 2 * D_FF), jnp.float32),
            pltpu.VMEM((2, FFN_HALO + TM_FFN, FFN_CW), jnp.float32),
            pltpu.VMEM((2, FFN_HALO + TM_FFN, FFN_CW), jnp.float32),
            pltpu.VMEM((TM_FFN, D_FF), jnp.bfloat16),
        ],
        compiler_params=pltpu.CompilerParams(
            dimension_semantics=("arbitrary", "arbitrary"), vmem_limit_bytes=VMEM_LIMIT),
        name="ffn",
    )(h.reshape(tiles), g_pre, w_up, dw_w, dw_b, w_down, g_post)
    return out.reshape(B, T, D)


def kernel(x, norm_mix_pre, w_in, conv_dw_w, conv_dw_b, conv_ln_g, conv_ln_b, rel_bias, w_out,
           norm_mix_post, norm_ffn_pre, w_up, ffn_dw_w, ffn_dw_b, w_down, norm_ffn_post):
    h = x
    for l in range(norm_mix_pre.shape[0]):
        h = _mixer(h, norm_mix_pre[l][None], w_in[l],
                   jnp.broadcast_to(conv_dw_w[l][:, None, :], (CONV_KERNEL, SUBLANES, CONV_WIDTH)),
                   conv_dw_b[l][None], conv_ln_g[l][None], conv_ln_b[l][None],
                   _rel_distance_row(rel_bias[l]), w_out[l], norm_mix_post[l][None])
        h = _ffn(h, norm_ffn_pre[l][None], w_up[l], ffn_dw_w[l],
                 ffn_dw_b[l][None], w_down[l], norm_ffn_post[l][None])
    return h
```

```python
import jax
import jax.numpy as jnp
from jax import lax
from jax.experimental import pallas as pl
from jax.experimental.pallas import tpu as pltpu

D_MODEL = 1024
CHUNK = 64
N_LEFT_CHUNKS = 8
CONV_WIDTH = 512
ATTN_WIDTH = 512
HEAD_DIM = 64
N_HEADS = ATTN_WIDTH // HEAD_DIM
N_PAIRS = N_HEADS // 2
PAIR_W = 2 * HEAD_DIM
CONV_KERNEL = 31
MAX_REL = 128
D_FF = 2816
FFN_CONV_KERNEL = 3
EPS = 1e-6
NEG_INF = -1e30

TM_MIX = 512
HALO_H = 32
HALO_KV = N_LEFT_CHUNKS * CHUNK
CONV_RB = 32
OUT_RB = 256
PROJ_RB = 256
VT_COLS = 256
SUBLANES = 8
HSHIFT_LEAD = HALO_H - SUBLANES
HSHIFT_ROWS = HALO_H + TM_MIX - SUBLANES
QUAD = 4
QW = QUAD * CHUNK
KW = HALO_KV + QW
N_QUADS = TM_MIX // QW
assert OUT_RB == QW
PIECE = 128
SCORE_ROWS = 384
SOFTMAX_ROWS = 256
HALF_W = 128
FAR_LAG = (HALO_KV - MAX_REL) // CHUNK - 1
LOG2E = 1.4426950408889634
ROLL_W = 1024
TM_FFN = 512
FFN_CW = 256
FFN_DOWN_RB = 256
FFN_SEG = TM_FFN // SUBLANES
FFN_HALO = (FFN_CONV_KERNEL - 1) * SUBLANES
W_IN_CHUNK = 128
W_OUT_CHUNK = 256
W_UP_CHUNK = 128
W_DOWN_CHUNK = 352
VMEM_LIMIT = 56 * 1024 * 1024


def _rms(xf, g):
    return xf * lax.rsqrt(jnp.mean(xf * xf, axis=-1, keepdims=True) + EPS) * g


def _dot(a, b):
    return jnp.dot(a, b, preferred_element_type=jnp.float32)


def _dot_nt(a, b):
    return lax.dot_general(a, b, (((1,), (1,)), ((), ())), preferred_element_type=jnp.float32)


def _build_bias(rel_ref, bias_t):
    r_idx = lax.broadcasted_iota(jnp.int32, (KW, QW), 0) // CHUNK
    c_idx = lax.broadcasted_iota(jnp.int32, (KW, QW), 1) // CHUNK
    visible = (r_idx >= c_idx) & (r_idx <= c_idx + N_LEFT_CHUNKS)
    for h in range(N_HEADS):
        row = jnp.broadcast_to(rel_ref[h:h + 1, :], (QW, ROLL_W))
        nat = pltpu.roll(row, 0, 1, stride=1, stride_axis=0)[:, 0:KW]
        rel_to_far = (nat.T - rel_ref[h:h + 1, 0:1]) * LOG2E
        masked = jnp.where(visible, rel_to_far, NEG_INF)
        for half in range(QW // HALF_W):
            bias_t[h, half] = masked[:, half * HALF_W:(half + 1) * HALF_W]


def _block_kind(piece, half):
    lags = [kc - qc
            for kc in range(piece * PIECE // CHUNK, (piece + 1) * PIECE // CHUNK)
            for qc in range(half * HALF_W // CHUNK, (half + 1) * HALF_W // CHUNK)]
    live = any(0 <= lag <= N_LEFT_CHUNKS for lag in lags)
    plain = all(0 <= lag <= FAR_LAG for lag in lags)
    return live, plain


def _first_live(piece, half, lo):
    return not any(_block_kind(t, half)[0] for t in range(lo // PIECE, piece))


def _unit_blocks(piece):
    for head in range(2):
        for half in range(QW // HALF_W):
            live, plain = _block_kind(piece, half)
            if live:
                start = head * QW + half * HALF_W
                yield head, half, plain, slice(start, start + HALF_W)


N_LANE_BLOCKS = 2 * QW // HALF_W


def _score_steps(pair, quad, lo, qbuf, kbuf, bias_t, s_all, mx_all):
    s_ref, mx_ref = s_all.at[quad], mx_all.at[quad]
    k0 = quad * QW
    lane = lax.broadcasted_iota(jnp.int32, (1, PAIR_W), 1)
    qq = qbuf[pair, k0:k0 + QW, :]
    zero = jnp.zeros_like(qq)
    qm = jnp.concatenate([jnp.where(lane < HEAD_DIM, qq, zero),
                          jnp.where(lane >= HEAD_DIM, qq, zero)], axis=0)

    def step(r0, r1):
        s = _dot_nt(kbuf[pair, k0 + r0:k0 + r1, :], qm)
        running = {}
        for piece in range(r0 // PIECE, r1 // PIECE):
            r = piece * PIECE
            for head, half, plain, lanes in _unit_blocks(piece):
                sh = s[r - r0:r - r0 + PIECE, lanes]
                if not plain:
                    sh = sh + bias_t[2 * pair + head, half, r:r + PIECE, :]
                s_ref[lanes.start // HALF_W, r:r + PIECE, :] = sh
                m = jnp.max(sh.reshape(PIECE // SUBLANES, SUBLANES, HALF_W), axis=0)
                key = (lanes.start, half)
                if key in running:
                    running[key] = jnp.maximum(running[key], m)
                elif _first_live(piece, half, lo):
                    running[key] = m
                else:
                    running[key] = jnp.maximum(mx_ref[:, lanes], m)
        for (start, _), m in running.items():
            mx_ref[:, start:start + HALF_W] = m

    starts = range(lo, KW, SCORE_ROWS)
    return [lambda r0=r0: step(r0, min(r0 + SCORE_ROWS, KW)) for r0 in starts]


def _softmax_steps(quad, lo, s_all, mx_all, p_all, sum_all):
    s_ref, mx_ref, p_ref, sum_ref = s_all.at[quad], mx_all.at[quad], p_all.at[quad], sum_all.at[quad]
    mx = jnp.max(mx_ref[...], axis=0, keepdims=True)

    def step(r0, r1):
        running = {}
        for piece in range(r0 // PIECE, r1 // PIECE):
            r = piece * PIECE
            for _, half, _, lanes in _unit_blocks(piece):
                blk = lanes.start // HALF_W
                e = jnp.exp2(s_ref[blk, r:r + PIECE, :] - mx[:, lanes])
                p_ref[blk, r:r + PIECE, :] = e.astype(jnp.bfloat16)
                part = jnp.sum(e.reshape(PIECE // SUBLANES, SUBLANES, HALF_W), axis=0)
                key = (lanes.start, half)
                if key in running:
                    running[key] = running[key] + part
                elif _first_live(piece, half, lo):
                    running[key] = part
                else:
                    running[key] = sum_ref[:, lanes] + part
        for (start, _), part in running.items():
            sum_ref[:, start:start + HALF_W] = part

    starts = range(lo, KW, SOFTMAX_ROWS)
    return [lambda r0=r0: step(r0, min(r0 + SOFTMAX_ROWS, KW)) for r0 in starts]


def _zero_dead_blocks(p_all):
    p_all[...] = jnp.zeros(p_all.shape, jnp.bfloat16)


def _weighted_values(pair, quad, lo, vtbuf, p_all, sum_all, out_t):
    row0 = pl.multiple_of(pair * PAIR_W, PAIR_W)
    k0 = quad * QW
    denom = jnp.sum(sum_all[quad], axis=0, keepdims=True)
    p = jnp.concatenate([p_all[quad, blk, lo:KW, :] for blk in range(N_LANE_BLOCKS)], axis=1)
    vt = jnp.concatenate([vtbuf[blk, pl.ds(row0, PAIR_W), :]
                          for blk in range((k0 + lo) // HALF_W, (k0 + KW) // HALF_W)], axis=1)
    o_t = _dot(vt, p)
    inv = 1.0 / denom
    for head in range(2):
        rows = slice(head * HEAD_DIM, (head + 1) * HEAD_DIM)
        lanes = slice(head * QW, (head + 1) * QW)
        out_t[quad, pl.ds(row0 + head * HEAD_DIM, HEAD_DIM), :] = o_t[rows, lanes] * inv[:, lanes]


def _interleave(a_steps, b_steps):
    done = 0
    for t, a_step in enumerate(a_steps):
        a_step()
        upto = (t + 1) * len(b_steps) // len(a_steps)
        for b_step in b_steps[done:upto]:
            b_step()
        done = upto


def _load_weight_bf16(w_hbm, dst, stage, sem, after_chunk=None):
    chunk_rows = stage.shape[1]
    n_chunks = w_hbm.shape[0] // chunk_rows

    def copy(c):
        rows = slice(c * chunk_rows, (c + 1) * chunk_rows)
        return pltpu.make_async_copy(w_hbm.at[rows, :], stage.at[c % 2], sem.at[c % 2])

    copy(0).start()
    for c in range(n_chunks):
        if c + 1 < n_chunks:
            copy(c + 1).start()
        copy(c).wait()
        rows = slice(c * chunk_rows, (c + 1) * chunk_rows)
        dst[rows, :] = stage[c % 2].astype(jnp.bfloat16)
        if after_chunk is not None:
            after_chunk(c, rows, stage.at[c % 2])


def _alternate(a_steps, b_steps):
    for t in range(max(len(a_steps), len(b_steps))):
        if t < len(a_steps):
            a_steps[t]()
        if t < len(b_steps):
            b_steps[t]()


def _mixer_kernel(x_ref, g_pre_ref, w_in_hbm, dw_w_ref, dw_b_ref,
                  ln_g_ref, ln_b_ref, rel_ref, w_out_hbm, g_post_ref, o_ref,
                  w_in_ref, w_vt_ref, w_out_ref, stage_in, stage_out, w_sem,
                  hbuf, hshift, cbuf, qbuf, kbuf, vtbuf, bias_t, s_ref, mx_ref, p_ref, sum_ref,
                  out_t, mixbuf):
    b = pl.program_id(0)
    i = pl.program_id(1)
    c_v = 2 * CONV_WIDTH + 2 * ATTN_WIDTH

    @pl.when((b == 0) & (i == 0))
    def _():
        def value_weights_transposed(c, rows, staged):
            w_vt_ref[:, rows] = staged[:, c_v:c_v + ATTN_WIDTH].T.astype(jnp.bfloat16)

        _load_weight_bf16(w_in_hbm, w_in_ref, stage_in, w_sem, value_weights_transposed)
        _load_weight_bf16(w_out_hbm, w_out_ref, stage_out, w_sem)
        _build_bias(rel_ref, bias_t)
        _zero_dead_blocks(p_ref)

    @pl.when(i == 0)
    def _():
        hbuf[0:HALO_H, :] = jnp.zeros((HALO_H, CONV_WIDTH), jnp.float32)
        kbuf[:, 0:HALO_KV, :] = jnp.zeros((N_PAIRS, HALO_KV, PAIR_W), jnp.bfloat16)
        vtbuf[0:HALO_KV // HALF_W] = jnp.zeros((HALO_KV // HALF_W, ATTN_WIDTH, HALF_W), jnp.bfloat16)

    c_q = 2 * CONV_WIDTH
    c_k = c_q + ATTN_WIDTH

    def conv_block(base):
        acc = jnp.broadcast_to(dw_b_ref[...][None], (CONV_RB // SUBLANES, SUBLANES, CONV_WIDTH))
        for j in range(CONV_KERNEL):
            off = HALO_H - (CONV_KERNEL - 1) + j
            r = off % SUBLANES
            rows = slice(base + off - r, base + off - r + CONV_RB)
            tap = hbuf[rows, :] if r == 0 else hshift[r - 1, rows, :]
            acc = acc + dw_w_ref[j][None] * tap.reshape(acc.shape)
        cbuf[base:base + CONV_RB, :] = acc.reshape(CONV_RB, CONV_WIDTH)

    pending = []

    def project_steps(blk):
        lo_row, hi_row = blk * PROJ_RB, (blk + 1) * PROJ_RB
        rows = slice(lo_row, hi_row)
        state = {}

        def norm_and_value():
            state["u"] = _rms(x_ref[rows, :], g_pre_ref[...]).astype(jnp.bfloat16)
            state["a_val"] = _dot(state["u"], w_in_ref[:, 0:CONV_WIDTH])

        def gate():
            a_gate = _dot(state["u"], w_in_ref[:, CONV_WIDTH:c_q])
            hbuf[HALO_H + lo_row:HALO_H + hi_row, :] = state["a_val"] * jax.nn.sigmoid(a_gate)

        def queries():
            q = _dot(state["u"], w_in_ref[:, c_q:c_k]) * (HEAD_DIM ** -0.5 * LOG2E)
            q = q.astype(jnp.bfloat16)
            for p in range(N_PAIRS):
                qbuf[p, rows, :] = q[:, p * PAIR_W:(p + 1) * PAIR_W]

        def keys_values():
            k = _dot(state["u"], w_in_ref[:, c_k:c_k + ATTN_WIDTH]).astype(jnp.bfloat16)
            for p in range(N_PAIRS):
                kbuf[p, HALO_KV + lo_row:HALO_KV + hi_row, :] = k[:, p * PAIR_W:(p + 1) * PAIR_W]
            pending.append(state["u"])
            if hi_row % VT_COLS == 0:
                u_wide = pending[0] if len(pending) == 1 else jnp.concatenate(pending, axis=0)
                v_t = _dot_nt(w_vt_ref[...], u_wide).astype(jnp.bfloat16)
                first = (HALO_KV + hi_row - VT_COLS) // HALF_W
                for j in range(VT_COLS // HALF_W):
                    vtbuf[first + j] = v_t[:, j * HALF_W:(j + 1) * HALF_W]
                pending.clear()

        return [norm_and_value, gate, queries, keys_values]

    def mix_steps(blk):
        lo_row, hi_row = blk * PROJ_RB, (blk + 1) * PROJ_RB
        rows = slice(lo_row, hi_row)

        def shifted_copies():
            sh_lo = 0 if blk == 0 else lo_row + HSHIFT_LEAD
            sh_hi = hi_row + HSHIFT_LEAD
            for r in range(1, SUBLANES):
                hshift[r - 1, sh_lo:sh_hi, :] = hbuf[sh_lo + r:sh_hi + r, :]

        def norm_swish():
            c = cbuf[rows, :]
            mu = jnp.mean(c, axis=-1, keepdims=True)
            xc = c - mu
            var = jnp.mean(xc * xc, axis=-1, keepdims=True)
            y = xc * lax.rsqrt(var + EPS) * ln_g_ref[...] + ln_b_ref[...]
            y = y * jax.nn.sigmoid(y)
            mixbuf[rows, 0:CONV_WIDTH] = y.astype(jnp.bfloat16)

        convs = [lambda base=lo_row + rb * CONV_RB: conv_block(base)
                 for rb in range(PROJ_RB // CONV_RB)]
        return [shifted_copies] + convs + [norm_swish]

    n_blocks = TM_MIX // PROJ_RB
    for step in project_steps(0):
        step()
    for blk in range(n_blocks):
        nxt = project_steps(blk + 1) if blk + 1 < n_blocks else []
        _alternate(nxt, mix_steps(blk))

    def attn_loop(first_tile):
        lo = [max(0, HALO_KV - quad * QW) if first_tile else 0 for quad in range(N_QUADS)]

        def scores(pair, quad):
            return _score_steps(pair, quad, lo[quad], qbuf, kbuf, bias_t, s_ref, mx_ref)

        for step in scores(0, 0):
            step()

        def body(pair, carry):
            for quad in range(N_QUADS):
                if quad + 1 < N_QUADS:
                    nxt = scores(pair, quad + 1)
                else:
                    nxt = scores(jnp.minimum(pair + 1, N_PAIRS - 1), 0)
                _interleave(nxt, _softmax_steps(quad, lo[quad], s_ref, mx_ref, p_ref, sum_ref))
                _weighted_values(pair, quad, lo[quad], vtbuf, p_ref, sum_ref, out_t)
            return carry
        lax.fori_loop(0, N_PAIRS, body, 0)

    @pl.when(i == 0)
    def _():
        attn_loop(True)

    @pl.when(i > 0)
    def _():
        attn_loop(False)

    for blk in range(TM_MIX // OUT_RB):
        rows = slice(blk * OUT_RB, (blk + 1) * OUT_RB)
        attn = out_t[blk].T.astype(jnp.bfloat16)
        mixed = _dot(jnp.concatenate([mixbuf[rows, 0:CONV_WIDTH], attn], axis=1), w_out_ref[...])
        o_ref[rows, :] = x_ref[rows, :] + _rms(mixed, g_post_ref[...])

    hbuf[0:HALO_H, :] = hbuf[TM_MIX:TM_MIX + HALO_H, :]
    kbuf[:, 0:HALO_KV, :] = kbuf[:, TM_MIX:TM_MIX + HALO_KV, :]
    vtbuf[0:HALO_KV // HALF_W] = vtbuf[TM_MIX // HALF_W:(TM_MIX + HALO_KV) // HALF_W]


def _ffn_kernel(h_hbm, g_pre_ref, w_up_hbm, dw_w_ref, dw_b_ref, w_down_hbm, g_post_ref, o_hbm,
                w_up_ref, w_down_ref, stage_up, stage_down, w_sem,
                xin, xout, sem_in, sem_out, carry, gbuf, vbuf, actbuf):
    i = pl.program_id(1)
    n = pl.program_id(0) * pl.num_programs(1) + i
    n_tiles = pl.num_programs(0) * pl.num_programs(1)
    slot = n % 2

    def in_copy(tile, sl, s):
        return pltpu.make_async_copy(h_hbm.at[tile, s], xin.at[sl, :, s, :], sem_in.at[sl])

    def out_copy(tile, sl, s):
        return pltpu.make_async_copy(xout.at[sl, :, s, :], o_hbm.at[tile, s], sem_out.at[sl])

    @pl.when(n == 0)
    def _():
        for s in range(SUBLANES):
            in_copy(0, 0, s).start()
        _load_weight_bf16(w_up_hbm, w_up_ref, stage_up, w_sem)
        _load_weight_bf16(w_down_hbm, w_down_ref, stage_down, w_sem)

    @pl.when(n + 1 < n_tiles)
    def _():
        for s in range(SUBLANES):
            in_copy(n + 1, 1 - slot, s).start()

    @pl.when(i == 0)
    def _():
        carry[...] = jnp.zeros(carry.shape, jnp.float32)

    @pl.when(n >= 2)
    def _():
        for s in range(SUBLANES):
            out_copy(n - 2, slot, s).wait()

    for s in range(SUBLANES):
        in_copy(n, slot, s).wait()

    xt = xin[slot].reshape(TM_FFN, D_MODEL)
    u = _rms(xt, g_pre_ref[...]).astype(jnp.bfloat16)
    first_sublane = lax.broadcasted_iota(jnp.int32, (SUBLANES, FFN_CW), 0) == 0

    def up_conv(buf, cols):
        hc = _dot(u, w_up_ref[:, cols])
        for k in (1, 2):
            last = hc[TM_FFN - k * SUBLANES:TM_FFN - (k - 1) * SUBLANES, :]
            prev = carry[(2 - k) * SUBLANES:(3 - k) * SUBLANES, cols]
            buf[(2 - k) * SUBLANES:(3 - k) * SUBLANES, :] = jnp.where(
                first_sublane, pltpu.roll(prev, 1, 0), pltpu.roll(last, 1, 0))
            carry[(2 - k) * SUBLANES:(3 - k) * SUBLANES, cols] = last
        buf[FFN_HALO:FFN_HALO + TM_FFN, :] = hc
        y = dw_b_ref[:, cols] + dw_w_ref[2:3, cols] * hc
        y = y + dw_w_ref[1:2, cols] * buf[SUBLANES:SUBLANES + TM_FFN, :]
        y = y + dw_w_ref[0:1, cols] * buf[0:TM_FFN, :]
        return y

    for c in range(D_FF // FFN_CW):
        gate = up_conv(gbuf.at[c % 2], slice(c * FFN_CW, (c + 1) * FFN_CW))
        val = up_conv(vbuf.at[c % 2], slice(D_FF + c * FFN_CW, D_FF + (c + 1) * FFN_CW))
        actbuf[:, c * FFN_CW:(c + 1) * FFN_CW] = (jax.nn.gelu(gate) * val).astype(jnp.bfloat16)
    for blk in range(TM_FFN // FFN_DOWN_RB):
        rows = slice(blk * FFN_DOWN_RB, (blk + 1) * FFN_DOWN_RB)
        slabs = slice(blk * FFN_DOWN_RB // SUBLANES, (blk + 1) * FFN_DOWN_RB // SUBLANES)
        f = _dot(actbuf[rows, :], w_down_ref[...])
        y = xt[rows, :] + _rms(f, g_post_ref[...])
        xout[slot, slabs] = y.reshape(FFN_DOWN_RB // SUBLANES, SUBLANES, D_MODEL)
    for s in range(SUBLANES):
        out_copy(n, slot, s).start()

    @pl.when(n == n_tiles - 1)
    def _():
        @pl.when(n >= 1)
        def _():
            for s in range(SUBLANES):
                out_copy(n - 1, 1 - slot, s).wait()
        for s in range(SUBLANES):
            out_copy(n, slot, s).wait()


def _rel_distance_row(rel_table):
    h = rel_table.shape[0]
    far = rel_table[:, 2 * MAX_REL:2 * MAX_REL + 1]
    n_far = HALO_KV - MAX_REL + 1
    near = jnp.flip(rel_table, axis=1)[:, 1:]
    n_tail = ROLL_W - n_far - near.shape[1]
    return jnp.concatenate([jnp.broadcast_to(far, (h, n_far)), near,
                            jnp.broadcast_to(far, (h, n_tail))], axis=1)


def _const_spec(shape):
    return pl.BlockSpec(shape, lambda b, i: (0,) * len(shape), pipeline_mode=pl.Buffered(1))


def _mixer(x, g_pre, w_in, dw_w, dw_b, ln_g, ln_b, rel_row, w_out, g_post):
    B, T, D = x.shape
    row_spec = pl.BlockSpec((None, TM_MIX, D), lambda b, i: (b, i, 0))
    hbm_spec = pl.BlockSpec(memory_space=pl.ANY)
    consts = (g_pre, w_in, dw_w, dw_b, ln_g, ln_b, rel_row, w_out, g_post)
    const_specs = [hbm_spec if c is w_in or c is w_out else _const_spec(c.shape) for c in consts]
    return pl.pallas_call(
        _mixer_kernel,
        grid=(B, T // TM_MIX),
        in_specs=[row_spec] + const_specs,
        out_specs=row_spec,
        out_shape=jax.ShapeDtypeStruct(x.shape, x.dtype),
        scratch_shapes=[
            pltpu.VMEM(w_in.shape, jnp.bfloat16),
            pltpu.VMEM((ATTN_WIDTH, D), jnp.bfloat16),
            pltpu.VMEM(w_out.shape, jnp.bfloat16),
            pltpu.VMEM((2, W_IN_CHUNK, w_in.shape[1]), jnp.float32),
            pltpu.VMEM((2, W_OUT_CHUNK, w_out.shape[1]), jnp.float32),
            pltpu.SemaphoreType.DMA((2,)),
            pltpu.VMEM((HALO_H + TM_MIX, CONV_WIDTH), jnp.float32),
            pltpu.VMEM((SUBLANES - 1, HSHIFT_ROWS, CONV_WIDTH), jnp.float32),
            pltpu.VMEM((TM_MIX, CONV_WIDTH), jnp.float32),
            pltpu.VMEM((N_PAIRS, TM_MIX, PAIR_W), jnp.bfloat16),
            pltpu.VMEM((N_PAIRS, HALO_KV + TM_MIX, PAIR_W), jnp.bfloat16),
            pltpu.VMEM(((HALO_KV + TM_MIX) // HALF_W, ATTN_WIDTH, HALF_W), jnp.bfloat16),
            pltpu.VMEM((N_HEADS, QW // HALF_W, KW, HALF_W), jnp.float32),
            pltpu.VMEM((N_QUADS, N_LANE_BLOCKS, KW, HALF_W), jnp.float32),
            pltpu.VMEM((N_QUADS, SUBLANES, 2 * QW), jnp.float32),
            pltpu.VMEM((N_QUADS, N_LANE_BLOCKS, KW, HALF_W), jnp.bfloat16),
            pltpu.VMEM((N_QUADS, SUBLANES, 2 * QW), jnp.float32),
            pltpu.VMEM((N_QUADS, ATTN_WIDTH, QW), jnp.float32),
            pltpu.VMEM((TM_MIX, CONV_WIDTH + ATTN_WIDTH), jnp.bfloat16),
        ],
        compiler_params=pltpu.CompilerParams(
            dimension_semantics=("arbitrary", "arbitrary"), vmem_limit_bytes=VMEM_LIMIT),
        name="mixer",
    )(x, *consts)


def _ffn(h, g_pre, w_up, dw_w, dw_b, w_down, g_post):
    B, T, D = h.shape
    n_t = T // TM_FFN
    tiles = (B * n_t, SUBLANES, FFN_SEG, D)
    hbm_spec = pl.BlockSpec(memory_space=pl.ANY)
    out = pl.pallas_call(
        _ffn_kernel,
        grid=(B, n_t),
        in_specs=[hbm_spec,
                  _const_spec((1, D)), hbm_spec, _const_spec(dw_w.shape),
                  _const_spec((1, 2 * D_FF)), hbm_spec, _const_spec((1, D))],
        out_specs=hbm_spec,
        out_shape=jax.ShapeDtypeStruct(tiles, h.dtype),
        scratch_shapes=[
            pltpu.VMEM(w_up.shape, jnp.bfloat16),
            pltpu.VMEM(w_down.shape, jnp.bfloat16),
            pltpu.VMEM((2, W_UP_CHUNK, w_up.shape[1]), jnp.float32),
            pltpu.VMEM((2, W_DOWN_CHUNK, w_down.shape[1]), jnp.float32),
            pltpu.SemaphoreType.DMA((2,)),
            pltpu.VMEM((2, FFN_SEG, SUBLANES, D), jnp.float32),
            pltpu.VMEM((2, FFN_SEG, SUBLANES, D), jnp.float32),
            pltpu.SemaphoreType.DMA((2,)),
            pltpu.SemaphoreType.DMA((2,)),
            pltpu.VMEM((FFN_HALO, 2 * D_FF), jnp.float32),
            pltpu.VMEM((2, FFN_HALO + TM_FFN, FFN_CW), jnp.float32),
            pltpu.VMEM((2, FFN_HALO + TM_FFN, FFN_CW), jnp.float32),
            pltpu.VMEM((TM_FFN, D_FF), jnp.bfloat16),
        ],
        compiler_params=pltpu.CompilerParams(
            dimension_semantics=("arbitrary", "arbitrary"), vmem_limit_bytes=VMEM_LIMIT),
        name="ffn",
    )(h.reshape(tiles), g_pre, w_up, dw_w, dw_b, w_down, g_post)
    return out.reshape(B, T, D)


def kernel(x, norm_mix_pre, w_in, conv_dw_w, conv_dw_b, conv_ln_g, conv_ln_b, rel_bias, w_out,
           norm_mix_post, norm_ffn_pre, w_up, ffn_dw_w, ffn_dw_b, w_down, norm_ffn_post):
    h = x
    for l in range(norm_mix_pre.shape[0]):
        h = _mixer(h, norm_mix_pre[l][None], w_in[l],
                   jnp.broadcast_to(conv_dw_w[l][:, None, :], (CONV_KERNEL, SUBLANES, CONV_WIDTH)),
                   conv_dw_b[l][None], conv_ln_g[l][None], conv_ln_b[l][None],
                   _rel_distance_row(rel_bias[l]), w_out[l], norm_mix_post[l][None])
        h = _ffn(h, norm_ffn_pre[l][None], w_up[l], ffn_dw_w[l],
                 ffn_dw_b[l][None], w_down[l], norm_ffn_post[l][None])
    return h
```

```python
import jax
import jax.numpy as jnp
from jax import lax
from jax.experimental import pallas as pl
from jax.experimental.pallas import tpu as pltpu

D_MODEL = 1024
CHUNK = 64
N_LEFT_CHUNKS = 8
CONV_WIDTH = 512
ATTN_WIDTH = 512
HEAD_DIM = 64
N_HEADS = ATTN_WIDTH // HEAD_DIM
N_PAIRS = N_HEADS // 2
PAIR_W = 2 * HEAD_DIM
CONV_KERNEL = 31
MAX_REL = 128
D_FF = 2816
FFN_CONV_KERNEL = 3
EPS = 1e-6
NEG_INF = -1e30

TM_MIX = 512
HALO_H = 32
HALO_KV = N_LEFT_CHUNKS * CHUNK
CONV_RB = 32
OUT_RB = 256
PROJ_RB = 256
VT_COLS = 256
SUBLANES = 8
HSHIFT_LEAD = HALO_H - SUBLANES
HSHIFT_ROWS = HALO_H + TM_MIX - SUBLANES
QUAD = 4
QW = QUAD * CHUNK
KW = HALO_KV + QW
N_QUADS = TM_MIX // QW
PIECE = 128
SCORE_ROWS = 384
SOFTMAX_ROWS = 256
HALF_W = 128
FAR_LAG = (HALO_KV - MAX_REL) // CHUNK - 1
LOG2E = 1.4426950408889634
ROLL_W = 1024
TM_FFN = 512
FFN_CW = 256
FFN_DOWN_RB = 256
FFN_SEG = TM_FFN // SUBLANES
FFN_HALO = (FFN_CONV_KERNEL - 1) * SUBLANES
W_IN_CHUNK = 128
W_OUT_CHUNK = 256
W_UP_CHUNK = 128
W_DOWN_CHUNK = 352
VMEM_LIMIT = 56 * 1024 * 1024


def _rms(xf, g):
    return xf * lax.rsqrt(jnp.mean(xf * xf, axis=-1, keepdims=True) + EPS) * g


def _dot(a, b):
    return jnp.dot(a, b, preferred_element_type=jnp.float32)


def _dot_nt(a, b):
    return lax.dot_general(a, b, (((1,), (1,)), ((), ())), preferred_element_type=jnp.float32)


def _build_bias(rel_ref, bias_t):
    r_idx = lax.broadcasted_iota(jnp.int32, (KW, QW), 0) // CHUNK
    c_idx = lax.broadcasted_iota(jnp.int32, (KW, QW), 1) // CHUNK
    visible = (r_idx >= c_idx) & (r_idx <= c_idx + N_LEFT_CHUNKS)
    for h in range(N_HEADS):
        row = jnp.broadcast_to(rel_ref[h:h + 1, :], (QW, ROLL_W))
        nat = pltpu.roll(row, 0, 1, stride=1, stride_axis=0)[:, 0:KW]
        rel_to_far = (nat.T - rel_ref[h:h + 1, 0:1]) * LOG2E
        masked = jnp.where(visible, rel_to_far, NEG_INF)
        for half in range(QW // HALF_W):
            bias_t[h, half] = masked[:, half * HALF_W:(half + 1) * HALF_W]


def _block_kind(piece, half):
    lags = [kc - qc
            for kc in range(piece * PIECE // CHUNK, (piece + 1) * PIECE // CHUNK)
            for qc in range(half * HALF_W // CHUNK, (half + 1) * HALF_W // CHUNK)]
    live = any(0 <= lag <= N_LEFT_CHUNKS for lag in lags)
    plain = all(0 <= lag <= FAR_LAG for lag in lags)
    return live, plain


def _first_live(piece, half, lo):
    return not any(_block_kind(t, half)[0] for t in range(lo // PIECE, piece))


def _unit_blocks(piece):
    for head in range(2):
        for half in range(QW // HALF_W):
            live, plain = _block_kind(piece, half)
            if live:
                start = head * QW + half * HALF_W
                yield head, half, plain, slice(start, start + HALF_W)


N_LANE_BLOCKS = 2 * QW // HALF_W


def _score_steps(pair, quad, lo, qbuf, kbuf, bias_t, s_all, mx_all):
    s_ref, mx_ref = s_all.at[quad], mx_all.at[quad]
    k0 = quad * QW
    qt = qbuf[pl.ds(pl.multiple_of(pair * PAIR_W, PAIR_W), PAIR_W), k0:k0 + QW]
    zero = jnp.zeros((HEAD_DIM, QW), qt.dtype)
    qm = jnp.concatenate([jnp.concatenate([qt[0:HEAD_DIM], zero], axis=0),
                          jnp.concatenate([zero, qt[HEAD_DIM:PAIR_W]], axis=0)], axis=1)

    def step(r0, r1):
        s = _dot(kbuf[pair, k0 + r0:k0 + r1, :], qm)
        running = {}
        for piece in range(r0 // PIECE, r1 // PIECE):
            r = piece * PIECE
            for head, half, plain, lanes in _unit_blocks(piece):
                sh = s[r - r0:r - r0 + PIECE, lanes]
                if not plain:
                    sh = sh + bias_t[2 * pair + head, half, r:r + PIECE, :]
                s_ref[lanes.start // HALF_W, r:r + PIECE, :] = sh
                m = jnp.max(sh.reshape(PIECE // SUBLANES, SUBLANES, HALF_W), axis=0)
                key = (lanes.start, half)
                if key in running:
                    running[key] = jnp.maximum(running[key], m)
                elif _first_live(piece, half, lo):
                    running[key] = m
                else:
                    running[key] = jnp.maximum(mx_ref[:, lanes], m)
        for (start, _), m in running.items():
            mx_ref[:, start:start + HALF_W] = m

    starts = range(lo, KW, SCORE_ROWS)
    return [lambda r0=r0: step(r0, min(r0 + SCORE_ROWS, KW)) for r0 in starts]


def _softmax_steps(quad, lo, s_all, mx_all, p_all, sum_all):
    s_ref, mx_ref, p_ref, sum_ref = s_all.at[quad], mx_all.at[quad], p_all.at[quad], sum_all.at[quad]
    mx = jnp.max(mx_ref[...], axis=0, keepdims=True)

    def step(r0, r1):
        running = {}
        for piece in range(r0 // PIECE, r1 // PIECE):
            r = piece * PIECE
            for _, half, _, lanes in _unit_blocks(piece):
                blk = lanes.start // HALF_W
                e = jnp.exp2(s_ref[blk, r:r + PIECE, :] - mx[:, lanes])
                p_ref[blk, r:r + PIECE, :] = e.astype(jnp.bfloat16)
                part = jnp.sum(e.reshape(PIECE // SUBLANES, SUBLANES, HALF_W), axis=0)
                key = (lanes.start, half)
                if key in running:
                    running[key] = running[key] + part
                elif _first_live(piece, half, lo):
                    running[key] = part
                else:
                    running[key] = sum_ref[:, lanes] + part
        for (start, _), part in running.items():
            sum_ref[:, start:start + HALF_W] = part

    starts = range(lo, KW, SOFTMAX_ROWS)
    return [lambda r0=r0: step(r0, min(r0 + SOFTMAX_ROWS, KW)) for r0 in starts]


def _zero_dead_blocks(p_all):
    p_all[...] = jnp.zeros(p_all.shape, jnp.bfloat16)


def _weighted_values(pair, quad, lo, vtbuf, p_all, sum_all, out_t):
    row0 = pl.multiple_of(pair * PAIR_W, PAIR_W)
    k0 = quad * QW
    denom = jnp.sum(sum_all[quad], axis=0, keepdims=True)
    p = jnp.concatenate([p_all[quad, blk, lo:KW, :] for blk in range(N_LANE_BLOCKS)], axis=1)
    o_t = _dot(vtbuf[pl.ds(row0, PAIR_W), k0 + lo:k0 + KW], p)
    inv = 1.0 / denom
    for head in range(2):
        rows = slice(head * HEAD_DIM, (head + 1) * HEAD_DIM)
        lanes = slice(head * QW, (head + 1) * QW)
        out_t[pl.ds(row0 + head * HEAD_DIM, HEAD_DIM), k0:k0 + QW] = o_t[rows, lanes] * inv[:, lanes]


def _interleave(a_steps, b_steps):
    done = 0
    for t, a_step in enumerate(a_steps):
        a_step()
        upto = (t + 1) * len(b_steps) // len(a_steps)
        for b_step in b_steps[done:upto]:
            b_step()
        done = upto


def _load_weight_bf16(w_hbm, dst, stage, sem, after_chunk=None):
    chunk_rows = stage.shape[1]
    n_chunks = w_hbm.shape[0] // chunk_rows

    def copy(c):
        rows = slice(c * chunk_rows, (c + 1) * chunk_rows)
        return pltpu.make_async_copy(w_hbm.at[rows, :], stage.at[c % 2], sem.at[c % 2])

    copy(0).start()
    for c in range(n_chunks):
        if c + 1 < n_chunks:
            copy(c + 1).start()
        copy(c).wait()
        rows = slice(c * chunk_rows, (c + 1) * chunk_rows)
        dst[rows, :] = stage[c % 2].astype(jnp.bfloat16)
        if after_chunk is not None:
            after_chunk(c, rows, stage.at[c % 2])


def _alternate(a_steps, b_steps):
    for t in range(max(len(a_steps), len(b_steps))):
        if t < len(a_steps):
            a_steps[t]()
        if t < len(b_steps):
            b_steps[t]()


def _mixer_kernel(x_ref, g_pre_ref, w_in_hbm, dw_w_ref, dw_b_ref,
                  ln_g_ref, ln_b_ref, rel_ref, w_out_hbm, g_post_ref, o_ref,
                  w_in_ref, w_vt_ref, w_qt_ref, w_out_ref, stage_in, stage_out, w_sem,
                  hbuf, hshift, cbuf, qbuf, kbuf, vtbuf, bias_t, s_ref, mx_ref, p_ref, sum_ref,
                  out_t, mixbuf):
    b = pl.program_id(0)
    i = pl.program_id(1)
    c_q = 2 * CONV_WIDTH
    c_k = c_q + ATTN_WIDTH
    c_v = c_k + ATTN_WIDTH

    @pl.when((b == 0) & (i == 0))
    def _():
        def value_weights_transposed(c, rows, staged):
            w_vt_ref[:, rows] = staged[:, c_v:c_v + ATTN_WIDTH].T.astype(jnp.bfloat16)
            w_qt_ref[:, rows] = staged[:, c_q:c_q + ATTN_WIDTH].T.astype(jnp.bfloat16)

        _load_weight_bf16(w_in_hbm, w_in_ref, stage_in, w_sem, value_weights_transposed)
        _load_weight_bf16(w_out_hbm, w_out_ref, stage_out, w_sem)
        _build_bias(rel_ref, bias_t)
        _zero_dead_blocks(p_ref)

    @pl.when(i == 0)
    def _():
        hbuf[0:HALO_H, :] = jnp.zeros((HALO_H, CONV_WIDTH), jnp.float32)
        kbuf[:, 0:HALO_KV, :] = jnp.zeros((N_PAIRS, HALO_KV, PAIR_W), jnp.bfloat16)
        vtbuf[:, 0:HALO_KV] = jnp.zeros((ATTN_WIDTH, HALO_KV), jnp.bfloat16)

    def conv_block(base):
        acc = jnp.broadcast_to(dw_b_ref[...][None], (CONV_RB // SUBLANES, SUBLANES, CONV_WIDTH))
        for j in range(CONV_KERNEL):
            off = HALO_H - (CONV_KERNEL - 1) + j
            r = off % SUBLANES
            rows = slice(base + off - r, base + off - r + CONV_RB)
            tap = hbuf[rows, :] if r == 0 else hshift[r - 1, rows, :]
            acc = acc + dw_w_ref[j][None] * tap.reshape(acc.shape)
        cbuf[base:base + CONV_RB, :] = acc.reshape(CONV_RB, CONV_WIDTH)

    pending = []

    def project_steps(blk):
        lo_row, hi_row = blk * PROJ_RB, (blk + 1) * PROJ_RB
        rows = slice(lo_row, hi_row)
        state = {}

        def norm_and_value():
            state["u"] = _rms(x_ref[rows, :], g_pre_ref[...]).astype(jnp.bfloat16)
            state["a_val"] = _dot(state["u"], w_in_ref[:, 0:CONV_WIDTH])

        def gate():
            a_gate = _dot(state["u"], w_in_ref[:, CONV_WIDTH:c_q])
            hbuf[HALO_H + lo_row:HALO_H + hi_row, :] = state["a_val"] * jax.nn.sigmoid(a_gate)

        def queries():
            q_t = _dot_nt(w_qt_ref[...], state["u"]) * (HEAD_DIM ** -0.5 * LOG2E)
            qbuf[:, rows] = q_t.astype(jnp.bfloat16)

        def keys_values():
            k = _dot(state["u"], w_in_ref[:, c_k:c_k + ATTN_WIDTH]).astype(jnp.bfloat16)
            for p in range(N_PAIRS):
                kbuf[p, HALO_KV + lo_row:HALO_KV + hi_row, :] = k[:, p * PAIR_W:(p + 1) * PAIR_W]
            pending.append(state["u"])
            if hi_row % VT_COLS == 0:
                u_wide = pending[0] if len(pending) == 1 else jnp.concatenate(pending, axis=0)
                vtbuf[:, HALO_KV + hi_row - VT_COLS:HALO_KV + hi_row] = (
                    _dot_nt(w_vt_ref[...], u_wide).astype(jnp.bfloat16))
                pending.clear()

        return [norm_and_value, gate, queries, keys_values]

    def mix_steps(blk):
        lo_row, hi_row = blk * PROJ_RB, (blk + 1) * PROJ_RB
        rows = slice(lo_row, hi_row)

        def shifted_copies():
            sh_lo = 0 if blk == 0 else lo_row + HSHIFT_LEAD
            sh_hi = hi_row + HSHIFT_LEAD
            for r in range(1, SUBLANES):
                hshift[r - 1, sh_lo:sh_hi, :] = hbuf[sh_lo + r:sh_hi + r, :]

        def norm_swish():
            c = cbuf[rows, :]
            mu = jnp.mean(c, axis=-1, keepdims=True)
            xc = c - mu
            var = jnp.mean(xc * xc, axis=-1, keepdims=True)
            y = xc * lax.rsqrt(var + EPS) * ln_g_ref[...] + ln_b_ref[...]
            y = y * jax.nn.sigmoid(y)
            mixbuf[rows, 0:CONV_WIDTH] = y.astype(jnp.bfloat16)

        convs = [lambda base=lo_row + rb * CONV_RB: conv_block(base)
                 for rb in range(PROJ_RB // CONV_RB)]
        return [shifted_copies] + convs + [norm_swish]

    n_blocks = TM_MIX // PROJ_RB
    for step in project_steps(0):
        step()
    for blk in range(n_blocks):
        nxt = project_steps(blk + 1) if blk + 1 < n_blocks else []
        _alternate(nxt, mix_steps(blk))

    def attn_loop(first_tile):
        lo = [max(0, HALO_KV - quad * QW) if first_tile else 0 for quad in range(N_QUADS)]

        def scores(pair, quad):
            return _score_steps(pair, quad, lo[quad], qbuf, kbuf, bias_t, s_ref, mx_ref)

        for step in scores(0, 0):
            step()

        def body(pair, carry):
            for quad in range(N_QUADS):
                if quad + 1 < N_QUADS:
                    nxt = scores(pair, quad + 1)
                else:
                    nxt = scores(jnp.minimum(pair + 1, N_PAIRS - 1), 0)
                _interleave(nxt, _softmax_steps(quad, lo[quad], s_ref, mx_ref, p_ref, sum_ref))
                _weighted_values(pair, quad, lo[quad], vtbuf, p_ref, sum_ref, out_t)
            return carry
        lax.fori_loop(0, N_PAIRS, body, 0)

    @pl.when(i == 0)
    def _():
        attn_loop(True)

    @pl.when(i > 0)
    def _():
        attn_loop(False)

    for blk in range(TM_MIX // OUT_RB):
        rows = slice(blk * OUT_RB, (blk + 1) * OUT_RB)
        attn = out_t[:, rows].T.astype(jnp.bfloat16)
        mixed = _dot(jnp.concatenate([mixbuf[rows, 0:CONV_WIDTH], attn], axis=1), w_out_ref[...])
        o_ref[rows, :] = x_ref[rows, :] + _rms(mixed, g_post_ref[...])

    hbuf[0:HALO_H, :] = hbuf[TM_MIX:TM_MIX + HALO_H, :]
    kbuf[:, 0:HALO_KV, :] = kbuf[:, TM_MIX:TM_MIX + HALO_KV, :]
    vtbuf[:, 0:HALO_KV] = vtbuf[:, TM_MIX:TM_MIX + HALO_KV]


def _ffn_kernel(h_hbm, g_pre_ref, w_up_hbm, dw_w_ref, dw_b_ref, w_down_hbm, g_post_ref, o_hbm,
                w_up_ref, w_down_ref, stage_up, stage_down, w_sem,
                xin, xout, sem_in, sem_out, carry, gbuf, vbuf, actbuf):
    i = pl.program_id(1)
    n = pl.program_id(0) * pl.num_programs(1) + i
    n_tiles = pl.num_programs(0) * pl.num_programs(1)
    slot = n % 2

    def in_copy(tile, sl, s):
        return pltpu.make_async_copy(h_hbm.at[tile, s], xin.at[sl, :, s, :], sem_in.at[sl])

    def out_copy(tile, sl, s):
        return pltpu.make_async_copy(xout.at[sl, :, s, :], o_hbm.at[tile, s], sem_out.at[sl])

    @pl.when(n == 0)
    def _():
        for s in range(SUBLANES):
            in_copy(0, 0, s).start()
        _load_weight_bf16(w_up_hbm, w_up_ref, stage_up, w_sem)
        _load_weight_bf16(w_down_hbm, w_down_ref, stage_down, w_sem)

    @pl.when(n + 1 < n_tiles)
    def _():
        for s in range(SUBLANES):
            in_copy(n + 1, 1 - slot, s).start()

    @pl.when(i == 0)
    def _():
        carry[...] = jnp.zeros(carry.shape, jnp.float32)

    @pl.when(n >= 2)
    def _():
        for s in range(SUBLANES):
            out_copy(n - 2, slot, s).wait()

    for s in range(SUBLANES):
        in_copy(n, slot, s).wait()

    xt = xin[slot].reshape(TM_FFN, D_MODEL)
    u = _rms(xt, g_pre_ref[...]).astype(jnp.bfloat16)
    first_sublane = lax.broadcasted_iota(jnp.int32, (SUBLANES, FFN_CW), 0) == 0

    def up_conv(buf, cols):
        hc = _dot(u, w_up_ref[:, cols])
        for k in (1, 2):
            last = hc[TM_FFN - k * SUBLANES:TM_FFN - (k - 1) * SUBLANES, :]
            prev = carry[(2 - k) * SUBLANES:(3 - k) * SUBLANES, cols]
            buf[(2 - k) * SUBLANES:(3 - k) * SUBLANES, :] = jnp.where(
                first_sublane, pltpu.roll(prev, 1, 0), pltpu.roll(last, 1, 0))
            carry[(2 - k) * SUBLANES:(3 - k) * SUBLANES, cols] = last
        buf[FFN_HALO:FFN_HALO + TM_FFN, :] = hc
        y = dw_b_ref[:, cols] + dw_w_ref[2:3, cols] * hc
        y = y + dw_w_ref[1:2, cols] * buf[SUBLANES:SUBLANES + TM_FFN, :]
        y = y + dw_w_ref[0:1, cols] * buf[0:TM_FFN, :]
        return y

    for c in range(D_FF // FFN_CW):
        gate = up_conv(gbuf.at[c % 2], slice(c * FFN_CW, (c + 1) * FFN_CW))
        val = up_conv(vbuf.at[c % 2], slice(D_FF + c * FFN_CW, D_FF + (c + 1) * FFN_CW))
        actbuf[:, c * FFN_CW:(c + 1) * FFN_CW] = (jax.nn.gelu(gate) * val).astype(jnp.bfloat16)
    for blk in range(TM_FFN // FFN_DOWN_RB):
        rows = slice(blk * FFN_DOWN_RB, (blk + 1) * FFN_DOWN_RB)
        slabs = slice(blk * FFN_DOWN_RB // SUBLANES, (blk + 1) * FFN_DOWN_RB // SUBLANES)
        f = _dot(actbuf[rows, :], w_down_ref[...])
        y = xt[rows, :] + _rms(f, g_post_ref[...])
        xout[slot, slabs] = y.reshape(FFN_DOWN_RB // SUBLANES, SUBLANES, D_MODEL)
    for s in range(SUBLANES):
        out_copy(n, slot, s).start()

    @pl.when(n == n_tiles - 1)
    def _():
        @pl.when(n >= 1)
        def _():
            for s in range(SUBLANES):
                out_copy(n - 1, 1 - slot, s).wait()
        for s in range(SUBLANES):
            out_copy(n, slot, s).wait()


def _rel_distance_row(rel_table):
    h = rel_table.shape[0]
    far = rel_table[:, 2 * MAX_REL:2 * MAX_REL + 1]
    n_far = HALO_KV - MAX_REL + 1
    near = jnp.flip(rel_table, axis=1)[:, 1:]
    n_tail = ROLL_W - n_far - near.shape[1]
    return jnp.concatenate([jnp.broadcast_to(far, (h, n_far)), near,
                            jnp.broadcast_to(far, (h, n_tail))], axis=1)


def _const_spec(shape):
    return pl.BlockSpec(shape, lambda b, i: (0,) * len(shape), pipeline_mode=pl.Buffered(1))


def _mixer(x, g_pre, w_in, dw_w, dw_b, ln_g, ln_b, rel_row, w_out, g_post):
    B, T, D = x.shape
    row_spec = pl.BlockSpec((None, TM_MIX, D), lambda b, i: (b, i, 0))
    hbm_spec = pl.BlockSpec(memory_space=pl.ANY)
    consts = (g_pre, w_in, dw_w, dw_b, ln_g, ln_b, rel_row, w_out, g_post)
    const_specs = [hbm_spec if c is w_in or c is w_out else _const_spec(c.shape) for c in consts]
    return pl.pallas_call(
        _mixer_kernel,
        grid=(B, T // TM_MIX),
        in_specs=[row_spec] + const_specs,
        out_specs=row_spec,
        out_shape=jax.ShapeDtypeStruct(x.shape, x.dtype),
        scratch_shapes=[
            pltpu.VMEM(w_in.shape, jnp.bfloat16),
            pltpu.VMEM((ATTN_WIDTH, D), jnp.bfloat16),
            pltpu.VMEM((ATTN_WIDTH, D), jnp.bfloat16),
            pltpu.VMEM(w_out.shape, jnp.bfloat16),
            pltpu.VMEM((2, W_IN_CHUNK, w_in.shape[1]), jnp.float32),
            pltpu.VMEM((2, W_OUT_CHUNK, w_out.shape[1]), jnp.float32),
            pltpu.SemaphoreType.DMA((2,)),
            pltpu.VMEM((HALO_H + TM_MIX, CONV_WIDTH), jnp.float32),
            pltpu.VMEM((SUBLANES - 1, HSHIFT_ROWS, CONV_WIDTH), jnp.float32),
            pltpu.VMEM((TM_MIX, CONV_WIDTH), jnp.float32),
            pltpu.VMEM((ATTN_WIDTH, TM_MIX), jnp.bfloat16),
            pltpu.VMEM((N_PAIRS, HALO_KV + TM_MIX, PAIR_W), jnp.bfloat16),
            pltpu.VMEM((ATTN_WIDTH, HALO_KV + TM_MIX), jnp.bfloat16),
            pltpu.VMEM((N_HEADS, QW // HALF_W, KW, HALF_W), jnp.float32),
            pltpu.VMEM((N_QUADS, N_LANE_BLOCKS, KW, HALF_W), jnp.float32),
            pltpu.VMEM((N_QUADS, SUBLANES, 2 * QW), jnp.float32),
            pltpu.VMEM((N_QUADS, N_LANE_BLOCKS, KW, HALF_W), jnp.bfloat16),
            pltpu.VMEM((N_QUADS, SUBLANES, 2 * QW), jnp.float32),
            pltpu.VMEM((ATTN_WIDTH, TM_MIX), jnp.float32),
            pltpu.VMEM((TM_MIX, CONV_WIDTH + ATTN_WIDTH), jnp.bfloat16),
        ],
        compiler_params=pltpu.CompilerParams(
            dimension_semantics=("arbitrary", "arbitrary"), vmem_limit_bytes=VMEM_LIMIT),
        name="mixer",
    )(x, *consts)


def _ffn(h, g_pre, w_up, dw_w, dw_b, w_down, g_post):
    B, T, D = h.shape
    n_t = T // TM_FFN
    tiles = (B * n_t, SUBLANES, FFN_SEG, D)
    hbm_spec = pl.BlockSpec(memory_space=pl.ANY)
    out = pl.pallas_call(
        _ffn_kernel,
        grid=(B, n_t),
        in_specs=[hbm_spec,
                  _const_spec((1, D)), hbm_spec, _const_spec(dw_w.shape),
                  _const_spec((1, 2 * D_FF)), hbm_spec, _const_spec((1, D))],
        out_specs=hbm_spec,
        out_shape=jax.ShapeDtypeStruct(tiles, h.dtype),
        scratch_shapes=[
            pltpu.VMEM(w_up.shape, jnp.bfloat16),
            pltpu.VMEM(w_down.shape, jnp.bfloat16),
            pltpu.VMEM((2, W_UP_CHUNK, w_up.shape[1]), jnp.float32),
            pltpu.VMEM((2, W_DOWN_CHUNK, w_down.shape[1]), jnp.float32),
            pltpu.SemaphoreType.DMA((2,)),
            pltpu.VMEM((2, FFN_SEG, SUBLANES, D), jnp.float32),
            pltpu.VMEM((2, FFN_SEG, SUBLANES, D), jnp.float32),
            pltpu.SemaphoreType.DMA((2,)),
            pltpu.SemaphoreType.DMA((2,)),
            pltpu.VMEM((FFN_HALO, 2 * D_FF), jnp.float32),
            pltpu.VMEM((2, FFN_HALO + TM_FFN, FFN_CW), jnp.float32),
            pltpu.VMEM((2, FFN_HALO + TM_FFN, FFN_CW), jnp.float32),
            pltpu.VMEM((TM_FFN, D_FF), jnp.bfloat16),
        ],
        compiler_params=pltpu.CompilerParams(
            dimension_semantics=("arbitrary", "arbitrary"), vmem_limit_bytes=VMEM_LIMIT),
        name="ffn",
    )(h.reshape(tiles), g_pre, w_up, dw_w, dw_b, w_down, g_post)
    return out.reshape(B, T, D)


def kernel(x, norm_mix_pre, w_in, conv_dw_w, conv_dw_b, conv_ln_g, conv_ln_b, rel_bias, w_out,
           norm_mix_post, norm_ffn_pre, w_up, ffn_dw_w, ffn_dw_b, w_down, norm_ffn_post):
    h = x
    for l in range(norm_mix_pre.shape[0]):
        h = _mixer(h, norm_mix_pre[l][None], w_in[l],
                   jnp.broadcast_to(conv_dw_w[l][:, None, :], (CONV_KERNEL, SUBLANES, CONV_WIDTH)),
                   conv_dw_b[l][None], conv_ln_g[l][None], conv_ln_b[l][None],
                   _rel_distance_row(rel_bias[l]), w_out[l], norm_mix_post[l][None])
        h = _ffn(h, norm_ffn_pre[l][None], w_up[l], ffn_dw_w[l],
                 ffn_dw_b[l][None], w_down[l], norm_ffn_post[l][None])
    return h
```

```python
import jax
import jax.numpy as jnp
from jax import lax
from jax.experimental import pallas as pl
from jax.experimental.pallas import tpu as pltpu

D_MODEL = 1024
CHUNK = 64
N_LEFT_CHUNKS = 8
CONV_WIDTH = 512
ATTN_WIDTH = 512
HEAD_DIM = 64
N_HEADS = ATTN_WIDTH // HEAD_DIM
N_PAIRS = N_HEADS // 2
PAIR_W = 2 * HEAD_DIM
CONV_KERNEL = 31
MAX_REL = 128
D_FF = 2816
FFN_CONV_KERNEL = 3
EPS = 1e-6
NEG_INF = -1e30

TM_MIX = 512
HALO_H = 32
HALO_KV = N_LEFT_CHUNKS * CHUNK
CONV_RB = 32
OUT_RB = 256
PROJ_RB = 256
VT_COLS = 256
SUBLANES = 8
HSHIFT_LEAD = HALO_H - SUBLANES
HSHIFT_ROWS = HALO_H + TM_MIX - SUBLANES
QUAD = 4
QW = QUAD * CHUNK
KW = HALO_KV + QW
N_QUADS = TM_MIX // QW
PIECE = 128
SCORE_ROWS = 384
SOFTMAX_ROWS = 256
HALF_W = 128
FAR_LAG = (HALO_KV - MAX_REL) // CHUNK - 1
LOG2E = 1.4426950408889634
ROLL_W = 1024
TM_FFN = 512
FFN_CW = 256
FFN_DOWN_RB = 256
FFN_SEG = TM_FFN // SUBLANES
FFN_HALO = (FFN_CONV_KERNEL - 1) * SUBLANES
W_IN_CHUNK = 128
W_OUT_CHUNK = 256
W_UP_CHUNK = 128
W_DOWN_CHUNK = 352
VMEM_LIMIT = 56 * 1024 * 1024


def _rms(xf, g):
    return xf * lax.rsqrt(jnp.mean(xf * xf, axis=-1, keepdims=True) + EPS) * g


def _dot(a, b):
    return jnp.dot(a, b, preferred_element_type=jnp.float32)


def _dot_nt(a, b):
    return lax.dot_general(a, b, (((1,), (1,)), ((), ())), preferred_element_type=jnp.float32)


def _build_bias(rel_ref, bias_t):
    r_idx = lax.broadcasted_iota(jnp.int32, (KW, QW), 0) // CHUNK
    c_idx = lax.broadcasted_iota(jnp.int32, (KW, QW), 1) // CHUNK
    visible = (r_idx >= c_idx) & (r_idx <= c_idx + N_LEFT_CHUNKS)
    for h in range(N_HEADS):
        row = jnp.broadcast_to(rel_ref[h:h + 1, :], (QW, ROLL_W))
        nat = pltpu.roll(row, 0, 1, stride=1, stride_axis=0)[:, 0:KW]
        rel_to_far = (nat.T - rel_ref[h:h + 1, 0:1]) * LOG2E
        masked = jnp.where(visible, rel_to_far, NEG_INF)
        for half in range(QW // HALF_W):
            bias_t[h, half] = masked[:, half * HALF_W:(half + 1) * HALF_W]


def _block_kind(piece, half):
    lags = [kc - qc
            for kc in range(piece * PIECE // CHUNK, (piece + 1) * PIECE // CHUNK)
            for qc in range(half * HALF_W // CHUNK, (half + 1) * HALF_W // CHUNK)]
    live = any(0 <= lag <= N_LEFT_CHUNKS for lag in lags)
    plain = all(0 <= lag <= FAR_LAG for lag in lags)
    return live, plain


def _first_live(piece, half, lo):
    return not any(_block_kind(t, half)[0] for t in range(lo // PIECE, piece))


def _unit_blocks(piece):
    for head in range(2):
        for half in range(QW // HALF_W):
            live, plain = _block_kind(piece, half)
            if live:
                start = head * QW + half * HALF_W
                yield head, half, plain, slice(start, start + HALF_W)


N_LANE_BLOCKS = 2 * QW // HALF_W


def _score_steps(pair, quad, lo, qbuf, kbuf, bias_t, s_all, mx_all):
    s_ref, mx_ref = s_all.at[quad], mx_all.at[quad]
    k0 = quad * QW
    qt = qbuf[pl.ds(pl.multiple_of(pair * PAIR_W, PAIR_W), PAIR_W), k0:k0 + QW]
    zero = jnp.zeros((HEAD_DIM, QW), qt.dtype)
    qm = jnp.concatenate([jnp.concatenate([qt[0:HEAD_DIM], zero], axis=0),
                          jnp.concatenate([zero, qt[HEAD_DIM:PAIR_W]], axis=0)], axis=1)

    def step(r0, r1):
        s = _dot(kbuf[pair, k0 + r0:k0 + r1, :], qm)
        running = {}
        for piece in range(r0 // PIECE, r1 // PIECE):
            r = piece * PIECE
            for head, half, plain, lanes in _unit_blocks(piece):
                sh = s[r - r0:r - r0 + PIECE, lanes]
                if not plain:
                    sh = sh + bias_t[2 * pair + head, half, r:r + PIECE, :]
                s_ref[lanes.start // HALF_W, r:r + PIECE, :] = sh
                m = jnp.max(sh.reshape(PIECE // SUBLANES, SUBLANES, HALF_W), axis=0)
                key = (lanes.start, half)
                if key in running:
                    running[key] = jnp.maximum(running[key], m)
                elif _first_live(piece, half, lo):
                    running[key] = m
                else:
                    running[key] = jnp.maximum(mx_ref[:, lanes], m)
        for (start, _), m in running.items():
            mx_ref[:, start:start + HALF_W] = m

    starts = range(lo, KW, SCORE_ROWS)
    return [lambda r0=r0: step(r0, min(r0 + SCORE_ROWS, KW)) for r0 in starts]


def _softmax_steps(quad, lo, s_all, mx_all, p_all, sum_all):
    s_ref, mx_ref, p_ref, sum_ref = s_all.at[quad], mx_all.at[quad], p_all.at[quad], sum_all.at[quad]
    mx = jnp.max(mx_ref[...], axis=0, keepdims=True)

    def step(r0, r1):
        running = {}
        for piece in range(r0 // PIECE, r1 // PIECE):
            r = piece * PIECE
            for _, half, _, lanes in _unit_blocks(piece):
                blk = lanes.start // HALF_W
                e = jnp.exp2(s_ref[blk, r:r + PIECE, :] - mx[:, lanes])
                p_ref[blk, r:r + PIECE, :] = e.astype(jnp.bfloat16)
                part = jnp.sum(e.reshape(PIECE // SUBLANES, SUBLANES, HALF_W), axis=0)
                key = (lanes.start, half)
                if key in running:
                    running[key] = running[key] + part
                elif _first_live(piece, half, lo):
                    running[key] = part
                else:
                    running[key] = sum_ref[:, lanes] + part
        for (start, _), part in running.items():
            sum_ref[:, start:start + HALF_W] = part

    starts = range(lo, KW, SOFTMAX_ROWS)
    return [lambda r0=r0: step(r0, min(r0 + SOFTMAX_ROWS, KW)) for r0 in starts]


def _zero_dead_blocks(p_all):
    p_all[...] = jnp.zeros(p_all.shape, jnp.bfloat16)


def _weighted_values(pair, quad, lo, vtbuf, p_all, sum_all, out_t):
    row0 = pl.multiple_of(pair * PAIR_W, PAIR_W)
    k0 = quad * QW
    denom = jnp.sum(sum_all[quad], axis=0, keepdims=True)
    p = jnp.concatenate([p_all[quad, blk, lo:KW, :] for blk in range(N_LANE_BLOCKS)], axis=1)
    o_t = _dot(vtbuf[pl.ds(row0, PAIR_W), k0 + lo:k0 + KW], p)
    inv = 1.0 / denom
    for head in range(2):
        rows = slice(head * HEAD_DIM, (head + 1) * HEAD_DIM)
        lanes = slice(head * QW, (head + 1) * QW)
        out_t[pl.ds(row0 + head * HEAD_DIM, HEAD_DIM), k0:k0 + QW] = o_t[rows, lanes] * inv[:, lanes]


def _interleave(a_steps, b_steps):
    done = 0
    for t, a_step in enumerate(a_steps):
        a_step()
        upto = (t + 1) * len(b_steps) // len(a_steps)
        for b_step in b_steps[done:upto]:
            b_step()
        done = upto


def _load_weight_bf16(w_hbm, dst, stage, sem, after_chunk=None):
    chunk_rows = stage.shape[1]
    n_chunks = w_hbm.shape[0] // chunk_rows

    def copy(c):
        rows = slice(c * chunk_rows, (c + 1) * chunk_rows)
        return pltpu.make_async_copy(w_hbm.at[rows, :], stage.at[c % 2], sem.at[c % 2])

    copy(0).start()
    for c in range(n_chunks):
        if c + 1 < n_chunks:
            copy(c + 1).start()
        copy(c).wait()
        rows = slice(c * chunk_rows, (c + 1) * chunk_rows)
        dst[rows, :] = stage[c % 2].astype(jnp.bfloat16)
        if after_chunk is not None:
            after_chunk(c, rows, stage.at[c % 2])


def _alternate(a_steps, b_steps):
    for t in range(max(len(a_steps), len(b_steps))):
        if t < len(a_steps):
            a_steps[t]()
        if t < len(b_steps):
            b_steps[t]()


def _mixer_kernel(x_ref, g_pre_ref, w_in_hbm, dw_w_ref, dw_b_ref,
                  ln_g_ref, ln_b_ref, rel_ref, w_out_hbm, g_post_ref, o_ref,
                  w_in_ref, w_vt_ref, w_qt_ref, w_out_ref, stage_in, stage_out, w_sem,
                  hbuf, hshift, cbuf, qbuf, kbuf, vtbuf, bias_t, s_ref, mx_ref, p_ref, sum_ref,
                  out_t, mixbuf):
    b = pl.program_id(0)
    i = pl.program_id(1)
    c_q = 2 * CONV_WIDTH
    c_k = c_q + ATTN_WIDTH
    c_v = c_k + ATTN_WIDTH

    @pl.when((b == 0) & (i == 0))
    def _():
        def value_weights_transposed(c, rows, staged):
            w_vt_ref[:, rows] = staged[:, c_v:c_v + ATTN_WIDTH].T.astype(jnp.bfloat16)
            w_qt_ref[:, rows] = staged[:, c_q:c_q + ATTN_WIDTH].T.astype(jnp.bfloat16)

        _load_weight_bf16(w_in_hbm, w_in_ref, stage_in, w_sem, value_weights_transposed)
        _load_weight_bf16(w_out_hbm, w_out_ref, stage_out, w_sem)
        _build_bias(rel_ref, bias_t)
        _zero_dead_blocks(p_ref)

    @pl.when(i == 0)
    def _():
        hbuf[0:HALO_H, :] = jnp.zeros((HALO_H, CONV_WIDTH), jnp.float32)
        kbuf[:, 0:HALO_KV, :] = jnp.zeros((N_PAIRS, HALO_KV, PAIR_W), jnp.bfloat16)
        vtbuf[:, 0:HALO_KV] = jnp.zeros((ATTN_WIDTH, HALO_KV), jnp.bfloat16)

    def conv_block(base):
        acc = jnp.broadcast_to(dw_b_ref[...][None], (CONV_RB // SUBLANES, SUBLANES, CONV_WIDTH))
        for j in range(CONV_KERNEL):
            off = HALO_H - (CONV_KERNEL - 1) + j
            r = off % SUBLANES
            rows = slice(base + off - r, base + off - r + CONV_RB)
            tap = hbuf[rows, :] if r == 0 else hshift[r - 1, rows, :]
            acc = acc + dw_w_ref[j][None] * tap.reshape(acc.shape)
        cbuf[base:base + CONV_RB, :] = acc.reshape(CONV_RB, CONV_WIDTH)

    pending = []

    def project_steps(blk):
        lo_row, hi_row = blk * PROJ_RB, (blk + 1) * PROJ_RB
        rows = slice(lo_row, hi_row)
        state = {}

        def norm_and_value():
            state["u"] = _rms(x_ref[rows, :], g_pre_ref[...]).astype(jnp.bfloat16)
            state["a_val"] = _dot(state["u"], w_in_ref[:, 0:CONV_WIDTH])

        def gate():
            a_gate = _dot(state["u"], w_in_ref[:, CONV_WIDTH:c_q])
            hbuf[HALO_H + lo_row:HALO_H + hi_row, :] = state["a_val"] * jax.nn.sigmoid(a_gate)

        def queries():
            q_t = _dot_nt(w_qt_ref[...], state["u"]) * (HEAD_DIM ** -0.5 * LOG2E)
            qbuf[:, rows] = q_t.astype(jnp.bfloat16)

        def keys_values():
            k = _dot(state["u"], w_in_ref[:, c_k:c_k + ATTN_WIDTH]).astype(jnp.bfloat16)
            for p in range(N_PAIRS):
                kbuf[p, HALO_KV + lo_row:HALO_KV + hi_row, :] = k[:, p * PAIR_W:(p + 1) * PAIR_W]
            pending.append(state["u"])
            if hi_row % VT_COLS == 0:
                u_wide = pending[0] if len(pending) == 1 else jnp.concatenate(pending, axis=0)
                vtbuf[:, HALO_KV + hi_row - VT_COLS:HALO_KV + hi_row] = (
                    _dot_nt(w_vt_ref[...], u_wide).astype(jnp.bfloat16))
                pending.clear()

        return [norm_and_value, gate, queries, keys_values]

    def mix_steps(blk):
        lo_row, hi_row = blk * PROJ_RB, (blk + 1) * PROJ_RB
        rows = slice(lo_row, hi_row)

        def shifted_copies():
            sh_lo = 0 if blk == 0 else lo_row + HSHIFT_LEAD
            sh_hi = hi_row + HSHIFT_LEAD
            for r in range(1, SUBLANES):
                hshift[r - 1, sh_lo:sh_hi, :] = hbuf[sh_lo + r:sh_hi + r, :]

        def norm_swish():
            c = cbuf[rows, :]
            mu = jnp.mean(c, axis=-1, keepdims=True)
            xc = c - mu
            var = jnp.mean(xc * xc, axis=-1, keepdims=True)
            y = xc * lax.rsqrt(var + EPS) * ln_g_ref[...] + ln_b_ref[...]
            y = y * jax.nn.sigmoid(y)
            mixbuf[rows, :] = y.astype(jnp.bfloat16)

        convs = [lambda base=lo_row + rb * CONV_RB: conv_block(base)
                 for rb in range(PROJ_RB // CONV_RB)]
        return [shifted_copies] + convs + [norm_swish]

    n_blocks = TM_MIX // PROJ_RB
    for step in project_steps(0):
        step()
    for blk in range(n_blocks):
        nxt = project_steps(blk + 1) if blk + 1 < n_blocks else []
        _alternate(nxt, mix_steps(blk))

    def attn_loop(first_tile):
        lo = [max(0, HALO_KV - quad * QW) if first_tile else 0 for quad in range(N_QUADS)]

        def scores(pair, quad):
            return _score_steps(pair, quad, lo[quad], qbuf, kbuf, bias_t, s_ref, mx_ref)

        for step in scores(0, 0):
            step()

        def body(pair, carry):
            for quad in range(N_QUADS):
                if quad + 1 < N_QUADS:
                    nxt = scores(pair, quad + 1)
                else:
                    nxt = scores(jnp.minimum(pair + 1, N_PAIRS - 1), 0)
                _interleave(nxt, _softmax_steps(quad, lo[quad], s_ref, mx_ref, p_ref, sum_ref))
                _weighted_values(pair, quad, lo[quad], vtbuf, p_ref, sum_ref, out_t)
            return carry
        lax.fori_loop(0, N_PAIRS, body, 0)

    @pl.when(i == 0)
    def _():
        attn_loop(True)

    @pl.when(i > 0)
    def _():
        attn_loop(False)

    for blk in range(TM_MIX // OUT_RB):
        rows = slice(blk * OUT_RB, (blk + 1) * OUT_RB)
        attn = out_t[:, rows].T.astype(jnp.bfloat16)
        mixed = _dot(jnp.concatenate([mixbuf[rows, :], attn], axis=1), w_out_ref[...])
        o_ref[rows, :] = x_ref[rows, :] + _rms(mixed, g_post_ref[...])

    hbuf[0:HALO_H, :] = hbuf[TM_MIX:TM_MIX + HALO_H, :]
    kbuf[:, 0:HALO_KV, :] = kbuf[:, TM_MIX:TM_MIX + HALO_KV, :]
    vtbuf[:, 0:HALO_KV] = vtbuf[:, TM_MIX:TM_MIX + HALO_KV]


def _ffn_kernel(h_hbm, g_pre_ref, w_up_hbm, dw_w_ref, dw_b_ref, w_down_hbm, g_post_ref, o_hbm,
                w_up_ref, w_down_ref, stage_up, stage_down, w_sem,
                xin, xout, sem_in, sem_out, carry, gbuf, vbuf, actbuf):
    i = pl.program_id(1)
    n = pl.program_id(0) * pl.num_programs(1) + i
    n_tiles = pl.num_programs(0) * pl.num_programs(1)
    slot = n % 2

    def in_copy(tile, sl, s):
        return pltpu.make_async_copy(h_hbm.at[tile, s], xin.at[sl, :, s, :], sem_in.at[sl])

    def out_copy(tile, sl, s):
        return pltpu.make_async_copy(xout.at[sl, :, s, :], o_hbm.at[tile, s], sem_out.at[sl])

    @pl.when(n == 0)
    def _():
        for s in range(SUBLANES):
            in_copy(0, 0, s).start()
        _load_weight_bf16(w_up_hbm, w_up_ref, stage_up, w_sem)
        _load_weight_bf16(w_down_hbm, w_down_ref, stage_down, w_sem)

    @pl.when(n + 1 < n_tiles)
    def _():
        for s in range(SUBLANES):
            in_copy(n + 1, 1 - slot, s).start()

    @pl.when(i == 0)
    def _():
        carry[...] = jnp.zeros(carry.shape, jnp.float32)

    @pl.when(n >= 2)
    def _():
        for s in range(SUBLANES):
            out_copy(n - 2, slot, s).wait()

    for s in range(SUBLANES):
        in_copy(n, slot, s).wait()

    xt = xin[slot].reshape(TM_FFN, D_MODEL)
    u = _rms(xt, g_pre_ref[...]).astype(jnp.bfloat16)
    first_sublane = lax.broadcasted_iota(jnp.int32, (SUBLANES, FFN_CW), 0) == 0

    def up_conv(buf, cols):
        hc = _dot(u, w_up_ref[:, cols])
        for k in (1, 2):
            last = hc[TM_FFN - k * SUBLANES:TM_FFN - (k - 1) * SUBLANES, :]
            prev = carry[(2 - k) * SUBLANES:(3 - k) * SUBLANES, cols]
            buf[(2 - k) * SUBLANES:(3 - k) * SUBLANES, :] = jnp.where(
                first_sublane, pltpu.roll(prev, 1, 0), pltpu.roll(last, 1, 0))
            carry[(2 - k) * SUBLANES:(3 - k) * SUBLANES, cols] = last
        buf[FFN_HALO:FFN_HALO + TM_FFN, :] = hc
        y = dw_b_ref[:, cols] + dw_w_ref[2:3, cols] * hc
        y = y + dw_w_ref[1:2, cols] * buf[SUBLANES:SUBLANES + TM_FFN, :]
        y = y + dw_w_ref[0:1, cols] * buf[0:TM_FFN, :]
        return y

    for c in range(D_FF // FFN_CW):
        gate = up_conv(gbuf.at[c % 2], slice(c * FFN_CW, (c + 1) * FFN_CW))
        val = up_conv(vbuf.at[c % 2], slice(D_FF + c * FFN_CW, D_FF + (c + 1) * FFN_CW))
        actbuf[:, c * FFN_CW:(c + 1) * FFN_CW] = (jax.nn.gelu(gate) * val).astype(jnp.bfloat16)
    for blk in range(TM_FFN // FFN_DOWN_RB):
        rows = slice(blk * FFN_DOWN_RB, (blk + 1) * FFN_DOWN_RB)
        slabs = slice(blk * FFN_DOWN_RB // SUBLANES, (blk + 1) * FFN_DOWN_RB // SUBLANES)
        f = _dot(actbuf[rows, :], w_down_ref[...])
        y = xt[rows, :] + _rms(f, g_post_ref[...])
        xout[slot, slabs] = y.reshape(FFN_DOWN_RB // SUBLANES, SUBLANES, D_MODEL)
    for s in range(SUBLANES):
        out_copy(n, slot, s).start()

    @pl.when(n == n_tiles - 1)
    def _():
        @pl.when(n >= 1)
        def _():
            for s in range(SUBLANES):
                out_copy(n - 1, 1 - slot, s).wait()
        for s in range(SUBLANES):
            out_copy(n, slot, s).wait()


def _rel_distance_row(rel_table):
    h = rel_table.shape[0]
    far = rel_table[:, 2 * MAX_REL:2 * MAX_REL + 1]
    n_far = HALO_KV - MAX_REL + 1
    near = jnp.flip(rel_table, axis=1)[:, 1:]
    n_tail = ROLL_W - n_far - near.shape[1]
    return jnp.concatenate([jnp.broadcast_to(far, (h, n_far)), near,
                            jnp.broadcast_to(far, (h, n_tail))], axis=1)


def _const_spec(shape):
    return pl.BlockSpec(shape, lambda b, i: (0,) * len(shape), pipeline_mode=pl.Buffered(1))


def _mixer(x, g_pre, w_in, dw_w, dw_b, ln_g, ln_b, rel_row, w_out, g_post):
    B, T, D = x.shape
    row_spec = pl.BlockSpec((None, TM_MIX, D), lambda b, i: (b, i, 0))
    hbm_spec = pl.BlockSpec(memory_space=pl.ANY)
    consts = (g_pre, w_in, dw_w, dw_b, ln_g, ln_b, rel_row, w_out, g_post)
    const_specs = [hbm_spec if c is w_in or c is w_out else _const_spec(c.shape) for c in consts]
    return pl.pallas_call(
        _mixer_kernel,
        grid=(B, T // TM_MIX),
        in_specs=[row_spec] + const_specs,
        out_specs=row_spec,
        out_shape=jax.ShapeDtypeStruct(x.shape, x.dtype),
        scratch_shapes=[
            pltpu.VMEM(w_in.shape, jnp.bfloat16),
            pltpu.VMEM((ATTN_WIDTH, D), jnp.bfloat16),
            pltpu.VMEM((ATTN_WIDTH, D), jnp.bfloat16),
            pltpu.VMEM(w_out.shape, jnp.bfloat16),
            pltpu.VMEM((2, W_IN_CHUNK, w_in.shape[1]), jnp.float32),
            pltpu.VMEM((2, W_OUT_CHUNK, w_out.shape[1]), jnp.float32),
            pltpu.SemaphoreType.DMA((2,)),
            pltpu.VMEM((HALO_H + TM_MIX, CONV_WIDTH), jnp.float32),
            pltpu.VMEM((SUBLANES - 1, HSHIFT_ROWS, CONV_WIDTH), jnp.float32),
            pltpu.VMEM((TM_MIX, CONV_WIDTH), jnp.float32),
            pltpu.VMEM((ATTN_WIDTH, TM_MIX), jnp.bfloat16),
            pltpu.VMEM((N_PAIRS, HALO_KV + TM_MIX, PAIR_W), jnp.bfloat16),
            pltpu.VMEM((ATTN_WIDTH, HALO_KV + TM_MIX), jnp.bfloat16),
            pltpu.VMEM((N_HEADS, QW // HALF_W, KW, HALF_W), jnp.float32),
            pltpu.VMEM((N_QUADS, N_LANE_BLOCKS, KW, HALF_W), jnp.float32),
            pltpu.VMEM((N_QUADS, SUBLANES, 2 * QW), jnp.float32),
            pltpu.VMEM((N_QUADS, N_LANE_BLOCKS, KW, HALF_W), jnp.bfloat16),
            pltpu.VMEM((N_QUADS, SUBLANES, 2 * QW), jnp.float32),
            pltpu.VMEM((ATTN_WIDTH, TM_MIX), jnp.float32),
            pltpu.VMEM((TM_MIX, CONV_WIDTH), jnp.bfloat16),
        ],
        compiler_params=pltpu.CompilerParams(
            dimension_semantics=("arbitrary", "arbitrary"), vmem_limit_bytes=VMEM_LIMIT),
        name="mixer",
    )(x, *consts)


def _ffn(h, g_pre, w_up, dw_w, dw_b, w_down, g_post):
    B, T, D = h.shape
    n_t = T // TM_FFN
    tiles = (B * n_t, SUBLANES, FFN_SEG, D)
    hbm_spec = pl.BlockSpec(memory_space=pl.ANY)
    out = pl.pallas_call(
        _ffn_kernel,
        grid=(B, n_t),
        in_specs=[hbm_spec,
                  _const_spec((1, D)), hbm_spec, _const_spec(dw_w.shape),
                  _const_spec((1, 2 * D_FF)), hbm_spec, _const_spec((1, D))],
        out_specs=hbm_spec,
        out_shape=jax.ShapeDtypeStruct(tiles, h.dtype),
        scratch_shapes=[
            pltpu.VMEM(w_up.shape, jnp.bfloat16),
            pltpu.VMEM(w_down.shape, jnp.bfloat16),
            pltpu.VMEM((2, W_UP_CHUNK, w_up.shape[1]), jnp.float32),
            pltpu.VMEM((2, W_DOWN_CHUNK, w_down.shape[1]), jnp.float32),
            pltpu.SemaphoreType.DMA((2,)),
            pltpu.VMEM((2, FFN_SEG, SUBLANES, D), jnp.float32),
            pltpu.VMEM((2, FFN_SEG, SUBLANES, D), jnp.float32),
            pltpu.SemaphoreType.DMA((2,)),
            pltpu.SemaphoreType.DMA((2,)),
            pltpu.VMEM((FFN_HALO, 2 * D_FF), jnp.float32),
            pltpu.VMEM((2, FFN_HALO + TM_FFN, FFN_CW), jnp.float32),
            pltpu.VMEM((2, FFN_HALO + TM_FFN, FFN_CW), jnp.float32),
            pltpu.VMEM((TM_FFN, D_FF), jnp.bfloat16),
        ],
        compiler_params=pltpu.CompilerParams(
            dimension_semantics=("arbitrary", "arbitrary"), vmem_limit_bytes=VMEM_LIMIT),
        name="ffn",
    )(h.reshape(tiles), g_pre, w_up, dw_w, dw_b, w_down, g_post)
    return out.reshape(B, T, D)


def kernel(x, norm_mix_pre, w_in, conv_dw_w, conv_dw_b, conv_ln_g, conv_ln_b, rel_bias, w_out,
           norm_mix_post, norm_ffn_pre, w_up, ffn_dw_w, ffn_dw_b, w_down, norm_ffn_post):
    h = x
    for l in range(norm_mix_pre.shape[0]):
        h = _mixer(h, norm_mix_pre[l][None], w_in[l],
                   jnp.broadcast_to(conv_dw_w[l][:, None, :], (CONV_KERNEL, SUBLANES, CONV_WIDTH)),
                   conv_dw_b[l][None], conv_ln_g[l][None], conv_ln_b[l][None],
                   _rel_distance_row(rel_bias[l]), w_out[l], norm_mix_post[l][None])
        h = _ffn(h, norm_ffn_pre[l][None], w_up[l], ffn_dw_w[l],
                 ffn_dw_b[l][None], w_down[l], norm_ffn_post[l][None])
    return h
```

```python
import jax
import jax.numpy as jnp
from jax import lax
from jax.experimental import pallas as pl
from jax.experimental.pallas import tpu as pltpu

D_MODEL = 1024
CHUNK = 64
N_LEFT_CHUNKS = 8
CONV_WIDTH = 512
ATTN_WIDTH = 512
HEAD_DIM = 64
N_HEADS = ATTN_WIDTH // HEAD_DIM
N_PAIRS = N_HEADS // 2
PAIR_W = 2 * HEAD_DIM
CONV_KERNEL = 31
MAX_REL = 128
D_FF = 2816
FFN_CONV_KERNEL = 3
EPS = 1e-6
NEG_INF = -1e30

TM_MIX = 512
HALO_H = 32
HALO_KV = N_LEFT_CHUNKS * CHUNK
CONV_RB = 32
OUT_RB = 256
PROJ_RB = 256
VT_COLS = 256
SUBLANES = 8
HSHIFT_LEAD = HALO_H - SUBLANES
HSHIFT_ROWS = HALO_H + TM_MIX - SUBLANES
QUAD = 4
QW = QUAD * CHUNK
KW = HALO_KV + QW
N_QUADS = TM_MIX // QW
ONES_ROWS = 16
PIECE = 128
SCORE_ROWS = 384
SOFTMAX_ROWS = 256
HALF_W = 128
FAR_LAG = (HALO_KV - MAX_REL) // CHUNK - 1
LOG2E = 1.4426950408889634
ROLL_W = 1024
TM_FFN = 512
FFN_CW = 256
FFN_DOWN_RB = 256
FFN_SEG = TM_FFN // SUBLANES
FFN_HALO = (FFN_CONV_KERNEL - 1) * SUBLANES
W_IN_CHUNK = 128
W_OUT_CHUNK = 256
W_UP_CHUNK = 128
W_DOWN_CHUNK = 352
VMEM_LIMIT = 56 * 1024 * 1024


def _rms(xf, g):
    return xf * lax.rsqrt(jnp.mean(xf * xf, axis=-1, keepdims=True) + EPS) * g


def _dot(a, b):
    return jnp.dot(a, b, preferred_element_type=jnp.float32)


def _dot_nt(a, b):
    return lax.dot_general(a, b, (((1,), (1,)), ((), ())), preferred_element_type=jnp.float32)


def _build_bias(rel_ref, bias_t):
    r_idx = lax.broadcasted_iota(jnp.int32, (KW, QW), 0) // CHUNK
    c_idx = lax.broadcasted_iota(jnp.int32, (KW, QW), 1) // CHUNK
    visible = (r_idx >= c_idx) & (r_idx <= c_idx + N_LEFT_CHUNKS)
    for h in range(N_HEADS):
        row = jnp.broadcast_to(rel_ref[h:h + 1, :], (QW, ROLL_W))
        nat = pltpu.roll(row, 0, 1, stride=1, stride_axis=0)[:, 0:KW]
        rel_to_far = (nat.T - rel_ref[h:h + 1, 0:1]) * LOG2E
        masked = jnp.where(visible, rel_to_far, NEG_INF)
        for half in range(QW // HALF_W):
            bias_t[h, half] = masked[:, half * HALF_W:(half + 1) * HALF_W]


def _block_kind(piece, half):
    lags = [kc - qc
            for kc in range(piece * PIECE // CHUNK, (piece + 1) * PIECE // CHUNK)
            for qc in range(half * HALF_W // CHUNK, (half + 1) * HALF_W // CHUNK)]
    live = any(0 <= lag <= N_LEFT_CHUNKS for lag in lags)
    plain = all(0 <= lag <= FAR_LAG for lag in lags)
    return live, plain


def _first_live(piece, half, lo):
    return not any(_block_kind(t, half)[0] for t in range(lo // PIECE, piece))


def _unit_blocks(piece):
    for head in range(2):
        for half in range(QW // HALF_W):
            live, plain = _block_kind(piece, half)
            if live:
                start = head * QW + half * HALF_W
                yield head, half, plain, slice(start, start + HALF_W)


N_LANE_BLOCKS = 2 * QW // HALF_W


def _score_steps(pair, quad, lo, qbuf, kbuf, bias_t, s_all, mx_all):
    s_ref, mx_ref = s_all.at[quad], mx_all.at[quad]
    k0 = quad * QW
    qt = qbuf[pl.ds(pl.multiple_of(pair * PAIR_W, PAIR_W), PAIR_W), k0:k0 + QW]
    zero = jnp.zeros((HEAD_DIM, QW), qt.dtype)
    qm = jnp.concatenate([jnp.concatenate([qt[0:HEAD_DIM], zero], axis=0),
                          jnp.concatenate([zero, qt[HEAD_DIM:PAIR_W]], axis=0)], axis=1)

    def step(r0, r1):
        s = _dot(kbuf[pair, k0 + r0:k0 + r1, :], qm)
        running = {}
        for piece in range(r0 // PIECE, r1 // PIECE):
            r = piece * PIECE
            for head, half, plain, lanes in _unit_blocks(piece):
                sh = s[r - r0:r - r0 + PIECE, lanes]
                if not plain:
                    sh = sh + bias_t[2 * pair + head, half, r:r + PIECE, :]
                s_ref[lanes.start // HALF_W, r:r + PIECE, :] = sh
                m = jnp.max(sh.reshape(PIECE // SUBLANES, SUBLANES, HALF_W), axis=0)
                key = (lanes.start, half)
                if key in running:
                    running[key] = jnp.maximum(running[key], m)
                elif _first_live(piece, half, lo):
                    running[key] = m
                else:
                    running[key] = jnp.maximum(mx_ref[:, lanes], m)
        for (start, _), m in running.items():
            mx_ref[:, start:start + HALF_W] = m

    starts = range(lo, KW, SCORE_ROWS)
    return [lambda r0=r0: step(r0, min(r0 + SCORE_ROWS, KW)) for r0 in starts]


def _softmax_steps(quad, lo, s_all, mx_all, p_all, sum_all):
    s_ref, mx_ref, p_ref, sum_ref = s_all.at[quad], mx_all.at[quad], p_all.at[quad], sum_all.at[quad]
    mx = jnp.max(mx_ref[...], axis=0, keepdims=True)

    def step(r0, r1):
        for piece in range(r0 // PIECE, r1 // PIECE):
            r = piece * PIECE
            for _, half, _, lanes in _unit_blocks(piece):
                blk = lanes.start // HALF_W
                e = jnp.exp2(s_ref[blk, r:r + PIECE, :] - mx[:, lanes])
                p_ref[blk, r:r + PIECE, :] = e.astype(jnp.bfloat16)

    starts = range(lo, KW, SOFTMAX_ROWS)
    return [lambda r0=r0: step(r0, min(r0 + SOFTMAX_ROWS, KW)) for r0 in starts]


def _zero_dead_blocks(p_all):
    p_all[...] = jnp.zeros(p_all.shape, jnp.bfloat16)


def _weighted_values(pair, quad, lo, vtbuf, p_all, sum_all, out_t):
    row0 = pl.multiple_of(pair * PAIR_W, PAIR_W)
    k0 = quad * QW
    p = jnp.concatenate([p_all[quad, blk, lo:KW, :] for blk in range(N_LANE_BLOCKS)], axis=1)
    ones = jnp.ones((ONES_ROWS, KW - lo), jnp.bfloat16)
    vt1 = jnp.concatenate([vtbuf[pl.ds(row0, PAIR_W), k0 + lo:k0 + KW], ones], axis=0)
    o_t = _dot(vt1, p)
    inv = 1.0 / o_t[PAIR_W:PAIR_W + 1, :]
    for head in range(2):
        rows = slice(head * HEAD_DIM, (head + 1) * HEAD_DIM)
        lanes = slice(head * QW, (head + 1) * QW)
        out_t[pl.ds(row0 + head * HEAD_DIM, HEAD_DIM), k0:k0 + QW] = o_t[rows, lanes] * inv[:, lanes]


def _interleave(a_steps, b_steps):
    done = 0
    for t, a_step in enumerate(a_steps):
        a_step()
        upto = (t + 1) * len(b_steps) // len(a_steps)
        for b_step in b_steps[done:upto]:
            b_step()
        done = upto


def _load_weight_bf16(w_hbm, dst, stage, sem, after_chunk=None):
    chunk_rows = stage.shape[1]
    n_chunks = w_hbm.shape[0] // chunk_rows

    def copy(c):
        rows = slice(c * chunk_rows, (c + 1) * chunk_rows)
        return pltpu.make_async_copy(w_hbm.at[rows, :], stage.at[c % 2], sem.at[c % 2])

    copy(0).start()
    for c in range(n_chunks):
        if c + 1 < n_chunks:
            copy(c + 1).start()
        copy(c).wait()
        rows = slice(c * chunk_rows, (c + 1) * chunk_rows)
        dst[rows, :] = stage[c % 2].astype(jnp.bfloat16)
        if after_chunk is not None:
            after_chunk(c, rows, stage.at[c % 2])


def _alternate(a_steps, b_steps):
    for t in range(max(len(a_steps), len(b_steps))):
        if t < len(a_steps):
            a_steps[t]()
        if t < len(b_steps):
            b_steps[t]()


def _mixer_kernel(x_ref, g_pre_ref, w_in_hbm, dw_w_ref, dw_b_ref,
                  ln_g_ref, ln_b_ref, rel_ref, w_out_hbm, g_post_ref, o_ref,
                  w_in_ref, w_vt_ref, w_qt_ref, w_out_ref, stage_in, stage_out, w_sem,
                  hbuf, hshift, cbuf, qbuf, kbuf, vtbuf, bias_t, s_ref, mx_ref, p_ref, sum_ref,
                  out_t, mixbuf):
    b = pl.program_id(0)
    i = pl.program_id(1)
    c_q = 2 * CONV_WIDTH
    c_k = c_q + ATTN_WIDTH
    c_v = c_k + ATTN_WIDTH

    @pl.when((b == 0) & (i == 0))
    def _():
        def value_weights_transposed(c, rows, staged):
            w_vt_ref[:, rows] = staged[:, c_v:c_v + ATTN_WIDTH].T.astype(jnp.bfloat16)
            w_qt_ref[:, rows] = staged[:, c_q:c_q + ATTN_WIDTH].T.astype(jnp.bfloat16)

        _load_weight_bf16(w_in_hbm, w_in_ref, stage_in, w_sem, value_weights_transposed)
        _load_weight_bf16(w_out_hbm, w_out_ref, stage_out, w_sem)
        _build_bias(rel_ref, bias_t)
        _zero_dead_blocks(p_ref)

    @pl.when(i == 0)
    def _():
        hbuf[0:HALO_H, :] = jnp.zeros((HALO_H, CONV_WIDTH), jnp.float32)
        kbuf[:, 0:HALO_KV, :] = jnp.zeros((N_PAIRS, HALO_KV, PAIR_W), jnp.bfloat16)
        vtbuf[:, 0:HALO_KV] = jnp.zeros((ATTN_WIDTH, HALO_KV), jnp.bfloat16)

    def conv_block(base):
        acc = jnp.broadcast_to(dw_b_ref[...][None], (CONV_RB // SUBLANES, SUBLANES, CONV_WIDTH))
        for j in range(CONV_KERNEL):
            off = HALO_H - (CONV_KERNEL - 1) + j
            r = off % SUBLANES
            rows = slice(base + off - r, base + off - r + CONV_RB)
            tap = hbuf[rows, :] if r == 0 else hshift[r - 1, rows, :]
            acc = acc + dw_w_ref[j][None] * tap.reshape(acc.shape)
        cbuf[base:base + CONV_RB, :] = acc.reshape(CONV_RB, CONV_WIDTH)

    pending = []

    def project_steps(blk):
        lo_row, hi_row = blk * PROJ_RB, (blk + 1) * PROJ_RB
        rows = slice(lo_row, hi_row)
        state = {}

        def norm_and_value():
            state["u"] = _rms(x_ref[rows, :], g_pre_ref[...]).astype(jnp.bfloat16)
            state["a_val"] = _dot(state["u"], w_in_ref[:, 0:CONV_WIDTH])

        def gate():
            a_gate = _dot(state["u"], w_in_ref[:, CONV_WIDTH:c_q])
            hbuf[HALO_H + lo_row:HALO_H + hi_row, :] = state["a_val"] * jax.nn.sigmoid(a_gate)

        def queries():
            q_t = _dot_nt(w_qt_ref[...], state["u"]) * (HEAD_DIM ** -0.5 * LOG2E)
            qbuf[:, rows] = q_t.astype(jnp.bfloat16)

        def keys_values():
            k = _dot(state["u"], w_in_ref[:, c_k:c_k + ATTN_WIDTH]).astype(jnp.bfloat16)
            for p in range(N_PAIRS):
                kbuf[p, HALO_KV + lo_row:HALO_KV + hi_row, :] = k[:, p * PAIR_W:(p + 1) * PAIR_W]
            pending.append(state["u"])
            if hi_row % VT_COLS == 0:
                u_wide = pending[0] if len(pending) == 1 else jnp.concatenate(pending, axis=0)
                vtbuf[:, HALO_KV + hi_row - VT_COLS:HALO_KV + hi_row] = (
                    _dot_nt(w_vt_ref[...], u_wide).astype(jnp.bfloat16))
                pending.clear()

        return [norm_and_value, gate, queries, keys_values]

    def mix_steps(blk):
        lo_row, hi_row = blk * PROJ_RB, (blk + 1) * PROJ_RB
        rows = slice(lo_row, hi_row)

        def shifted_copies():
            sh_lo = 0 if blk == 0 else lo_row + HSHIFT_LEAD
            sh_hi = hi_row + HSHIFT_LEAD
            for r in range(1, SUBLANES):
                hshift[r - 1, sh_lo:sh_hi, :] = hbuf[sh_lo + r:sh_hi + r, :]

        def norm_swish():
            c = cbuf[rows, :]
            mu = jnp.mean(c, axis=-1, keepdims=True)
            xc = c - mu
            var = jnp.mean(xc * xc, axis=-1, keepdims=True)
            y = xc * lax.rsqrt(var + EPS) * ln_g_ref[...] + ln_b_ref[...]
            y = y * jax.nn.sigmoid(y)
            mixbuf[rows, 0:CONV_WIDTH] = y.astype(jnp.bfloat16)

        convs = [lambda base=lo_row + rb * CONV_RB: conv_block(base)
                 for rb in range(PROJ_RB // CONV_RB)]
        return [shifted_copies] + convs + [norm_swish]

    n_blocks = TM_MIX // PROJ_RB
    for step in project_steps(0):
        step()
    for blk in range(n_blocks):
        nxt = project_steps(blk + 1) if blk + 1 < n_blocks else []
        _alternate(nxt, mix_steps(blk))

    def attn_loop(first_tile):
        lo = [max(0, HALO_KV - quad * QW) if first_tile else 0 for quad in range(N_QUADS)]

        def scores(pair, quad):
            return _score_steps(pair, quad, lo[quad], qbuf, kbuf, bias_t, s_ref, mx_ref)

        for step in scores(0, 0):
            step()

        def body(pair, carry):
            for quad in range(N_QUADS):
                if quad + 1 < N_QUADS:
                    nxt = scores(pair, quad + 1)
                else:
                    nxt = scores(jnp.minimum(pair + 1, N_PAIRS - 1), 0)
                _interleave(nxt, _softmax_steps(quad, lo[quad], s_ref, mx_ref, p_ref, sum_ref))
                _weighted_values(pair, quad, lo[quad], vtbuf, p_ref, sum_ref, out_t)
            return carry
        lax.fori_loop(0, N_PAIRS, body, 0)

    @pl.when(i == 0)
    def _():
        attn_loop(True)

    @pl.when(i > 0)
    def _():
        attn_loop(False)

    for blk in range(TM_MIX // OUT_RB):
        rows = slice(blk * OUT_RB, (blk + 1) * OUT_RB)
        attn = out_t[:, rows].T.astype(jnp.bfloat16)
        mixed = _dot(jnp.concatenate([mixbuf[rows, 0:CONV_WIDTH], attn], axis=1), w_out_ref[...])
        o_ref[rows, :] = x_ref[rows, :] + _rms(mixed, g_post_ref[...])

    hbuf[0:HALO_H, :] = hbuf[TM_MIX:TM_MIX + HALO_H, :]
    kbuf[:, 0:HALO_KV, :] = kbuf[:, TM_MIX:TM_MIX + HALO_KV, :]
    vtbuf[:, 0:HALO_KV] = vtbuf[:, TM_MIX:TM_MIX + HALO_KV]


def _ffn_kernel(h_hbm, g_pre_ref, w_up_hbm, dw_w_ref, dw_b_ref, w_down_hbm, g_post_ref, o_hbm,
                w_up_ref, w_down_ref, stage_up, stage_down, w_sem,
                xin, xout, sem_in, sem_out, carry, gbuf, vbuf, actbuf):
    i = pl.program_id(1)
    n = pl.program_id(0) * pl.num_programs(1) + i
    n_tiles = pl.num_programs(0) * pl.num_programs(1)
    slot = n % 2

    def in_copy(tile, sl, s):
        return pltpu.make_async_copy(h_hbm.at[tile, s], xin.at[sl, :, s, :], sem_in.at[sl])

    def out_copy(tile, sl, s):
        return pltpu.make_async_copy(xout.at[sl, :, s, :], o_hbm.at[tile, s], sem_out.at[sl])

    @pl.when(n == 0)
    def _():
        for s in range(SUBLANES):
            in_copy(0, 0, s).start()
        _load_weight_bf16(w_up_hbm, w_up_ref, stage_up, w_sem)
        _load_weight_bf16(w_down_hbm, w_down_ref, stage_down, w_sem)

    @pl.when(n + 1 < n_tiles)
    def _():
        for s in range(SUBLANES):
            in_copy(n + 1, 1 - slot, s).start()

    @pl.when(i == 0)
    def _():
        carry[...] = jnp.zeros(carry.shape, jnp.float32)

    @pl.when(n >= 2)
    def _():
        for s in range(SUBLANES):
            out_copy(n - 2, slot, s).wait()

    for s in range(SUBLANES):
        in_copy(n, slot, s).wait()

    xt = xin[slot].reshape(TM_FFN, D_MODEL)
    u = _rms(xt, g_pre_ref[...]).astype(jnp.bfloat16)
    first_sublane = lax.broadcasted_iota(jnp.int32, (SUBLANES, FFN_CW), 0) == 0

    def up_conv(buf, cols):
        hc = _dot(u, w_up_ref[:, cols])
        for k in (1, 2):
            last = hc[TM_FFN - k * SUBLANES:TM_FFN - (k - 1) * SUBLANES, :]
            prev = carry[(2 - k) * SUBLANES:(3 - k) * SUBLANES, cols]
            buf[(2 - k) * SUBLANES:(3 - k) * SUBLANES, :] = jnp.where(
                first_sublane, pltpu.roll(prev, 1, 0), pltpu.roll(last, 1, 0))
            carry[(2 - k) * SUBLANES:(3 - k) * SUBLANES, cols] = last
        buf[FFN_HALO:FFN_HALO + TM_FFN, :] = hc
        y = dw_b_ref[:, cols] + dw_w_ref[2:3, cols] * hc
        y = y + dw_w_ref[1:2, cols] * buf[SUBLANES:SUBLANES + TM_FFN, :]
        y = y + dw_w_ref[0:1, cols] * buf[0:TM_FFN, :]
        return y

    for c in range(D_FF // FFN_CW):
        gate = up_conv(gbuf.at[c % 2], slice(c * FFN_CW, (c + 1) * FFN_CW))
        val = up_conv(vbuf.at[c % 2], slice(D_FF + c * FFN_CW, D_FF + (c + 1) * FFN_CW))
        actbuf[:, c * FFN_CW:(c + 1) * FFN_CW] = (jax.nn.gelu(gate) * val).astype(jnp.bfloat16)
    for blk in range(TM_FFN // FFN_DOWN_RB):
        rows = slice(blk * FFN_DOWN_RB, (blk + 1) * FFN_DOWN_RB)
        slabs = slice(blk * FFN_DOWN_RB // SUBLANES, (blk + 1) * FFN_DOWN_RB // SUBLANES)
        f = _dot(actbuf[rows, :], w_down_ref[...])
        y = xt[rows, :] + _rms(f, g_post_ref[...])
        xout[slot, slabs] = y.reshape(FFN_DOWN_RB // SUBLANES, SUBLANES, D_MODEL)
    for s in range(SUBLANES):
        out_copy(n, slot, s).start()

    @pl.when(n == n_tiles - 1)
    def _():
        @pl.when(n >= 1)
        def _():
            for s in range(SUBLANES):
                out_copy(n - 1, 1 - slot, s).wait()
        for s in range(SUBLANES):
            out_copy(n, slot, s).wait()


def _rel_distance_row(rel_table):
    h = rel_table.shape[0]
    far = rel_table[:, 2 * MAX_REL:2 * MAX_REL + 1]
    n_far = HALO_KV - MAX_REL + 1
    near = jnp.flip(rel_table, axis=1)[:, 1:]
    n_tail = ROLL_W - n_far - near.shape[1]
    return jnp.concatenate([jnp.broadcast_to(far, (h, n_far)), near,
                            jnp.broadcast_to(far, (h, n_tail))], axis=1)


def _const_spec(shape):
    return pl.BlockSpec(shape, lambda b, i: (0,) * len(shape), pipeline_mode=pl.Buffered(1))


def _mixer(x, g_pre, w_in, dw_w, dw_b, ln_g, ln_b, rel_row, w_out, g_post):
    B, T, D = x.shape
    row_spec = pl.BlockSpec((None, TM_MIX, D), lambda b, i: (b, i, 0))
    hbm_spec = pl.BlockSpec(memory_space=pl.ANY)
    consts = (g_pre, w_in, dw_w, dw_b, ln_g, ln_b, rel_row, w_out, g_post)
    const_specs = [hbm_spec if c is w_in or c is w_out else _const_spec(c.shape) for c in consts]
    return pl.pallas_call(
        _mixer_kernel,
        grid=(B, T // TM_MIX),
        in_specs=[row_spec] + const_specs,
        out_specs=row_spec,
        out_shape=jax.ShapeDtypeStruct(x.shape, x.dtype),
        scratch_shapes=[
            pltpu.VMEM(w_in.shape, jnp.bfloat16),
            pltpu.VMEM((ATTN_WIDTH, D), jnp.bfloat16),
            pltpu.VMEM((ATTN_WIDTH, D), jnp.bfloat16),
            pltpu.VMEM(w_out.shape, jnp.bfloat16),
            pltpu.VMEM((2, W_IN_CHUNK, w_in.shape[1]), jnp.float32),
            pltpu.VMEM((2, W_OUT_CHUNK, w_out.shape[1]), jnp.float32),
            pltpu.SemaphoreType.DMA((2,)),
            pltpu.VMEM((HALO_H + TM_MIX, CONV_WIDTH), jnp.float32),
            pltpu.VMEM((SUBLANES - 1, HSHIFT_ROWS, CONV_WIDTH), jnp.float32),
            pltpu.VMEM((TM_MIX, CONV_WIDTH), jnp.float32),
            pltpu.VMEM((ATTN_WIDTH, TM_MIX), jnp.bfloat16),
            pltpu.VMEM((N_PAIRS, HALO_KV + TM_MIX, PAIR_W), jnp.bfloat16),
            pltpu.VMEM((ATTN_WIDTH, HALO_KV + TM_MIX), jnp.bfloat16),
            pltpu.VMEM((N_HEADS, QW // HALF_W, KW, HALF_W), jnp.float32),
            pltpu.VMEM((N_QUADS, N_LANE_BLOCKS, KW, HALF_W), jnp.float32),
            pltpu.VMEM((N_QUADS, SUBLANES, 2 * QW), jnp.float32),
            pltpu.VMEM((N_QUADS, N_LANE_BLOCKS, KW, HALF_W), jnp.bfloat16),
            pltpu.VMEM((N_QUADS, SUBLANES, 2 * QW), jnp.float32),
            pltpu.VMEM((ATTN_WIDTH, TM_MIX), jnp.float32),
            pltpu.VMEM((TM_MIX, CONV_WIDTH + ATTN_WIDTH), jnp.bfloat16),
        ],
        compiler_params=pltpu.CompilerParams(
            dimension_semantics=("arbitrary", "arbitrary"), vmem_limit_bytes=VMEM_LIMIT),
        name="mixer",
    )(x, *consts)


def _ffn(h, g_pre, w_up, dw_w, dw_b, w_down, g_post):
    B, T, D = h.shape
    n_t = T // TM_FFN
    tiles = (B * n_t, SUBLANES, FFN_SEG, D)
    hbm_spec = pl.BlockSpec(memory_space=pl.ANY)
    out = pl.pallas_call(
        _ffn_kernel,
        grid=(B, n_t),
        in_specs=[hbm_spec,
                  _const_spec((1, D)), hbm_spec, _const_spec(dw_w.shape),
                  _const_spec((1, 2 * D_FF)), hbm_spec, _const_spec((1, D))],
        out_specs=hbm_spec,
        out_shape=jax.ShapeDtypeStruct(tiles, h.dtype),
        scratch_shapes=[
            pltpu.VMEM(w_up.shape, jnp.bfloat16),
            pltpu.VMEM(w_down.shape, jnp.bfloat16),
            pltpu.VMEM((2, W_UP_CHUNK, w_up.shape[1]), jnp.float32),
            pltpu.VMEM((2, W_DOWN_CHUNK, w_down.shape[1]), jnp.float32),
            pltpu.SemaphoreType.DMA((2,)),
            pltpu.VMEM((2, FFN_SEG, SUBLANES, D), jnp.float32),
            pltpu.VMEM((2, FFN_SEG, SUBLANES, D), jnp.float32),
            pltpu.SemaphoreType.DMA((2,)),
            pltpu.SemaphoreType.DMA((2,)),
            pltpu.VMEM((FFN_HALO, 2 * D_FF), jnp.float32),
            pltpu.VMEM((2, FFN_HALO + TM_FFN, FFN_CW), jnp.float32),
            pltpu.VMEM((2, FFN_HALO + TM_FFN, FFN_CW), jnp.float32),
            pltpu.VMEM((TM_FFN, D_FF), jnp.bfloat16),
        ],
        compiler_params=pltpu.CompilerParams(
            dimension_semantics=("arbitrary", "arbitrary"), vmem_limit_bytes=VMEM_LIMIT),
        name="ffn",
    )(h.reshape(tiles), g_pre, w_up, dw_w, dw_b, w_down, g_post)
    return out.reshape(B, T, D)


def kernel(x, norm_mix_pre, w_in, conv_dw_w, conv_dw_b, conv_ln_g, conv_ln_b, rel_bias, w_out,
           norm_mix_post, norm_ffn_pre, w_up, ffn_dw_w, ffn_dw_b, w_down, norm_ffn_post):
    h = x
    for l in range(norm_mix_pre.shape[0]):
        h = _mixer(h, norm_mix_pre[l][None], w_in[l],
                   jnp.broadcast_to(conv_dw_w[l][:, None, :], (CONV_KERNEL, SUBLANES, CONV_WIDTH)),
                   conv_dw_b[l][None], conv_ln_g[l][None], conv_ln_b[l][None],
                   _rel_distance_row(rel_bias[l]), w_out[l], norm_mix_post[l][None])
        h = _ffn(h, norm_ffn_pre[l][None], w_up[l], ffn_dw_w[l],
                 ffn_dw_b[l][None], w_down[l], norm_ffn_post[l][None])
    return h
```

```python
import jax
import jax.numpy as jnp
from jax import lax
from jax.experimental import pallas as pl
from jax.experimental.pallas import tpu as pltpu

D_MODEL = 1024
CHUNK = 64
N_LEFT_CHUNKS = 8
CONV_WIDTH = 512
ATTN_WIDTH = 512
HEAD_DIM = 64
N_HEADS = ATTN_WIDTH // HEAD_DIM
N_PAIRS = N_HEADS // 2
PAIR_W = 2 * HEAD_DIM
CONV_KERNEL = 31
MAX_REL = 128
D_FF = 2816
FFN_CONV_KERNEL = 3
EPS = 1e-6
NEG_INF = -1e30

TM_MIX = 512
HALO_H = 32
HALO_KV = N_LEFT_CHUNKS * CHUNK
CONV_RB = 32
OUT_RB = 256
PROJ_RB = 256
VT_COLS = 256
SUBLANES = 8
HSHIFT_LEAD = HALO_H - SUBLANES
HSHIFT_ROWS = HALO_H + TM_MIX - SUBLANES
QUAD = 4
QW = QUAD * CHUNK
KW = HALO_KV + QW
N_QUADS = TM_MIX // QW
ONES_ROWS = 16
PIECE = 128
SCORE_ROWS = 384
SOFTMAX_ROWS = 256
HALF_W = 128
FAR_LAG = (HALO_KV - MAX_REL) // CHUNK - 1
LOG2E = 1.4426950408889634
ROLL_W = 1024
TM_FFN = 512
FFN_CW = 256
FFN_DOWN_RB = 256
FFN_SEG = TM_FFN // SUBLANES
FFN_HALO = (FFN_CONV_KERNEL - 1) * SUBLANES
W_IN_CHUNK = 128
W_OUT_CHUNK = 256
W_UP_CHUNK = 128
W_DOWN_CHUNK = 352
VMEM_LIMIT = 56 * 1024 * 1024


def _rms(xf, g):
    return xf * lax.rsqrt(jnp.mean(xf * xf, axis=-1, keepdims=True) + EPS) * g


def _dot(a, b):
    return jnp.dot(a, b, preferred_element_type=jnp.float32)


def _dot_nt(a, b):
    return lax.dot_general(a, b, (((1,), (1,)), ((), ())), preferred_element_type=jnp.float32)


def _build_bias(rel_ref, bias_t):
    r_idx = lax.broadcasted_iota(jnp.int32, (KW, QW), 0) // CHUNK
    c_idx = lax.broadcasted_iota(jnp.int32, (KW, QW), 1) // CHUNK
    visible = (r_idx >= c_idx) & (r_idx <= c_idx + N_LEFT_CHUNKS)
    for h in range(N_HEADS):
        row = jnp.broadcast_to(rel_ref[h:h + 1, :], (QW, ROLL_W))
        nat = pltpu.roll(row, 0, 1, stride=1, stride_axis=0)[:, 0:KW]
        rel_to_far = (nat.T - rel_ref[h:h + 1, 0:1]) * LOG2E
        masked = jnp.where(visible, rel_to_far, NEG_INF)
        for half in range(QW // HALF_W):
            bias_t[h, half] = masked[:, half * HALF_W:(half + 1) * HALF_W]


def _block_kind(piece, half):
    lags = [kc - qc
            for kc in range(piece * PIECE // CHUNK, (piece + 1) * PIECE // CHUNK)
            for qc in range(half * HALF_W // CHUNK, (half + 1) * HALF_W // CHUNK)]
    live = any(0 <= lag <= N_LEFT_CHUNKS for lag in lags)
    plain = all(0 <= lag <= FAR_LAG for lag in lags)
    return live, plain


def _first_live(piece, half, lo):
    return not any(_block_kind(t, half)[0] for t in range(lo // PIECE, piece))


def _unit_blocks(piece):
    for head in range(2):
        for half in range(QW // HALF_W):
            live, plain = _block_kind(piece, half)
            if live:
                start = head * QW + half * HALF_W
                yield head, half, plain, slice(start, start + HALF_W)


N_LANE_BLOCKS = 2 * QW // HALF_W


def _score_steps(pair, quad, lo, qbuf, kbuf, bias_t, s_all, mx_all):
    s_ref, mx_ref = s_all.at[quad], mx_all.at[quad]
    k0 = quad * QW
    qt = qbuf[pl.ds(pl.multiple_of(pair * PAIR_W, PAIR_W), PAIR_W), k0:k0 + QW]
    zero = jnp.zeros((HEAD_DIM, QW), qt.dtype)
    qm = jnp.concatenate([jnp.concatenate([qt[0:HEAD_DIM], zero], axis=0),
                          jnp.concatenate([zero, qt[HEAD_DIM:PAIR_W]], axis=0)], axis=1)

    def step(r0, r1):
        s = _dot(kbuf[pair, k0 + r0:k0 + r1, :], qm)
        running = {}
        for piece in range(r0 // PIECE, r1 // PIECE):
            r = piece * PIECE
            for head, half, plain, lanes in _unit_blocks(piece):
                sh = s[r - r0:r - r0 + PIECE, lanes]
                if not plain:
                    sh = sh + bias_t[2 * pair + head, half, r:r + PIECE, :]
                s_ref[lanes.start // HALF_W, r:r + PIECE, :] = sh
                m = jnp.max(sh.reshape(PIECE // SUBLANES, SUBLANES, HALF_W), axis=0)
                key = (lanes.start, half)
                if key in running:
                    running[key] = jnp.maximum(running[key], m)
                elif _first_live(piece, half, lo):
                    running[key] = m
                else:
                    running[key] = jnp.maximum(mx_ref[:, lanes], m)
        for (start, _), m in running.items():
            mx_ref[:, start:start + HALF_W] = m

    starts = range(lo, KW, SCORE_ROWS)
    return [lambda r0=r0: step(r0, min(r0 + SCORE_ROWS, KW)) for r0 in starts]


def _softmax_steps(quad, lo, s_all, mx_all, p_all):
    s_ref, mx_ref, p_ref = s_all.at[quad], mx_all.at[quad], p_all.at[quad]
    mx = jnp.max(mx_ref[...], axis=0, keepdims=True)

    def step(r0, r1):
        for piece in range(r0 // PIECE, r1 // PIECE):
            r = piece * PIECE
            for _, half, _, lanes in _unit_blocks(piece):
                blk = lanes.start // HALF_W
                e = jnp.exp2(s_ref[blk, r:r + PIECE, :] - mx[:, lanes])
                p_ref[blk, r:r + PIECE, :] = e.astype(jnp.bfloat16)

    starts = range(lo, KW, SOFTMAX_ROWS)
    return [lambda r0=r0: step(r0, min(r0 + SOFTMAX_ROWS, KW)) for r0 in starts]


def _zero_dead_blocks(p_all):
    p_all[...] = jnp.zeros(p_all.shape, jnp.bfloat16)


def _weighted_values(pair, quad, lo, vtbuf, p_all, out_t):
    row0 = pl.multiple_of(pair * PAIR_W, PAIR_W)
    k0 = quad * QW
    p = jnp.concatenate([p_all[quad, blk, lo:KW, :] for blk in range(N_LANE_BLOCKS)], axis=1)
    ones = jnp.ones((ONES_ROWS, KW - lo), jnp.bfloat16)
    vt1 = jnp.concatenate([vtbuf[pl.ds(row0, PAIR_W), k0 + lo:k0 + KW], ones], axis=0)
    o_t = _dot(vt1, p)
    inv = 1.0 / o_t[PAIR_W:PAIR_W + 1, :]
    for head in range(2):
        rows = slice(head * HEAD_DIM, (head + 1) * HEAD_DIM)
        lanes = slice(head * QW, (head + 1) * QW)
        out_t[pl.ds(row0 + head * HEAD_DIM, HEAD_DIM), k0:k0 + QW] = o_t[rows, lanes] * inv[:, lanes]


def _interleave(a_steps, b_steps):
    done = 0
    for t, a_step in enumerate(a_steps):
        a_step()
        upto = (t + 1) * len(b_steps) // len(a_steps)
        for b_step in b_steps[done:upto]:
            b_step()
        done = upto


def _load_weight_bf16(w_hbm, dst, stage, sem, after_chunk=None):
    chunk_rows = stage.shape[1]
    n_chunks = w_hbm.shape[0] // chunk_rows

    def copy(c):
        rows = slice(c * chunk_rows, (c + 1) * chunk_rows)
        return pltpu.make_async_copy(w_hbm.at[rows, :], stage.at[c % 2], sem.at[c % 2])

    copy(0).start()
    for c in range(n_chunks):
        if c + 1 < n_chunks:
            copy(c + 1).start()
        copy(c).wait()
        rows = slice(c * chunk_rows, (c + 1) * chunk_rows)
        dst[rows, :] = stage[c % 2].astype(jnp.bfloat16)
        if after_chunk is not None:
            after_chunk(c, rows, stage.at[c % 2])


def _alternate(a_steps, b_steps):
    for t in range(max(len(a_steps), len(b_steps))):
        if t < len(a_steps):
            a_steps[t]()
        if t < len(b_steps):
            b_steps[t]()


def _mixer_kernel(x_ref, g_pre_ref, w_in_hbm, dw_w_ref, dw_b_ref,
                  ln_g_ref, ln_b_ref, rel_ref, w_out_hbm, g_post_ref, o_ref,
                  w_in_ref, w_vt_ref, w_qt_ref, w_out_ref, stage_in, stage_out, w_sem,
                  hbuf, hshift, cbuf, qbuf, kbuf, vtbuf, bias_t, s_ref, mx_ref, p_ref,
                  out_t, mixbuf):
    b = pl.program_id(0)
    i = pl.program_id(1)
    c_q = 2 * CONV_WIDTH
    c_k = c_q + ATTN_WIDTH
    c_v = c_k + ATTN_WIDTH

    @pl.when((b == 0) & (i == 0))
    def _():
        def value_weights_transposed(c, rows, staged):
            w_vt_ref[:, rows] = staged[:, c_v:c_v + ATTN_WIDTH].T.astype(jnp.bfloat16)
            w_qt_ref[:, rows] = staged[:, c_q:c_q + ATTN_WIDTH].T.astype(jnp.bfloat16)

        _load_weight_bf16(w_in_hbm, w_in_ref, stage_in, w_sem, value_weights_transposed)
        _load_weight_bf16(w_out_hbm, w_out_ref, stage_out, w_sem)
        _build_bias(rel_ref, bias_t)
        _zero_dead_blocks(p_ref)

    @pl.when(i == 0)
    def _():
        hbuf[0:HALO_H, :] = jnp.zeros((HALO_H, CONV_WIDTH), jnp.float32)
        kbuf[:, 0:HALO_KV, :] = jnp.zeros((N_PAIRS, HALO_KV, PAIR_W), jnp.bfloat16)
        vtbuf[:, 0:HALO_KV] = jnp.zeros((ATTN_WIDTH, HALO_KV), jnp.bfloat16)

    def conv_block(base):
        acc = jnp.broadcast_to(dw_b_ref[...][None], (CONV_RB // SUBLANES, SUBLANES, CONV_WIDTH))
        for j in range(CONV_KERNEL):
            off = HALO_H - (CONV_KERNEL - 1) + j
            r = off % SUBLANES
            rows = slice(base + off - r, base + off - r + CONV_RB)
            tap = hbuf[rows, :] if r == 0 else hshift[r - 1, rows, :]
            acc = acc + dw_w_ref[j][None] * tap.reshape(acc.shape)
        cbuf[base:base + CONV_RB, :] = acc.reshape(CONV_RB, CONV_WIDTH)

    pending = []

    def project_steps(blk):
        lo_row, hi_row = blk * PROJ_RB, (blk + 1) * PROJ_RB
        rows = slice(lo_row, hi_row)
        state = {}

        def norm_and_value():
            state["u"] = _rms(x_ref[rows, :], g_pre_ref[...]).astype(jnp.bfloat16)
            state["a_val"] = _dot(state["u"], w_in_ref[:, 0:CONV_WIDTH])

        def gate():
            a_gate = _dot(state["u"], w_in_ref[:, CONV_WIDTH:c_q])
            hbuf[HALO_H + lo_row:HALO_H + hi_row, :] = state["a_val"] * jax.nn.sigmoid(a_gate)

        def queries():
            q_t = _dot_nt(w_qt_ref[...], state["u"]) * (HEAD_DIM ** -0.5 * LOG2E)
            qbuf[:, rows] = q_t.astype(jnp.bfloat16)

        def keys_values():
            k = _dot(state["u"], w_in_ref[:, c_k:c_k + ATTN_WIDTH]).astype(jnp.bfloat16)
            for p in range(N_PAIRS):
                kbuf[p, HALO_KV + lo_row:HALO_KV + hi_row, :] = k[:, p * PAIR_W:(p + 1) * PAIR_W]
            pending.append(state["u"])
            if hi_row % VT_COLS == 0:
                u_wide = pending[0] if len(pending) == 1 else jnp.concatenate(pending, axis=0)
                vtbuf[:, HALO_KV + hi_row - VT_COLS:HALO_KV + hi_row] = (
                    _dot_nt(w_vt_ref[...], u_wide).astype(jnp.bfloat16))
                pending.clear()

        return [norm_and_value, gate, queries, keys_values]

    def mix_steps(blk):
        lo_row, hi_row = blk * PROJ_RB, (blk + 1) * PROJ_RB
        rows = slice(lo_row, hi_row)

        def shifted_copies():
            sh_lo = 0 if blk == 0 else lo_row + HSHIFT_LEAD
            sh_hi = hi_row + HSHIFT_LEAD
            for r in range(1, SUBLANES):
                hshift[r - 1, sh_lo:sh_hi, :] = hbuf[sh_lo + r:sh_hi + r, :]

        def norm_swish():
            c = cbuf[rows, :]
            mu = jnp.mean(c, axis=-1, keepdims=True)
            xc = c - mu
            var = jnp.mean(xc * xc, axis=-1, keepdims=True)
            y = xc * lax.rsqrt(var + EPS) * ln_g_ref[...] + ln_b_ref[...]
            y = y * jax.nn.sigmoid(y)
            mixbuf[rows, 0:CONV_WIDTH] = y.astype(jnp.bfloat16)

        convs = [lambda base=lo_row + rb * CONV_RB: conv_block(base)
                 for rb in range(PROJ_RB // CONV_RB)]
        return [shifted_copies] + convs + [norm_swish]

    n_blocks = TM_MIX // PROJ_RB
    for step in project_steps(0):
        step()
    for blk in range(n_blocks):
        nxt = project_steps(blk + 1) if blk + 1 < n_blocks else []
        _alternate(nxt, mix_steps(blk))

    def attn_loop(first_tile):
        lo = [max(0, HALO_KV - quad * QW) if first_tile else 0 for quad in range(N_QUADS)]

        def scores(pair, quad):
            return _score_steps(pair, quad, lo[quad], qbuf, kbuf, bias_t, s_ref, mx_ref)

        for step in scores(0, 0):
            step()

        def body(pair, carry):
            for quad in range(N_QUADS):
                if quad + 1 < N_QUADS:
                    nxt = scores(pair, quad + 1)
                else:
                    nxt = scores(jnp.minimum(pair + 1, N_PAIRS - 1), 0)
                _interleave(nxt, _softmax_steps(quad, lo[quad], s_ref, mx_ref, p_ref))
                _weighted_values(pair, quad, lo[quad], vtbuf, p_ref, out_t)
            return carry
        lax.fori_loop(0, N_PAIRS, body, 0)

    @pl.when(i == 0)
    def _():
        attn_loop(True)

    @pl.when(i > 0)
    def _():
        attn_loop(False)

    for blk in range(TM_MIX // OUT_RB):
        rows = slice(blk * OUT_RB, (blk + 1) * OUT_RB)
        attn = out_t[:, rows].T.astype(jnp.bfloat16)
        mixed = _dot(jnp.concatenate([mixbuf[rows, 0:CONV_WIDTH], attn], axis=1), w_out_ref[...])
        o_ref[rows, :] = x_ref[rows, :] + _rms(mixed, g_post_ref[...])

    hbuf[0:HALO_H, :] = hbuf[TM_MIX:TM_MIX + HALO_H, :]
    kbuf[:, 0:HALO_KV, :] = kbuf[:, TM_MIX:TM_MIX + HALO_KV, :]
    vtbuf[:, 0:HALO_KV] = vtbuf[:, TM_MIX:TM_MIX + HALO_KV]


def _ffn_kernel(h_hbm, g_pre_ref, w_up_hbm, dw_w_ref, dw_b_ref, w_down_hbm, g_post_ref, o_hbm,
                w_up_ref, w_down_ref, stage_up, stage_down, w_sem,
                xin, xout, sem_in, sem_out, carry, gbuf, vbuf, actbuf):
    i = pl.program_id(1)
    n = pl.program_id(0) * pl.num_programs(1) + i
    n_tiles = pl.num_programs(0) * pl.num_programs(1)
    slot = n % 2

    def in_copy(tile, sl, s):
        return pltpu.make_async_copy(h_hbm.at[tile, s], xin.at[sl, :, s, :], sem_in.at[sl])

    def out_copy(tile, sl, s):
        return pltpu.make_async_copy(xout.at[sl, :, s, :], o_hbm.at[tile, s], sem_out.at[sl])

    @pl.when(n == 0)
    def _():
        for s in range(SUBLANES):
            in_copy(0, 0, s).start()
        _load_weight_bf16(w_up_hbm, w_up_ref, stage_up, w_sem)
        _load_weight_bf16(w_down_hbm, w_down_ref, stage_down, w_sem)

    @pl.when(n + 1 < n_tiles)
    def _():
        for s in range(SUBLANES):
            in_copy(n + 1, 1 - slot, s).start()

    @pl.when(i == 0)
    def _():
        carry[...] = jnp.zeros(carry.shape, jnp.float32)

    @pl.when(n >= 2)
    def _():
        for s in range(SUBLANES):
            out_copy(n - 2, slot, s).wait()

    for s in range(SUBLANES):
        in_copy(n, slot, s).wait()

    xt = xin[slot].reshape(TM_FFN, D_MODEL)
    u = _rms(xt, g_pre_ref[...]).astype(jnp.bfloat16)
    first_sublane = lax.broadcasted_iota(jnp.int32, (SUBLANES, FFN_CW), 0) == 0

    def up_conv(buf, cols):
        hc = _dot(u, w_up_ref[:, cols])
        for k in (1, 2):
            last = hc[TM_FFN - k * SUBLANES:TM_FFN - (k - 1) * SUBLANES, :]
            prev = carry[(2 - k) * SUBLANES:(3 - k) * SUBLANES, cols]
            buf[(2 - k) * SUBLANES:(3 - k) * SUBLANES, :] = jnp.where(
                first_sublane, pltpu.roll(prev, 1, 0), pltpu.roll(last, 1, 0))
            carry[(2 - k) * SUBLANES:(3 - k) * SUBLANES, cols] = last
        buf[FFN_HALO:FFN_HALO + TM_FFN, :] = hc
        y = dw_b_ref[:, cols] + dw_w_ref[2:3, cols] * hc
        y = y + dw_w_ref[1:2, cols] * buf[SUBLANES:SUBLANES + TM_FFN, :]
        y = y + dw_w_ref[0:1, cols] * buf[0:TM_FFN, :]
        return y

    for c in range(D_FF // FFN_CW):
        gate = up_conv(gbuf.at[c % 2], slice(c * FFN_CW, (c + 1) * FFN_CW))
        val = up_conv(vbuf.at[c % 2], slice(D_FF + c * FFN_CW, D_FF + (c + 1) * FFN_CW))
        actbuf[:, c * FFN_CW:(c + 1) * FFN_CW] = (jax.nn.gelu(gate) * val).astype(jnp.bfloat16)
    for blk in range(TM_FFN // FFN_DOWN_RB):
        rows = slice(blk * FFN_DOWN_RB, (blk + 1) * FFN_DOWN_RB)
        slabs = slice(blk * FFN_DOWN_RB // SUBLANES, (blk + 1) * FFN_DOWN_RB // SUBLANES)
        f = _dot(actbuf[rows, :], w_down_ref[...])
        y = xt[rows, :] + _rms(f, g_post_ref[...])
        xout[slot, slabs] = y.reshape(FFN_DOWN_RB // SUBLANES, SUBLANES, D_MODEL)
    for s in range(SUBLANES):
        out_copy(n, slot, s).start()

    @pl.when(n == n_tiles - 1)
    def _():
        @pl.when(n >= 1)
        def _():
            for s in range(SUBLANES):
                out_copy(n - 1, 1 - slot, s).wait()
        for s in range(SUBLANES):
            out_copy(n, slot, s).wait()


def _rel_distance_row(rel_table):
    h = rel_table.shape[0]
    far = rel_table[:, 2 * MAX_REL:2 * MAX_REL + 1]
    n_far = HALO_KV - MAX_REL + 1
    near = jnp.flip(rel_table, axis=1)[:, 1:]
    n_tail = ROLL_W - n_far - near.shape[1]
    return jnp.concatenate([jnp.broadcast_to(far, (h, n_far)), near,
                            jnp.broadcast_to(far, (h, n_tail))], axis=1)


def _const_spec(shape):
    return pl.BlockSpec(shape, lambda b, i: (0,) * len(shape), pipeline_mode=pl.Buffered(1))


def _mixer(x, g_pre, w_in, dw_w, dw_b, ln_g, ln_b, rel_row, w_out, g_post):
    B, T, D = x.shape
    row_spec = pl.BlockSpec((None, TM_MIX, D), lambda b, i: (b, i, 0))
    hbm_spec = pl.BlockSpec(memory_space=pl.ANY)
    consts = (g_pre, w_in, dw_w, dw_b, ln_g, ln_b, rel_row, w_out, g_post)
    const_specs = [hbm_spec if c is w_in or c is w_out else _const_spec(c.shape) for c in consts]
    return pl.pallas_call(
        _mixer_kernel,
        grid=(B, T // TM_MIX),
        in_specs=[row_spec] + const_specs,
        out_specs=row_spec,
        out_shape=jax.ShapeDtypeStruct(x.shape, x.dtype),
        scratch_shapes=[
            pltpu.VMEM(w_in.shape, jnp.bfloat16),
            pltpu.VMEM((ATTN_WIDTH, D), jnp.bfloat16),
            pltpu.VMEM((ATTN_WIDTH, D), jnp.bfloat16),
            pltpu.VMEM(w_out.shape, jnp.bfloat16),
            pltpu.VMEM((2, W_IN_CHUNK, w_in.shape[1]), jnp.float32),
            pltpu.VMEM((2, W_OUT_CHUNK, w_out.shape[1]), jnp.float32),
            pltpu.SemaphoreType.DMA((2,)),
            pltpu.VMEM((HALO_H + TM_MIX, CONV_WIDTH), jnp.float32),
            pltpu.VMEM((SUBLANES - 1, HSHIFT_ROWS, CONV_WIDTH), jnp.float32),
            pltpu.VMEM((TM_MIX, CONV_WIDTH), jnp.float32),
            pltpu.VMEM((ATTN_WIDTH, TM_MIX), jnp.bfloat16),
            pltpu.VMEM((N_PAIRS, HALO_KV + TM_MIX, PAIR_W), jnp.bfloat16),
            pltpu.VMEM((ATTN_WIDTH, HALO_KV + TM_MIX), jnp.bfloat16),
            pltpu.VMEM((N_HEADS, QW // HALF_W, KW, HALF_W), jnp.float32),
            pltpu.VMEM((N_QUADS, N_LANE_BLOCKS, KW, HALF_W), jnp.float32),
            pltpu.VMEM((N_QUADS, SUBLANES, 2 * QW), jnp.float32),
            pltpu.VMEM((N_QUADS, N_LANE_BLOCKS, KW, HALF_W), jnp.bfloat16),
            pltpu.VMEM((ATTN_WIDTH, TM_MIX), jnp.float32),
            pltpu.VMEM((TM_MIX, CONV_WIDTH + ATTN_WIDTH), jnp.bfloat16),
        ],
        compiler_params=pltpu.CompilerParams(
            dimension_semantics=("arbitrary", "arbitrary"), vmem_limit_bytes=VMEM_LIMIT),
        name="mixer",
    )(x, *consts)


def _ffn(h, g_pre, w_up, dw_w, dw_b, w_down, g_post):
    B, T, D = h.shape
    n_t = T // TM_FFN
    tiles = (B * n_t, SUBLANES, FFN_SEG, D)
    hbm_spec = pl.BlockSpec(memory_space=pl.ANY)
    out = pl.pallas_call(
        _ffn_kernel,
        grid=(B, n_t),
        in_specs=[hbm_spec,
                  _const_spec((1, D)), hbm_spec, _const_spec(dw_w.shape),
                  _const_spec((1, 2 * D_FF)), hbm_spec, _const_spec((1, D))],
        out_specs=hbm_spec,
        out_shape=jax.ShapeDtypeStruct(tiles, h.dtype),
        scratch_shapes=[
            pltpu.VMEM(w_up.shape, jnp.bfloat16),
            pltpu.VMEM(w_down.shape, jnp.bfloat16),
            pltpu.VMEM((2, W_UP_CHUNK, w_up.shape[1]), jnp.float32),
            pltpu.VMEM((2, W_DOWN_CHUNK, w_down.shape[1]), jnp.float32),
            pltpu.SemaphoreType.DMA((2,)),
            pltpu.VMEM((2, FFN_SEG, SUBLANES, D), jnp.float32),
            pltpu.VMEM((2, FFN_SEG, SUBLANES, D), jnp.float32),
            pltpu.SemaphoreType.DMA((2,)),
            pltpu.SemaphoreType.DMA((2,)),
            pltpu.VMEM((FFN_HALO, 2 * D_FF), jnp.float32),
            pltpu.VMEM((2, FFN_HALO + TM_FFN, FFN_CW), jnp.float32),
            pltpu.VMEM((2, FFN_HALO + TM_FFN, FFN_CW), jnp.float32),
            pltpu.VMEM((TM_FFN, D_FF), jnp.bfloat16),
        ],
        compiler_params=pltpu.CompilerParams(
            dimension_semantics=("arbitrary", "arbitrary"), vmem_limit_bytes=VMEM_LIMIT),
        name="ffn",
    )(h.reshape(tiles), g_pre, w_up, dw_w, dw_b, w_down, g_post)
    return out.reshape(B, T, D)


def kernel(x, norm_mix_pre, w_in, conv_dw_w, conv_dw_b, conv_ln_g, conv_ln_b, rel_bias, w_out,
           norm_mix_post, norm_ffn_pre, w_up, ffn_dw_w, ffn_dw_b, w_down, norm_ffn_post):
    h = x
    for l in range(norm_mix_pre.shape[0]):
        h = _mixer(h, norm_mix_pre[l][None], w_in[l],
                   jnp.broadcast_to(conv_dw_w[l][:, None, :], (CONV_KERNEL, SUBLANES, CONV_WIDTH)),
                   conv_dw_b[l][None], conv_ln_g[l][None], conv_ln_b[l][None],
                   _rel_distance_row(rel_bias[l]), w_out[l], norm_mix_post[l][None])
        h = _ffn(h, norm_ffn_pre[l][None], w_up[l], ffn_dw_w[l],
                 ffn_dw_b[l][None], w_down[l], norm_ffn_post[l][None])
    return h
```

```python
import jax
import jax.numpy as jnp
from jax import lax
from jax.experimental import pallas as pl
from jax.experimental.pallas import tpu as pltpu

D_MODEL = 1024
CHUNK = 64
N_LEFT_CHUNKS = 8
CONV_WIDTH = 512
ATTN_WIDTH = 512
HEAD_DIM = 64
N_HEADS = ATTN_WIDTH // HEAD_DIM
N_PAIRS = N_HEADS // 2
PAIR_W = 2 * HEAD_DIM
CONV_KERNEL = 31
MAX_REL = 128
D_FF = 2816
FFN_CONV_KERNEL = 3
EPS = 1e-6
NEG_INF = -1e30

TM_MIX = 512
HALO_H = 32
HALO_KV = N_LEFT_CHUNKS * CHUNK
CONV_RB = 32
OUT_RB = 256
PROJ_RB = 256
VT_COLS = 256
SUBLANES = 8
HSHIFT_LEAD = HALO_H - SUBLANES
HSHIFT_ROWS = HALO_H + TM_MIX - SUBLANES
QUAD = 4
QW = QUAD * CHUNK
KW = HALO_KV + QW
N_QUADS = TM_MIX // QW
ONES_ROWS = 16
PIECE = 128
SCORE_ROWS = 384
SOFTMAX_ROWS = 256
HALF_W = 128
FAR_LAG = (HALO_KV - MAX_REL) // CHUNK - 1
LOG2E = 1.4426950408889634
ROLL_W = 1024
TM_FFN = 512
FFN_CW = 256
FFN_DOWN_RB = 256
FFN_SEG = TM_FFN // SUBLANES
FFN_HALO = (FFN_CONV_KERNEL - 1) * SUBLANES
W_IN_CHUNK = 128
W_OUT_CHUNK = 256
W_UP_CHUNK = 128
W_DOWN_CHUNK = 352
VMEM_LIMIT = 56 * 1024 * 1024


def _rms(xf, g):
    return xf * lax.rsqrt(jnp.mean(xf * xf, axis=-1, keepdims=True) + EPS) * g


def _dot(a, b):
    return jnp.dot(a, b, preferred_element_type=jnp.float32)


def _dot_nt(a, b):
    return lax.dot_general(a, b, (((1,), (1,)), ((), ())), preferred_element_type=jnp.float32)


def _build_bias(rel_ref, bias_t):
    r_idx = lax.broadcasted_iota(jnp.int32, (KW, QW), 0) // CHUNK
    c_idx = lax.broadcasted_iota(jnp.int32, (KW, QW), 1) // CHUNK
    visible = (r_idx >= c_idx) & (r_idx <= c_idx + N_LEFT_CHUNKS)
    for h in range(N_HEADS):
        row = jnp.broadcast_to(rel_ref[h:h + 1, :], (QW, ROLL_W))
        nat = pltpu.roll(row, 0, 1, stride=1, stride_axis=0)[:, 0:KW]
        rel_to_far = (nat.T - rel_ref[h:h + 1, 0:1]) * LOG2E
        masked = jnp.where(visible, rel_to_far, NEG_INF)
        for half in range(QW // HALF_W):
            bias_t[h, half] = masked[:, half * HALF_W:(half + 1) * HALF_W]


def _block_kind(piece, half):
    lags = [kc - qc
            for kc in range(piece * PIECE // CHUNK, (piece + 1) * PIECE // CHUNK)
            for qc in range(half * HALF_W // CHUNK, (half + 1) * HALF_W // CHUNK)]
    live = any(0 <= lag <= N_LEFT_CHUNKS for lag in lags)
    plain = all(0 <= lag <= FAR_LAG for lag in lags)
    return live, plain


def _first_live(piece, half, lo):
    return not any(_block_kind(t, half)[0] for t in range(lo // PIECE, piece))


def _unit_blocks(piece):
    for head in range(2):
        for half in range(QW // HALF_W):
            live, plain = _block_kind(piece, half)
            if live:
                start = head * QW + half * HALF_W
                yield head, half, plain, slice(start, start + HALF_W)


N_LANE_BLOCKS = 2 * QW // HALF_W


def _score_steps(pair, quad, lo, qbuf, kbuf, bias_t, s_all, mx_all):
    s_ref, mx_ref = s_all.at[quad], mx_all.at[quad]
    k0 = quad * QW
    qt = qbuf[pl.ds(pl.multiple_of(pair * PAIR_W, PAIR_W), PAIR_W), k0:k0 + QW]
    zero = jnp.zeros((HEAD_DIM, QW), qt.dtype)
    qm = jnp.concatenate([jnp.concatenate([qt[0:HEAD_DIM], zero], axis=0),
                          jnp.concatenate([zero, qt[HEAD_DIM:PAIR_W]], axis=0)], axis=1)

    def step(r0, r1):
        s = _dot(kbuf[pair, k0 + r0:k0 + r1, :], qm)
        running = {}
        for piece in range(r0 // PIECE, r1 // PIECE):
            r = piece * PIECE
            for head, half, plain, lanes in _unit_blocks(piece):
                sh = s[r - r0:r - r0 + PIECE, lanes]
                if not plain:
                    sh = sh + bias_t[2 * pair + head, half, r:r + PIECE, :]
                s_ref[lanes.start // HALF_W, r:r + PIECE, :] = sh
                m = jnp.max(sh.reshape(PIECE // SUBLANES, SUBLANES, HALF_W), axis=0)
                key = (lanes.start, half)
                if key in running:
                    running[key] = jnp.maximum(running[key], m)
                elif _first_live(piece, half, lo):
                    running[key] = m
                else:
                    running[key] = jnp.maximum(mx_ref[:, lanes], m)
        for (start, _), m in running.items():
            mx_ref[:, start:start + HALF_W] = m

    starts = range(lo, KW, SCORE_ROWS)
    return [lambda r0=r0: step(r0, min(r0 + SCORE_ROWS, KW)) for r0 in starts]


def _softmax_steps(quad, lo, s_all, mx_all, p_all):
    s_ref, mx_ref, p_ref = s_all.at[quad], mx_all.at[quad], p_all.at[quad]
    mx = jnp.max(mx_ref[...], axis=0, keepdims=True)

    def step(r0, r1):
        for piece in range(r0 // PIECE, r1 // PIECE):
            r = piece * PIECE
            for _, half, _, lanes in _unit_blocks(piece):
                blk = lanes.start // HALF_W
                e = jnp.exp2(s_ref[blk, r:r + PIECE, :] - mx[:, lanes])
                p_ref[blk, r:r + PIECE, :] = e.astype(jnp.bfloat16)

    starts = range(lo, KW, SOFTMAX_ROWS)
    return [lambda r0=r0: step(r0, min(r0 + SOFTMAX_ROWS, KW)) for r0 in starts]


def _zero_dead_blocks(p_all):
    p_all[...] = jnp.zeros(p_all.shape, jnp.bfloat16)


def _weighted_values(pair, quad, lo, vtbuf, p_all, out_t):
    row0 = pl.multiple_of(pair * PAIR_W, PAIR_W)
    k0 = quad * QW
    p = jnp.concatenate([p_all[quad, blk, lo:KW, :] for blk in range(N_LANE_BLOCKS)], axis=1)
    ones = jnp.ones((ONES_ROWS, KW - lo), jnp.bfloat16)
    vt1 = jnp.concatenate([vtbuf[pl.ds(row0, PAIR_W), k0 + lo:k0 + KW], ones], axis=0)
    o_t = _dot(vt1, p)
    inv = 1.0 / o_t[PAIR_W:PAIR_W + 1, :]
    for head in range(2):
        rows = slice(head * HEAD_DIM, (head + 1) * HEAD_DIM)
        lanes = slice(head * QW, (head + 1) * QW)
        out_t[pl.ds(row0 + head * HEAD_DIM, HEAD_DIM), k0:k0 + QW] = o_t[rows, lanes] * inv[:, lanes]


def _interleave(a_steps, b_steps):
    done = 0
    for t, a_step in enumerate(a_steps):
        a_step()
        upto = (t + 1) * len(b_steps) // len(a_steps)
        for b_step in b_steps[done:upto]:
            b_step()
        done = upto


def _load_weight_bf16(w_hbm, dst, stage, sem, after_chunk=None):
    chunk_rows = stage.shape[1]
    n_chunks = w_hbm.shape[0] // chunk_rows

    def copy(c):
        rows = slice(c * chunk_rows, (c + 1) * chunk_rows)
        return pltpu.make_async_copy(w_hbm.at[rows, :], stage.at[c % 2], sem.at[c % 2])

    copy(0).start()
    for c in range(n_chunks):
        if c + 1 < n_chunks:
            copy(c + 1).start()
        copy(c).wait()
        rows = slice(c * chunk_rows, (c + 1) * chunk_rows)
        dst[rows, :] = stage[c % 2].astype(jnp.bfloat16)
        if after_chunk is not None:
            after_chunk(c, rows, stage.at[c % 2])


def _alternate(a_steps, b_steps):
    for t in range(max(len(a_steps), len(b_steps))):
        if t < len(a_steps):
            a_steps[t]()
        if t < len(b_steps):
            b_steps[t]()


def _mixer_kernel(x_ref, g_pre_ref, w_in_hbm, dw_w_ref, dw_b_ref,
                  ln_g_ref, ln_b_ref, rel_ref, w_out_hbm, g_post_ref, o_ref,
                  w_in_ref, w_vt_ref, w_qt_ref, w_out_ref, stage_in, stage_out, w_sem,
                  hbuf, hshift, cbuf, qbuf, kbuf, vtbuf, bias_t, s_ref, mx_ref, p_ref,
                  out_t, mixbuf):
    b = pl.program_id(0)
    i = pl.program_id(1)
    c_q = 2 * CONV_WIDTH
    c_k = c_q + ATTN_WIDTH
    c_v = c_k + ATTN_WIDTH

    @pl.when((b == 0) & (i == 0))
    def _():
        def value_weights_transposed(c, rows, staged):
            w_vt_ref[:, rows] = staged[:, c_v:c_v + ATTN_WIDTH].T.astype(jnp.bfloat16)
            w_qt_ref[:, rows] = staged[:, c_q:c_q + ATTN_WIDTH].T.astype(jnp.bfloat16)

        _load_weight_bf16(w_in_hbm, w_in_ref, stage_in, w_sem, value_weights_transposed)
        _load_weight_bf16(w_out_hbm, w_out_ref, stage_out, w_sem)
        _build_bias(rel_ref, bias_t)
        _zero_dead_blocks(p_ref)

    @pl.when(i == 0)
    def _():
        hbuf[0:HALO_H, :] = jnp.zeros((HALO_H, CONV_WIDTH), jnp.float32)
        kbuf[:, 0:HALO_KV, :] = jnp.zeros((N_PAIRS, HALO_KV, PAIR_W), jnp.bfloat16)
        vtbuf[:, 0:HALO_KV] = jnp.zeros((ATTN_WIDTH, HALO_KV), jnp.bfloat16)

    def conv_block(base):
        acc = jnp.broadcast_to(dw_b_ref[...][None], (CONV_RB // SUBLANES, SUBLANES, CONV_WIDTH))
        for j in range(CONV_KERNEL):
            off = HALO_H - (CONV_KERNEL - 1) + j
            r = off % SUBLANES
            rows = slice(base + off - r, base + off - r + CONV_RB)
            tap = hbuf[rows, :] if r == 0 else hshift[r - 1, rows, :]
            acc = acc + dw_w_ref[j][None] * tap.reshape(acc.shape)
        cbuf[base:base + CONV_RB, :] = acc.reshape(CONV_RB, CONV_WIDTH)

    pending = []

    def project_steps(blk):
        lo_row, hi_row = blk * PROJ_RB, (blk + 1) * PROJ_RB
        rows = slice(lo_row, hi_row)
        state = {}

        def norm_and_value():
            state["u"] = _rms(x_ref[rows, :], g_pre_ref[0:1, :]).astype(jnp.bfloat16)
            state["a_val"] = _dot(state["u"], w_in_ref[:, 0:CONV_WIDTH])

        def gate():
            a_gate = _dot(state["u"], w_in_ref[:, CONV_WIDTH:c_q])
            hbuf[HALO_H + lo_row:HALO_H + hi_row, :] = state["a_val"] * jax.nn.sigmoid(a_gate)

        def queries():
            q_t = _dot_nt(w_qt_ref[...], state["u"]) * (HEAD_DIM ** -0.5 * LOG2E)
            qbuf[:, rows] = q_t.astype(jnp.bfloat16)

        def keys_values():
            k = _dot(state["u"], w_in_ref[:, c_k:c_k + ATTN_WIDTH]).astype(jnp.bfloat16)
            for p in range(N_PAIRS):
                kbuf[p, HALO_KV + lo_row:HALO_KV + hi_row, :] = k[:, p * PAIR_W:(p + 1) * PAIR_W]
            pending.append(state["u"])
            if hi_row % VT_COLS == 0:
                u_wide = pending[0] if len(pending) == 1 else jnp.concatenate(pending, axis=0)
                vtbuf[:, HALO_KV + hi_row - VT_COLS:HALO_KV + hi_row] = (
                    _dot_nt(w_vt_ref[...], u_wide).astype(jnp.bfloat16))
                pending.clear()

        return [norm_and_value, gate, queries, keys_values]

    def mix_steps(blk):
        lo_row, hi_row = blk * PROJ_RB, (blk + 1) * PROJ_RB
        rows = slice(lo_row, hi_row)

        def shifted_copies():
            sh_lo = 0 if blk == 0 else lo_row + HSHIFT_LEAD
            sh_hi = hi_row + HSHIFT_LEAD
            for r in range(1, SUBLANES):
                hshift[r - 1, sh_lo:sh_hi, :] = hbuf[sh_lo + r:sh_hi + r, :]

        def norm_swish():
            c = cbuf[rows, :]
            mu = jnp.mean(c, axis=-1, keepdims=True)
            xc = c - mu
            var = jnp.mean(xc * xc, axis=-1, keepdims=True)
            y = xc * lax.rsqrt(var + EPS) * ln_g_ref[0:1, :] + ln_b_ref[0:1, :]
            y = y * jax.nn.sigmoid(y)
            mixbuf[rows, 0:CONV_WIDTH] = y.astype(jnp.bfloat16)

        convs = [lambda base=lo_row + rb * CONV_RB: conv_block(base)
                 for rb in range(PROJ_RB // CONV_RB)]
        return [shifted_copies] + convs + [norm_swish]

    n_blocks = TM_MIX // PROJ_RB
    for step in project_steps(0):
        step()
    for blk in range(n_blocks):
        nxt = project_steps(blk + 1) if blk + 1 < n_blocks else []
        _alternate(nxt, mix_steps(blk))

    def attn_loop(first_tile):
        lo = [max(0, HALO_KV - quad * QW) if first_tile else 0 for quad in range(N_QUADS)]

        def scores(pair, quad):
            return _score_steps(pair, quad, lo[quad], qbuf, kbuf, bias_t, s_ref, mx_ref)

        for step in scores(0, 0):
            step()

        def body(pair, carry):
            for quad in range(N_QUADS):
                if quad + 1 < N_QUADS:
                    nxt = scores(pair, quad + 1)
                else:
                    nxt = scores(jnp.minimum(pair + 1, N_PAIRS - 1), 0)
                _interleave(nxt, _softmax_steps(quad, lo[quad], s_ref, mx_ref, p_ref))
                _weighted_values(pair, quad, lo[quad], vtbuf, p_ref, out_t)
            return carry
        lax.fori_loop(0, N_PAIRS, body, 0)

    @pl.when(i == 0)
    def _():
        attn_loop(True)

    @pl.when(i > 0)
    def _():
        attn_loop(False)

    for blk in range(TM_MIX // OUT_RB):
        rows = slice(blk * OUT_RB, (blk + 1) * OUT_RB)
        attn = out_t[:, rows].T.astype(jnp.bfloat16)
        mixed = _dot(jnp.concatenate([mixbuf[rows, 0:CONV_WIDTH], attn], axis=1), w_out_ref[...])
        o_ref[rows, :] = x_ref[rows, :] + _rms(mixed, g_post_ref[0:1, :])

    hbuf[0:HALO_H, :] = hbuf[TM_MIX:TM_MIX + HALO_H, :]
    kbuf[:, 0:HALO_KV, :] = kbuf[:, TM_MIX:TM_MIX + HALO_KV, :]
    vtbuf[:, 0:HALO_KV] = vtbuf[:, TM_MIX:TM_MIX + HALO_KV]


def _ffn_kernel(h_hbm, g_pre_ref, w_up_hbm, dw_w_ref, dw_b_ref, w_down_hbm, g_post_ref, o_hbm,
                w_up_ref, w_down_ref, stage_up, stage_down, w_sem,
                xin, xout, sem_in, sem_out, carry, gbuf, vbuf, actbuf):
    i = pl.program_id(1)
    n = pl.program_id(0) * pl.num_programs(1) + i
    n_tiles = pl.num_programs(0) * pl.num_programs(1)
    slot = n % 2

    def in_copy(tile, sl, s):
        return pltpu.make_async_copy(h_hbm.at[tile, s], xin.at[sl, :, s, :], sem_in.at[sl])

    def out_copy(tile, sl, s):
        return pltpu.make_async_copy(xout.at[sl, :, s, :], o_hbm.at[tile, s], sem_out.at[sl])

    @pl.when(n == 0)
    def _():
        for s in range(SUBLANES):
            in_copy(0, 0, s).start()
        _load_weight_bf16(w_up_hbm, w_up_ref, stage_up, w_sem)
        _load_weight_bf16(w_down_hbm, w_down_ref, stage_down, w_sem)

    @pl.when(n + 1 < n_tiles)
    def _():
        for s in range(SUBLANES):
            in_copy(n + 1, 1 - slot, s).start()

    @pl.when(i == 0)
    def _():
        carry[...] = jnp.zeros(carry.shape, jnp.float32)

    @pl.when(n >= 2)
    def _():
        for s in range(SUBLANES):
            out_copy(n - 2, slot, s).wait()

    for s in range(SUBLANES):
        in_copy(n, slot, s).wait()

    xt = xin[slot].reshape(TM_FFN, D_MODEL)
    u = _rms(xt, g_pre_ref[0:1, :]).astype(jnp.bfloat16)
    first_sublane = lax.broadcasted_iota(jnp.int32, (SUBLANES, FFN_CW), 0) == 0

    def up_conv(buf, cols):
        hc = _dot(u, w_up_ref[:, cols])
        for k in (1, 2):
            last = hc[TM_FFN - k * SUBLANES:TM_FFN - (k - 1) * SUBLANES, :]
            prev = carry[(2 - k) * SUBLANES:(3 - k) * SUBLANES, cols]
            buf[(2 - k) * SUBLANES:(3 - k) * SUBLANES, :] = jnp.where(
                first_sublane, pltpu.roll(prev, 1, 0), pltpu.roll(last, 1, 0))
            carry[(2 - k) * SUBLANES:(3 - k) * SUBLANES, cols] = last
        buf[FFN_HALO:FFN_HALO + TM_FFN, :] = hc
        y = dw_b_ref[0:1, cols] + dw_w_ref[2:3, cols] * hc
        y = y + dw_w_ref[1:2, cols] * buf[SUBLANES:SUBLANES + TM_FFN, :]
        y = y + dw_w_ref[0:1, cols] * buf[0:TM_FFN, :]
        return y

    for c in range(D_FF // FFN_CW):
        gate = up_conv(gbuf.at[c % 2], slice(c * FFN_CW, (c + 1) * FFN_CW))
        val = up_conv(vbuf.at[c % 2], slice(D_FF + c * FFN_CW, D_FF + (c + 1) * FFN_CW))
        actbuf[:, c * FFN_CW:(c + 1) * FFN_CW] = (jax.nn.gelu(gate) * val).astype(jnp.bfloat16)
    for blk in range(TM_FFN // FFN_DOWN_RB):
        rows = slice(blk * FFN_DOWN_RB, (blk + 1) * FFN_DOWN_RB)
        slabs = slice(blk * FFN_DOWN_RB // SUBLANES, (blk + 1) * FFN_DOWN_RB // SUBLANES)
        f = _dot(actbuf[rows, :], w_down_ref[...])
        y = xt[rows, :] + _rms(f, g_post_ref[0:1, :])
        xout[slot, slabs] = y.reshape(FFN_DOWN_RB // SUBLANES, SUBLANES, D_MODEL)
    for s in range(SUBLANES):
        out_copy(n, slot, s).start()

    @pl.when(n == n_tiles - 1)
    def _():
        @pl.when(n >= 1)
        def _():
            for s in range(SUBLANES):
                out_copy(n - 1, 1 - slot, s).wait()
        for s in range(SUBLANES):
            out_copy(n, slot, s).wait()


def _rel_distance_row(rel_table):
    h = rel_table.shape[0]
    far = rel_table[:, 2 * MAX_REL:2 * MAX_REL + 1]
    n_far = HALO_KV - MAX_REL + 1
    near = jnp.flip(rel_table, axis=1)[:, 1:]
    n_tail = ROLL_W - n_far - near.shape[1]
    return jnp.concatenate([jnp.broadcast_to(far, (h, n_far)), near,
                            jnp.broadcast_to(far, (h, n_tail))], axis=1)


def _const_spec(shape):
    return pl.BlockSpec(shape, lambda b, i: (0,) * len(shape), pipeline_mode=pl.Buffered(1))


def _mixer(x, g_pre, w_in, dw_w, dw_b, ln_g, ln_b, rel_row, w_out, g_post):
    B, T, D = x.shape
    row_spec = pl.BlockSpec((None, TM_MIX, D), lambda b, i: (b, i, 0))
    hbm_spec = pl.BlockSpec(memory_space=pl.ANY)
    consts = (g_pre, w_in, dw_w, dw_b, ln_g, ln_b, rel_row, w_out, g_post)
    const_specs = [hbm_spec if c is w_in or c is w_out else _const_spec(c.shape) for c in consts]
    return pl.pallas_call(
        _mixer_kernel,
        grid=(B, T // TM_MIX),
        in_specs=[row_spec] + const_specs,
        out_specs=row_spec,
        out_shape=jax.ShapeDtypeStruct(x.shape, x.dtype),
        scratch_shapes=[
            pltpu.VMEM(w_in.shape, jnp.bfloat16),
            pltpu.VMEM((ATTN_WIDTH, D), jnp.bfloat16),
            pltpu.VMEM((ATTN_WIDTH, D), jnp.bfloat16),
            pltpu.VMEM(w_out.shape, jnp.bfloat16),
            pltpu.VMEM((2, W_IN_CHUNK, w_in.shape[1]), jnp.float32),
            pltpu.VMEM((2, W_OUT_CHUNK, w_out.shape[1]), jnp.float32),
            pltpu.SemaphoreType.DMA((2,)),
            pltpu.VMEM((HALO_H + TM_MIX, CONV_WIDTH), jnp.float32),
            pltpu.VMEM((SUBLANES - 1, HSHIFT_ROWS, CONV_WIDTH), jnp.float32),
            pltpu.VMEM((TM_MIX, CONV_WIDTH), jnp.float32),
            pltpu.VMEM((ATTN_WIDTH, TM_MIX), jnp.bfloat16),
            pltpu.VMEM((N_PAIRS, HALO_KV + TM_MIX, PAIR_W), jnp.bfloat16),
            pltpu.VMEM((ATTN_WIDTH, HALO_KV + TM_MIX), jnp.bfloat16),
            pltpu.VMEM((N_HEADS, QW // HALF_W, KW, HALF_W), jnp.float32),
            pltpu.VMEM((N_QUADS, N_LANE_BLOCKS, KW, HALF_W), jnp.float32),
            pltpu.VMEM((N_QUADS, SUBLANES, 2 * QW), jnp.float32),
            pltpu.VMEM((N_QUADS, N_LANE_BLOCKS, KW, HALF_W), jnp.bfloat16),
            pltpu.VMEM((ATTN_WIDTH, TM_MIX), jnp.float32),
            pltpu.VMEM((TM_MIX, CONV_WIDTH + ATTN_WIDTH), jnp.bfloat16),
        ],
        compiler_params=pltpu.CompilerParams(
            dimension_semantics=("arbitrary", "arbitrary"), vmem_limit_bytes=VMEM_LIMIT),
        name="mixer",
    )(x, *consts)


def _ffn(h, g_pre, w_up, dw_w, dw_b, w_down, g_post):
    B, T, D = h.shape
    n_t = T // TM_FFN
    tiles = (B * n_t, SUBLANES, FFN_SEG, D)
    hbm_spec = pl.BlockSpec(memory_space=pl.ANY)
    out = pl.pallas_call(
        _ffn_kernel,
        grid=(B, n_t),
        in_specs=[hbm_spec,
                  _const_spec(g_pre.shape), hbm_spec, _const_spec(dw_w.shape),
                  _const_spec(dw_b.shape), hbm_spec, _const_spec(g_post.shape)],
        out_specs=hbm_spec,
        out_shape=jax.ShapeDtypeStruct(tiles, h.dtype),
        scratch_shapes=[
            pltpu.VMEM(w_up.shape, jnp.bfloat16),
            pltpu.VMEM(w_down.shape, jnp.bfloat16),
            pltpu.VMEM((2, W_UP_CHUNK, w_up.shape[1]), jnp.float32),
            pltpu.VMEM((2, W_DOWN_CHUNK, w_down.shape[1]), jnp.float32),
            pltpu.SemaphoreType.DMA((2,)),
            pltpu.VMEM((2, FFN_SEG, SUBLANES, D), jnp.float32),
            pltpu.VMEM((2, FFN_SEG, SUBLANES, D), jnp.float32),
            pltpu.SemaphoreType.DMA((2,)),
            pltpu.SemaphoreType.DMA((2,)),
            pltpu.VMEM((FFN_HALO, 2 * D_FF), jnp.float32),
            pltpu.VMEM((2, FFN_HALO + TM_FFN, FFN_CW), jnp.float32),
            pltpu.VMEM((2, FFN_HALO + TM_FFN, FFN_CW), jnp.float32),
            pltpu.VMEM((TM_FFN, D_FF), jnp.bfloat16),
        ],
        compiler_params=pltpu.CompilerParams(
            dimension_semantics=("arbitrary", "arbitrary"), vmem_limit_bytes=VMEM_LIMIT),
        name="ffn",
    )(h.reshape(tiles), g_pre, w_up, dw_w, dw_b, w_down, g_post)
    return out.reshape(B, T, D)


def kernel(x, norm_mix_pre, w_in, conv_dw_w, conv_dw_b, conv_ln_g, conv_ln_b, rel_bias, w_out,
           norm_mix_post, norm_ffn_pre, w_up, ffn_dw_w, ffn_dw_b, w_down, norm_ffn_post):
    def rows(v):
        return jnp.broadcast_to(v[None], (SUBLANES, v.shape[0]))

    h = x
    for l in range(norm_mix_pre.shape[0]):
        h = _mixer(h, rows(norm_mix_pre[l]), w_in[l],
                   jnp.broadcast_to(conv_dw_w[l][:, None, :], (CONV_KERNEL, SUBLANES, CONV_WIDTH)),
                   rows(conv_dw_b[l]), rows(conv_ln_g[l]), rows(conv_ln_b[l]),
                   _rel_distance_row(rel_bias[l]), w_out[l], rows(norm_mix_post[l]))
        taps = jnp.pad(ffn_dw_w[l], ((0, SUBLANES - FFN_CONV_KERNEL), (0, 0)))
        h = _ffn(h, rows(norm_ffn_pre[l]), w_up[l], taps,
                 rows(ffn_dw_b[l]), w_down[l], rows(norm_ffn_post[l]))
    return h
```
